```python
import jax, jax.numpy as jnp
from jax import lax
import numpy as np

D_MODEL = 2048
BATCH = 8
SEQ = 2048
DEPTH = 2

N_MIXERS = 4
GROUP_WIDTH = D_MODEL // N_MIXERS
HEAD_DIM = 128
N_HEADS = GROUP_WIDTH // HEAD_DIM
ROT_DIM = HEAD_DIM // 4
ROPE_THETA = 500000.0

NSA_KV_BRANCHES = 3
CMP_BLOCK = 32
CMP_STRIDE = 16
CMP_HIDDEN = 128
SEL_BLOCK = 64
SEL_TOPK = 8
WIN = 256
WIN_BLOCK = 128
NSA_Q_CHUNK = 128
FORCE_SCORE = 1e9

MLSTM_CHUNK = 64

CONV_WIDTH = 4
LRU_C = 8.0
LRU_BLOCKS = N_HEADS
LRU_BLOCK_WIDTH = GROUP_WIDTH // LRU_BLOCKS

MOBA_BLOCK = 256
MOBA_TOPK = 3
MOBA_Q_CHUNK = 32

DEEPNORM_ALPHA = (2 * DEPTH) ** 0.25
DEEPNORM_BETA = (8 * DEPTH) ** -0.25
NEG = -1e30
LN_EPS = 1e-5

IN_SPLITS = (
    ('nsa_q', GROUP_WIDTH),
    ('nsa_kv', 2 * NSA_KV_BRANCHES * HEAD_DIM),
    ('nsa_gate', NSA_KV_BRANCHES * N_HEADS),
    ('nsa_z', GROUP_WIDTH),
    ('mlstm_qkv', 3 * GROUP_WIDTH),
    ('mlstm_if', 2 * N_HEADS),
    ('mlstm_o', GROUP_WIDTH),
    ('mlstm_z', GROUP_WIDTH),
    ('lru_x', GROUP_WIDTH),
    ('lru_z', GROUP_WIDTH),
    ('moba_qkv', 3 * GROUP_WIDTH),
    ('moba_z', GROUP_WIDTH),
)
IN_WIDTH = sum(w for _, w in IN_SPLITS)
MIX_WIDTH = N_MIXERS * GROUP_WIDTH

kernel_name = 'hybrid_nsa_mlstm_rglru_moba_deepnorm'


def split_columns(h):
    parts, start = {}, 0
    for name, width in IN_SPLITS:
        parts[name] = h[..., start:start + width]
        start += width
    return parts


def layer_norm(x, g, b):
    xf = x.astype(jnp.float32)
    mu = jnp.mean(xf, -1, keepdims=True)
    var = jnp.mean(jnp.square(xf - mu), -1, keepdims=True)
    y = (xf - mu) * lax.rsqrt(var + LN_EPS) * g.astype(jnp.float32) + b.astype(jnp.float32)
    return y.astype(x.dtype)


def partial_rotary(t, pos):
    half = ROT_DIM // 2
    inv_freq = jnp.power(ROPE_THETA, -jnp.arange(half, dtype=jnp.float32) * (2.0 / ROT_DIM))
    ang = pos.astype(jnp.float32)[:, None] * inv_freq[None, :]
    cos = jnp.cos(ang)[:, None, :]
    sin = jnp.sin(ang)[:, None, :]
    tr = t[..., :ROT_DIM].astype(jnp.float32)
    t1, t2 = tr[..., :half], tr[..., half:]
    rot = jnp.concatenate([t1 * cos - t2 * sin, t2 * cos + t1 * sin], axis=-1)
    return jnp.concatenate([rot.astype(t.dtype), t[..., ROT_DIM:]], axis=-1)


def masked_softmax(s, mask):
    s = jnp.where(mask, s.astype(jnp.float32), NEG)
    return jax.nn.softmax(s, axis=-1) * mask


def nsa_mixer(q, k_cmp, v_cmp, k_sel, v_sel, k_win, v_win, gates, cmp_w1, cmp_w2, cmp_pe):
    B, S, H, dh = q.shape
    scale = dh ** -0.5
    f32 = jnp.float32

    n_cmp = (S - CMP_BLOCK) // CMP_STRIDE + 1
    cmp_start = jnp.arange(n_cmp) * CMP_STRIDE
    idx = cmp_start[:, None] + jnp.arange(CMP_BLOCK)[None, :]

    def compress(t, w1, w2, pe):
        blocks = t[:, idx] + pe
        hid = jax.nn.silu(blocks.reshape(B, n_cmp, CMP_BLOCK * dh) @ w1)
        return hid @ w2

    kc = compress(k_cmp, cmp_w1[0], cmp_w2[0], cmp_pe[0])
    vc = compress(v_cmp, cmp_w1[1], cmp_w2[1], cmp_pe[1])
    cmp_end = cmp_start + CMP_BLOCK - 1

    n_sel = S // SEL_BLOCK
    sel_start = jnp.arange(n_sel) * SEL_BLOCK
    overlap = ((cmp_start[:, None] < sel_start[None, :] + SEL_BLOCK)
               & (cmp_start[:, None] + CMP_BLOCK > sel_start[None, :])).astype(f32)
    k_sel_blocks = k_sel.reshape(B, n_sel, SEL_BLOCK, dh)
    v_sel_blocks = v_sel.reshape(B, n_sel, SEL_BLOCK, dh)
    top_n = min(SEL_TOPK, n_sel)
    sel_ids = jnp.arange(n_sel)
    b_idx = jnp.arange(B)[:, None, None]

    QC = NSA_Q_CHUNK
    n_chunks = S // QC
    q_chunks = q.reshape(B, n_chunks, QC, H, dh).transpose(1, 0, 2, 3, 4)

    def chunk_fn(args):
        qc, c = args
        t = c * QC + jnp.arange(QC)
        s_c = jnp.einsum('bqhd,bnd->bhqn', qc, kc) * scale
        p_c = masked_softmax(s_c, cmp_end[None, :] <= t[:, None])
        o_c = jnp.einsum('bhqn,bnd->bqhd', p_c.astype(vc.dtype), vc)
        imp = jnp.einsum('bhqn,nj->bqj', p_c, overlap)
        cur = t // SEL_BLOCK
        forced = (sel_ids[None] == 0) | (sel_ids[None] == cur[:, None]) | (sel_ids[None] == cur[:, None] - 1)
        valid = sel_start[None] <= t[:, None]
        imp = jnp.where(forced, FORCE_SCORE, jnp.where(valid, imp, NEG))
        _, sel = lax.top_k(imp, top_n)
        ks = k_sel_blocks[b_idx, sel]
        vs = v_sel_blocks[b_idx, sel]
        s_s = jnp.einsum('bqhd,bqnkd->bhqnk', qc, ks).reshape(B, H, QC, top_n * SEL_BLOCK) * scale
        key_pos = sel[..., None] * SEL_BLOCK + jnp.arange(SEL_BLOCK)
        mask_s = (key_pos <= t[None, :, None, None]).reshape(B, 1, QC, top_n * SEL_BLOCK)
        p_s = masked_softmax(s_s, mask_s)
        o_s = jnp.einsum('bhqm,bqmd->bqhd', p_s.astype(vs.dtype), vs.reshape(B, QC, top_n * SEL_BLOCK, dh))
        return o_c, o_s

    o_c, o_s = lax.map(chunk_fn, (q_chunks, jnp.arange(n_chunks)))
    o_c = o_c.transpose(1, 0, 2, 3, 4).reshape(B, S, H, dh)
    o_s = o_s.transpose(1, 0, 2, 3, 4).reshape(B, S, H, dh)

    nwb = S // WIN_BLOCK
    n_prev = WIN // WIN_BLOCK

    def banded(t):
        tb = t.reshape(B, nwb, WIN_BLOCK, dh)
        shifted = [jnp.pad(tb, ((0, 0), (s, 0), (0, 0), (0, 0)))[:, :nwb] for s in range(n_prev, -1, -1)]
        return jnp.concatenate(shifted, axis=2)

    kw, vw = banded(k_win), banded(v_win)
    qb = q.reshape(B, nwb, WIN_BLOCK, H, dh)
    s_w = jnp.einsum('bnqhd,bnkd->bhnqk', qb, kw) * scale
    qpos = jnp.arange(S).reshape(nwb, WIN_BLOCK)
    kpos = qpos[:, :1] - n_prev * WIN_BLOCK + jnp.arange((n_prev + 1) * WIN_BLOCK)[None, :]
    diff = qpos[:, :, None] - kpos[:, None, :]
    mask_w = (kpos[:, None, :] >= 0) & (diff >= 0) & (diff < WIN)
    p_w = masked_softmax(s_w, mask_w)
    o_w = jnp.einsum('bhnqk,bnkd->bnqhd', p_w.astype(vw.dtype), vw).reshape(B, S, H, dh)

    o = gates[..., 0:1] * o_c + gates[..., 1:2] * o_s + gates[..., 2:3] * o_w
    return o.reshape(B, S, H * dh)


def mlstm_mixer(q, k, v, i_pre, f_pre, o_pre, norm_g):
    B, S, H, dh = q.shape
    L = MLSTM_CHUNK
    nc = S // L
    f32 = jnp.float32

    def heads_to_chunks(t):
        return t.astype(f32).reshape(B, nc, L, H, dh).transpose(0, 3, 1, 2, 4)

    def gate_to_chunks(t):
        return t.astype(f32).reshape(B, nc, L, H).transpose(0, 3, 1, 2)

    qc = heads_to_chunks(q)
    kc = heads_to_chunks(k) * (dh ** -0.5)
    vc = heads_to_chunks(v)
    ig = gate_to_chunks(i_pre)
    log_f = jax.nn.log_sigmoid(gate_to_chunks(f_pre))
    b = jnp.cumsum(log_f, axis=-1)
    causal = jnp.tril(jnp.ones((L, L), dtype=bool))
    d_log = jnp.where(causal, b[..., :, None] - b[..., None, :] + ig[..., None, :], NEG)
    m_intra = jnp.max(d_log, axis=-1)

    b_last = b[..., -1]
    w_log = b_last[..., None] - b + ig
    m_loc = jnp.max(w_log, axis=-1)
    e = jnp.exp(w_log - m_loc[..., None])
    g_c = jnp.einsum('bhcld,bhcle->bhcde', e[..., None] * kc, vc)
    g_n = jnp.einsum('bhcl,bhcld->bhcd', e, kc)

    def step(carry, xs):
        c_st, n_st, m_st = carry
        a, ml, gc, gn = xs
        m_new = jnp.maximum(a + m_st, ml)
        s_old = jnp.exp(a + m_st - m_new)
        s_new = jnp.exp(ml - m_new)
        c_next = s_old[..., None, None] * c_st + s_new[..., None, None] * gc
        n_next = s_old[..., None] * n_st + s_new[..., None] * gn
        return (c_next, n_next, m_new), (c_st, n_st, m_st)

    init = (jnp.zeros((B, H, dh, dh), f32), jnp.zeros((B, H, dh), f32), jnp.full((B, H), NEG, f32))
    xs = (jnp.moveaxis(b_last, -1, 0), jnp.moveaxis(m_loc, -1, 0),
          jnp.moveaxis(g_c, 2, 0), jnp.moveaxis(g_n, 2, 0))
    _, (c_prev, n_prev, m_prev) = lax.scan(step, init, xs)
    c_prev = jnp.moveaxis(c_prev, 0, 2)
    n_prev = jnp.moveaxis(n_prev, 0, 2)
    m_prev = jnp.moveaxis(m_prev, 0, -1)

    m_inter = b + m_prev[..., None]
    m_t = jnp.maximum(m_inter, m_intra)
    w_inter = jnp.exp(m_inter - m_t)
    qk = jnp.einsum('bhcld,bhcsd->bhcls', qc, kc) * jnp.exp(d_log - m_t[..., None])
    num = (jnp.einsum('bhcls,bhcse->bhcle', qk, vc)
           + w_inter[..., None] * jnp.einsum('bhcld,bhcde->bhcle', qc, c_prev))
    den = jnp.sum(qk, axis=-1) + w_inter * jnp.einsum('bhcld,bhcd->bhcl', qc, n_prev)
    h = num / jnp.maximum(jnp.abs(den), jnp.exp(-m_t))[..., None]
    h = h.transpose(0, 2, 3, 1, 4).reshape(B, S, H, dh)
    h = jax.nn.sigmoid(o_pre.astype(f32)).reshape(B, S, H, dh) * h
    mu = jnp.mean(h, -1, keepdims=True)
    var = jnp.mean(jnp.square(h - mu), -1, keepdims=True)
    return ((h - mu) * lax.rsqrt(var + LN_EPS)).reshape(B, S, H * dh) * norm_g.astype(f32)


def rglru_mixer(xr, conv_w, conv_b, gate_w, gate_b, lam):
    B, S, W = xr.shape
    f32 = jnp.float32
    u = lax.conv_general_dilated(xr, conv_w[:, None, :], window_strides=(1,),
                                 padding=[(CONV_WIDTH - 1, 0)],
                                 dimension_numbers=('NWC', 'WIO', 'NWC'),
                                 feature_group_count=W) + conv_b
    ub = u.reshape(B, S, LRU_BLOCKS, LRU_BLOCK_WIDTH)
    r = jax.nn.sigmoid((jnp.einsum('bsnd,nde->bsne', ub, gate_w[0]).reshape(B, S, W) + gate_b[0]).astype(f32))
    i = jax.nn.sigmoid((jnp.einsum('bsnd,nde->bsne', ub, gate_w[1]).reshape(B, S, W) + gate_b[1]).astype(f32))
    log_a = -LRU_C * r * jax.nn.softplus(-lam.astype(f32))
    a = jnp.exp(log_a)
    bx = jnp.sqrt(-jnp.expm1(2.0 * log_a)) * (i * u.astype(f32))

    def combine(lhs, rhs):
        a1, b1 = lhs
        a2, b2 = rhs
        return a1 * a2, a2 * b1 + b2

    _, h = lax.associative_scan(combine, (a, bx), axis=1)
    return h


def moba_mixer(q, k, v):
    B, S, H, dh = q.shape
    scale = dh ** -0.5
    nb = -(-S // MOBA_BLOCK)
    Sp = nb * MOBA_BLOCK
    pad = ((0, 0), (0, Sp - S), (0, 0), (0, 0))
    qh = jnp.pad(q, pad).transpose(0, 2, 1, 3)
    kb = jnp.pad(k, pad).transpose(0, 2, 1, 3).reshape(B, H, nb, MOBA_BLOCK, dh)
    vb = jnp.pad(v, pad).transpose(0, 2, 1, 3).reshape(B, H, nb, MOBA_BLOCK, dh)
    kmean = jnp.mean(kb.astype(jnp.float32), axis=3).astype(kb.dtype)
    top_n = min(MOBA_TOPK, nb)
    QC = MOBA_Q_CHUNK
    n_chunks = Sp // QC
    q_chunks = qh.reshape(B, H, n_chunks, QC, dh).transpose(2, 0, 1, 3, 4)
    b_idx = jnp.arange(B)[:, None, None, None]
    h_idx = jnp.arange(H)[None, :, None, None]
    blk_ids = jnp.arange(nb)

    def chunk_fn(args):
        qc, c = args
        t = c * QC + jnp.arange(QC)
        cur = (c * QC) // MOBA_BLOCK
        gs = jnp.einsum('bhqd,bhnd->bhqn', qc, kmean).astype(jnp.float32)
        gs = jnp.where(blk_ids < cur, gs, NEG)
        _, sel = lax.top_k(gs, top_n)
        sel_valid = sel < cur
        ks = kb[b_idx, h_idx, sel]
        vs = vb[b_idx, h_idx, sel]
        s_sel = jnp.einsum('bhqd,bhqnkd->bhqnk', qc, ks).reshape(B, H, QC, top_n * MOBA_BLOCK)
        m_sel = jnp.broadcast_to(sel_valid[..., None], (B, H, QC, top_n, MOBA_BLOCK)).reshape(B, H, QC, top_n * MOBA_BLOCK)
        k_own = lax.dynamic_index_in_dim(kb, cur, axis=2, keepdims=False)
        v_own = lax.dynamic_index_in_dim(vb, cur, axis=2, keepdims=False)
        s_own = jnp.einsum('bhqd,bhkd->bhqk', qc, k_own)
        own_pos = cur * MOBA_BLOCK + jnp.arange(MOBA_BLOCK)
        m_own = jnp.broadcast_to(own_pos[None, :] <= t[:, None], (B, H, QC, MOBA_BLOCK))
        s = jnp.concatenate([s_sel, s_own], axis=-1) * scale
        mask = jnp.concatenate([m_sel, m_own], axis=-1)
        p = masked_softmax(s, mask).astype(vb.dtype)
        n_s = top_n * MOBA_BLOCK
        o = (jnp.einsum('bhqm,bhqmd->bhqd', p[..., :n_s], vs.reshape(B, H, QC, n_s, dh))
             + jnp.einsum('bhqk,bhkd->bhqd', p[..., n_s:], v_own))
        return o

    o = lax.map(chunk_fn, (q_chunks, jnp.arange(n_chunks)))
    o = o.transpose(1, 2, 0, 3, 4).reshape(B, H, Sp, dh)[:, :, :S]
    return o.transpose(0, 2, 1, 3).reshape(B, S, H * dh)


def hybrid_layer(x, pos, w_in, nsa_cmp_w1, nsa_cmp_w2, nsa_cmp_pe, mlstm_i_bias, mlstm_f_bias,
                 mlstm_norm_g, lru_conv_w, lru_conv_b, lru_gate_w, lru_gate_b, lru_lambda,
                 w_out, ln_g, ln_b):
    B, S, _ = x.shape
    H, dh = N_HEADS, HEAD_DIM
    parts = split_columns(x @ w_in)

    qa = partial_rotary(parts['nsa_q'].reshape(B, S, H, dh), pos)
    kv = parts['nsa_kv'].reshape(B, S, 2, NSA_KV_BRANCHES, dh)
    ka = partial_rotary(kv[:, :, 0], pos)
    va = kv[:, :, 1]
    gates = jax.nn.sigmoid(parts['nsa_gate'].reshape(B, S, H, NSA_KV_BRANCHES))
    y_a = nsa_mixer(qa, ka[:, :, 0], va[:, :, 0], ka[:, :, 1], va[:, :, 1], ka[:, :, 2], va[:, :, 2],
                    gates, nsa_cmp_w1, nsa_cmp_w2, nsa_cmp_pe)
    y_a = y_a.astype(x.dtype) * jax.nn.silu(parts['nsa_z'])

    qkv_b = parts['mlstm_qkv'].reshape(B, S, 3, H, dh)
    if_b = parts['mlstm_if'].reshape(B, S, 2, H)
    y_b = mlstm_mixer(qkv_b[:, :, 0], qkv_b[:, :, 1], qkv_b[:, :, 2],
                      if_b[:, :, 0] + mlstm_i_bias, if_b[:, :, 1] + mlstm_f_bias,
                      parts['mlstm_o'], mlstm_norm_g)
    y_b = y_b.astype(x.dtype) * jax.nn.silu(parts['mlstm_z'])

    y_c = rglru_mixer(parts['lru_x'], lru_conv_w, lru_conv_b, lru_gate_w, lru_gate_b, lru_lambda)
    y_c = y_c.astype(x.dtype) * jax.nn.silu(parts['lru_z'])

    qkv_d = parts['moba_qkv'].reshape(B, S, 3, H, dh)
    qd = partial_rotary(qkv_d[:, :, 0], pos)
    kd = partial_rotary(qkv_d[:, :, 1], pos)
    y_d = moba_mixer(qd, kd, qkv_d[:, :, 2])
    y_d = y_d.astype(x.dtype) * jax.nn.silu(parts['moba_z'])

    y = jnp.concatenate([y_a, y_b, y_c, y_d], axis=-1) @ w_out
    return layer_norm(DEEPNORM_ALPHA * x + y, ln_g, ln_b)


def setup_inputs(seed: int = 0) -> dict:
    key = jax.random.key(seed)
    ks = jax.random.split(key, 16)
    nrm = jax.random.normal
    x = nrm(ks[0], (BATCH, SEQ, D_MODEL), jnp.float32)
    w_in = nrm(ks[1], (DEPTH, D_MODEL, IN_WIDTH), jnp.float32) * D_MODEL ** -0.5
    nsa_cmp_w1 = nrm(ks[2], (DEPTH, 2, CMP_BLOCK * HEAD_DIM, CMP_HIDDEN), jnp.float32) * (CMP_BLOCK * HEAD_DIM) ** -0.5
    nsa_cmp_w2 = nrm(ks[3], (DEPTH, 2, CMP_HIDDEN, HEAD_DIM), jnp.float32) * CMP_HIDDEN ** -0.5
    nsa_cmp_pe = nrm(ks[4], (DEPTH, 2, CMP_BLOCK, HEAD_DIM), jnp.float32) * 0.1
    mlstm_i_bias = nrm(ks[5], (DEPTH, N_HEADS), jnp.float32) * 0.1
    mlstm_f_bias = jnp.linspace(3.0, 6.0, N_HEADS, dtype=jnp.float32)[None, :] + nrm(ks[6], (DEPTH, N_HEADS), jnp.float32) * 0.1
    mlstm_norm_g = 1.0 + nrm(ks[7], (DEPTH, GROUP_WIDTH), jnp.float32) * 0.02
    lru_conv_w = nrm(ks[8], (DEPTH, CONV_WIDTH, GROUP_WIDTH), jnp.float32) * CONV_WIDTH ** -0.5
    lru_conv_b = nrm(ks[9], (DEPTH, GROUP_WIDTH), jnp.float32) * 0.01
    lru_gate_w = nrm(ks[10], (DEPTH, 2, LRU_BLOCKS, LRU_BLOCK_WIDTH, LRU_BLOCK_WIDTH), jnp.float32) * LRU_BLOCK_WIDTH ** -0.5
    lru_gate_b = nrm(ks[11], (DEPTH, 2, GROUP_WIDTH), jnp.float32) * 0.01
    a_c = jax.random.uniform(ks[12], (DEPTH, GROUP_WIDTH), jnp.float32, minval=0.9, maxval=0.999)
    s_lam = a_c ** (1.0 / LRU_C)
    lru_lambda = jnp.log(s_lam) - jnp.log1p(-s_lam)
    w_out = nrm(ks[13], (DEPTH, MIX_WIDTH, D_MODEL), jnp.float32) * (MIX_WIDTH ** -0.5) * DEEPNORM_BETA
    ln_g = 1.0 + nrm(ks[14], (DEPTH, D_MODEL), jnp.float32) * 0.02
    ln_b = nrm(ks[15], (DEPTH, D_MODEL), jnp.float32) * 0.02
    return {'x': x, 'w_in': w_in, 'nsa_cmp_w1': nsa_cmp_w1, 'nsa_cmp_w2': nsa_cmp_w2,
            'nsa_cmp_pe': nsa_cmp_pe, 'mlstm_i_bias': mlstm_i_bias, 'mlstm_f_bias': mlstm_f_bias,
            'mlstm_norm_g': mlstm_norm_g, 'lru_conv_w': lru_conv_w, 'lru_conv_b': lru_conv_b,
            'lru_gate_w': lru_gate_w, 'lru_gate_b': lru_gate_b, 'lru_lambda': lru_lambda,
            'w_out': w_out, 'ln_g': ln_g, 'ln_b': ln_b}


def reference(x, w_in, nsa_cmp_w1, nsa_cmp_w2, nsa_cmp_pe, mlstm_i_bias, mlstm_f_bias, mlstm_norm_g,
              lru_conv_w, lru_conv_b, lru_gate_w, lru_gate_b, lru_lambda, w_out, ln_g, ln_b):
    pos = jnp.arange(x.shape[1], dtype=jnp.int32)
    for l in range(DEPTH):
        x = hybrid_layer(x, pos, w_in[l], nsa_cmp_w1[l], nsa_cmp_w2[l], nsa_cmp_pe[l],
                         mlstm_i_bias[l], mlstm_f_bias[l], mlstm_norm_g[l],
                         lru_conv_w[l], lru_conv_b[l], lru_gate_w[l], lru_gate_b[l], lru_lambda[l],
                         w_out[l], ln_g[l], ln_b[l])
    return x
```

```python
import functools

import jax
import jax.numpy as jnp
from jax import lax
from jax.experimental import pallas as pl
from jax.experimental.pallas import tpu as pltpu

F32 = jnp.float32
BF16 = jnp.bfloat16

D_MODEL = 2048
DEPTH = 2
GROUP = 512
DH = 128
NH = 4
ROT_DIM = 32
ROPE_THETA = 500000.0

CMP_BLOCK = 32
CMP_STRIDE = 16
SEL_BLOCK = 64
SEL_TOPK = 8
WIN = 256
FORCE_SCORE = 1e9

MLSTM_CHUNK = 64
LRU_C = 8.0
MOBA_BLOCK = 256
MOBA_TOPK = 3

DEEPNORM_ALPHA = (2 * DEPTH) ** 0.25
NEG = -1e30
LN_EPS = 1e-5
SCALE = DH ** -0.5

LANES = 128
VMEM_LIMIT = 56 * 1024 * 1024

_OFF = {}
_o = 0
for _name, _w in (('nsa_q', 512), ('nsa_kv', 768), ('nsa_gate', 12), ('nsa_z', 512), ('mlstm_qkv', 1536),
                  ('mlstm_if', 8), ('mlstm_o', 512), ('mlstm_z', 512), ('lru_x', 512), ('lru_z', 512),
                  ('moba_qkv', 1536), ('moba_z', 512)):
    _OFF[_name] = _o
    _o += _w

GD_I = 12
GD_F = 16

NT = (((1,), (1,)), ((), ()))
TN = (((0,), (0,)), ((), ()))


def _cparams(sem):
    return pltpu.CompilerParams(dimension_semantics=sem, vmem_limit_bytes=VMEM_LIMIT)


def _dot(a, b):
    return jnp.dot(a, b, preferred_element_type=F32)


def _dot_nt(a, b):
    return lax.dot_general(a, b, NT, preferred_element_type=F32)


def _split_hi_lo(a):
    hi = a.astype(BF16)
    lo = (a - hi.astype(F32)).astype(BF16)
    return hi, lo


def _silu(x):
    return x * jax.nn.sigmoid(x)


def _log_sigmoid(x):
    return jnp.minimum(x, 0.0) - jnp.log(1.0 + jnp.exp(-jnp.abs(x)))


def _masked_exp(s, mask):
    s = jnp.where(mask, s, NEG)
    m = jnp.max(s, axis=-1, keepdims=True)
    e = jnp.exp(s - m)
    inv = 1.0 / jnp.sum(e, axis=-1, keepdims=True)
    return jnp.where(mask, e, 0.0), inv


def _proj_kernel(x_ref, w_ref, *rest, rotary):
    if rotary:
        cos_ref, sin_ref, o_ref = rest
    else:
        (o_ref,) = rest
    acc = _dot(x_ref[...], w_ref[...])
    if not rotary:
        o_ref[...] = acc.astype(o_ref.dtype)
        return
    c = cos_ref[...]
    s = sin_ref[...]
    lane = lax.broadcasted_iota(jnp.int32, c.shape, 1)
    half = ROT_DIM // 2
    for h in range(acc.shape[1] // DH):
        t = acc[:, h * DH:(h + 1) * DH]
        swapped = jnp.where(lane < half, pltpu.roll(t, DH - half, 1), pltpu.roll(t, half, 1))
        o_ref[:, h * DH:(h + 1) * DH] = (t * c + swapped * s).astype(o_ref.dtype)


def _project(xb, w, out_dtype, tm, tn, rope=None, seq=None):
    m, k = xb.shape
    n = w.shape[1]
    tn = min(tn, n)
    in_specs = [pl.BlockSpec((tm, k), lambda i, j: (i, 0)),
                pl.BlockSpec((k, tn), lambda i, j: (0, j))]
    args = [xb, w]
    if rope is not None:
        nrep = seq // tm
        in_specs += [pl.BlockSpec((tm, DH), lambda i, j: (i % nrep, 0))] * 2
        args += list(rope)
    return pl.pallas_call(
        functools.partial(_proj_kernel, rotary=rope is not None),
        out_shape=jax.ShapeDtypeStruct((m, n), out_dtype),
        grid=(m // tm, n // tn),
        in_specs=in_specs,
        out_specs=pl.BlockSpec((tm, tn), lambda i, j: (i, j)),
        compiler_params=_cparams(("parallel", "arbitrary")),
        name="in_proj_rot" if rope is not None else "in_proj",
    )(*args)


def _compress_kernel(tk_ref, tv_ref, w1_ref, w2_ref, pe_ref, kc_ref, vc_ref, *, nb):
    half = (CMP_BLOCK // 2) * DH
    for idx, (t_ref, o_ref) in enumerate(((tk_ref, kc_ref), (tv_ref, vc_ref))):
        t = t_ref[...].astype(F32)
        lo = (t + pe_ref[idx, 0:1, :]).astype(BF16)
        hi = (t + pe_ref[idx, 1:2, :]).astype(BF16)
        a = _dot(lo, w1_ref[idx, 0:half, :])
        b = _dot(hi, w1_ref[idx, half:2 * half, :])
        hid = _silu(a + pltpu.roll(b, nb - 1, 0))
        out = _dot(hid.astype(BF16), w2_ref[idx])
        o_ref[...] = jnp.zeros(o_ref.shape, o_ref.dtype)
        o_ref[0:nb, :] = out.astype(o_ref.dtype)


def _compress(tk, tv, w1, w2, pe):
    b, nb, width = tk.shape
    blk = pl.BlockSpec((None, nb, width), lambda i: (i, 0, 0))
    full = lambda a: pl.BlockSpec(a.shape, lambda i: (0,) * a.ndim)
    out = jax.ShapeDtypeStruct((b, LANES, DH), BF16)
    return pl.pallas_call(
        functools.partial(_compress_kernel, nb=nb),
        out_shape=(out, out),
        grid=(b,),
        in_specs=[blk, blk, full(w1), full(w2), full(pe)],
        out_specs=(pl.BlockSpec((None, LANES, DH), lambda i: (i, 0, 0)),) * 2,
        compiler_params=_cparams(("parallel",)),
        name="nsa_compress",
    )(tk, tv, w1, w2, pe)


def _nsa_kernel(q_ref, kc_ref, vc_ref, ks_ref, vs_ref, kw_ref, vw_ref, g_ref, z_ref, ov_ref, e_ref,
                o_ref, *, seq, tq):
    i = pl.program_id(1)
    rows = NH * tq
    n_cmp = seq // CMP_STRIDE - 1
    n_sel = seq // SEL_BLOCK
    q = q_ref[...]
    qs = jnp.concatenate([q[:, h * DH:(h + 1) * DH] for h in range(NH)], axis=0)

    def tok(shape):
        return i * tq + (lax.broadcasted_iota(jnp.int32, shape, 0) & (tq - 1))

    lane = lax.broadcasted_iota(jnp.int32, (rows, LANES), 1)
    t_r = tok((rows, LANES))
    mask_c = jnp.logical_and(lane * CMP_STRIDE + (CMP_BLOCK - 1) <= t_r, lane < n_cmp)
    e_c, inv_c = _masked_exp(_dot_nt(qs, kc_ref[...]) * SCALE, mask_c)
    p_c = e_c * inv_c
    o_c = _dot(p_c.astype(BF16), vc_ref[...])

    p_sum = p_c[0:tq]
    for h in range(1, NH):
        p_sum = p_sum + p_c[h * tq:(h + 1) * tq]
    p_hi, p_lo = _split_hi_lo(p_sum)
    imp = _dot(p_hi, ov_ref[...]) + _dot(p_lo, ov_ref[...])
    lane_q = lax.broadcasted_iota(jnp.int32, (tq, LANES), 1)
    t_q = tok((tq, LANES))
    cur = t_q >> 6
    forced = jnp.logical_or(lane_q == 0, jnp.logical_or(lane_q == cur, lane_q == cur - 1))
    valid = lane_q * SEL_BLOCK <= t_q
    score = jnp.where(forced, FORCE_SCORE, jnp.where(valid, imp, NEG))
    score = jnp.where(lane_q < n_sel, score, -jnp.inf)
    lane_f = lane_q.astype(F32)
    sel = jnp.zeros((tq, LANES), F32)
    for _ in range(min(SEL_TOPK, n_sel)):
        best = jnp.max(score, axis=-1, keepdims=True)
        first = jnp.min(jnp.where(score == best, lane_f, float(LANES)), axis=-1, keepdims=True)
        pick = lane_f == first
        sel = jnp.where(pick, 1.0, sel)
        score = jnp.where(pick, -jnp.inf, score)

    key_sel = _dot(sel.astype(BF16), e_ref[...])
    key_sel = jnp.concatenate([key_sel] * NH, axis=0)
    kpos = lax.broadcasted_iota(jnp.int32, (rows, seq), 1)
    mask_s = jnp.logical_and(key_sel > 0.5, kpos <= tok((rows, seq)))
    e_s, inv_s = _masked_exp(_dot_nt(qs, ks_ref[...]) * SCALE, mask_s)
    o_s = _dot(e_s.astype(BF16), vs_ref[...]) * inv_s

    nwin = min(3 * tq, seq)
    start = pl.multiple_of(jnp.maximum(i - 2, 0) * tq, tq)
    kw = kw_ref[pl.ds(start, nwin), :]
    vw = vw_ref[pl.ds(start, nwin), :]
    diff = tok((rows, nwin)) - (start + lax.broadcasted_iota(jnp.int32, (rows, nwin), 1))
    mask_w = jnp.logical_and(diff >= 0, diff < WIN)
    e_w, inv_w = _masked_exp(_dot_nt(qs, kw) * SCALE, mask_w)
    o_w = _dot(e_w.astype(BF16), vw) * inv_w

    g = jax.nn.sigmoid(g_ref[...])
    for h in range(NH):
        r = slice(h * tq, (h + 1) * tq)
        c = slice(h * DH, (h + 1) * DH)
        mix = (g[:, 3 * h:3 * h + 1] * o_c[r] + g[:, 3 * h + 1:3 * h + 2] * o_s[r]
               + g[:, 3 * h + 2:3 * h + 3] * o_w[r])
        o_ref[:, c] = (mix * _silu(z_ref[:, c])).astype(o_ref.dtype)


def _nsa(ga, gb, gc, gd, kc, vc, ov, e_sel, batch, seq):
    tq = 128
    nq = seq // tq
    row = lambda b, i: (b * nq + i, 0)
    kv = lambda col: pl.BlockSpec((seq, DH), lambda b, i: (b, col))
    cmp_spec = pl.BlockSpec((None, LANES, DH), lambda b, i: (b, 0, 0))
    full = lambda a: pl.BlockSpec(a.shape, lambda b, i: (0,) * a.ndim)
    return pl.pallas_call(
        functools.partial(_nsa_kernel, seq=seq, tq=tq),
        out_shape=jax.ShapeDtypeStruct((batch * seq, GROUP), BF16),
        grid=(batch, nq),
        in_specs=[pl.BlockSpec((tq, GROUP), row), cmp_spec, cmp_spec,
                  kv(5), kv(1), kv(6), kv(2),
                  pl.BlockSpec((tq, LANES), row), pl.BlockSpec((tq, GROUP), row),
                  full(ov), full(e_sel)],
        out_specs=pl.BlockSpec((tq, GROUP), row),
        compiler_params=_cparams(("parallel", "arbitrary")),
        name="nsa_attention",
    )(ga, kc, vc, ga, gb, ga, gb, gd, gc, ov, e_sel)


def _mlstm_kernel(q_ref, k_ref, v_ref, gcol_ref, grow_ref, bcol_ref, brow_ref, og_ref, z_ref, ng_ref,
                  tril_ref, triu_ref, out_ref, c_sc, n_sc, m_sc, *, tc):
    L = MLSTM_CHUNK

    @pl.when(pl.program_id(1) == 0)
    def _():
        c_sc[...] = jnp.zeros(c_sc.shape, F32)
        n_sc[...] = jnp.zeros(n_sc.shape, F32)
        m_sc[...] = jnp.full(m_sc.shape, NEG, F32)

    gcol = gcol_ref[...] + bcol_ref[...]
    grow = grow_ref[...] + brow_ref[...]
    hi, lo = _split_hi_lo(_log_sigmoid(gcol))
    bcol_all = _dot(tril_ref[...], hi) + _dot(tril_ref[...], lo)
    hi, lo = _split_hi_lo(_log_sigmoid(grow))
    brow_all = _dot(hi, triu_ref[...]) + _dot(lo, triu_ref[...])

    causal = (lax.broadcasted_iota(jnp.int32, (L, L), 1) <= lax.broadcasted_iota(jnp.int32, (L, L), 0))

    for c in range(tc // L):
        rs = slice(c * L, (c + 1) * L)
        for h in range(NH):
            cs = slice(h * DH, (h + 1) * DH)
            qh = q_ref[rs, cs]
            kh = k_ref[rs, cs]
            vh = v_ref[rs, cs]
            ig_col = gcol[rs, GD_I + h:GD_I + h + 1]
            b_col = bcol_all[rs, GD_F + h:GD_F + h + 1]
            ig_row = grow[h:h + 1, rs]
            b_row = brow_all[NH + h:NH + h + 1, rs]
            c_prev = c_sc[h]
            n_prev = n_sc[h]
            m_prev = m_sc[h][:, 0:1]

            d_log = jnp.where(causal, b_col - b_row + ig_row, NEG)
            m_intra = jnp.max(d_log, axis=-1, keepdims=True)
            m_inter = b_col + m_prev
            m_t = jnp.maximum(m_inter, m_intra)
            w_inter = jnp.exp(m_inter - m_t)
            qk = _dot_nt(qh, kh) * SCALE * jnp.exp(d_log - m_t)
            num = _dot(qk.astype(BF16), vh) + w_inter * _dot(qh, c_prev.astype(BF16))
            den = (jnp.sum(qk, axis=-1, keepdims=True)
                   + w_inter * jnp.sum(qh.astype(F32) * n_prev, axis=-1, keepdims=True))
            hh = num / jnp.maximum(jnp.abs(den), jnp.exp(-m_t))
            hh = jax.nn.sigmoid(og_ref[rs, cs]) * hh
            mu = jnp.mean(hh, axis=-1, keepdims=True)
            var = jnp.mean(jnp.square(hh - mu), axis=-1, keepdims=True)
            hn = (hh - mu) * lax.rsqrt(var + LN_EPS) * ng_ref[:, cs]
            out_ref[rs, cs] = (hn * _silu(z_ref[rs, cs])).astype(out_ref.dtype)

            b_last = b_row[:, L - 1:L]
            m_loc = jnp.max(b_last - b_row + ig_row, axis=-1, keepdims=True)
            e_col = jnp.exp(b_last - b_col + ig_col - m_loc)
            ek = (e_col * SCALE) * kh.astype(F32)
            g_c = lax.dot_general(ek.astype(BF16), vh, TN, preferred_element_type=F32)
            g_n = jnp.sum(ek, axis=0, keepdims=True)
            m_new = jnp.maximum(b_last + m_prev, m_loc)
            s_old = jnp.exp(b_last + m_prev - m_new)
            s_new = jnp.exp(m_loc - m_new)
            c_sc[h] = s_old * c_prev + s_new * g_c
            n_sc[h] = s_old * n_prev + s_new * g_n
            m_sc[h] = jnp.broadcast_to(m_new, (1, LANES))


def _mlstm(gb, gc, gd, gd_t, bias_col, bias_row, norm_g, tril, triu, batch, seq):
    tc = 256
    nt = seq // tc
    row = lambda col: (lambda b, j: (b * nt + j, col))
    full = lambda a: pl.BlockSpec(a.shape, lambda b, j: (0,) * a.ndim)
    return pl.pallas_call(
        functools.partial(_mlstm_kernel, tc=tc),
        out_shape=jax.ShapeDtypeStruct((batch * seq, GROUP), BF16),
        grid=(batch, nt),
        in_specs=[pl.BlockSpec((tc, GROUP), row(1)), pl.BlockSpec((tc, GROUP), row(2)),
                  pl.BlockSpec((tc, GROUP), row(3)),
                  pl.BlockSpec((tc, LANES), row(0)),
                  pl.BlockSpec((None, 2 * NH, tc), lambda b, j: (b, 0, j)),
                  full(bias_col), full(bias_row),
                  pl.BlockSpec((tc, GROUP), row(1)), pl.BlockSpec((tc, GROUP), row(2)),
                  full(norm_g), full(tril), full(triu)],
        out_specs=pl.BlockSpec((tc, GROUP), row(0)),
        scratch_shapes=[pltpu.VMEM((NH, DH, DH), F32), pltpu.VMEM((NH, 1, DH), F32),
                        pltpu.VMEM((NH, 1, LANES), F32)],
        compiler_params=_cparams(("parallel", "arbitrary")),
        name="mlstm",
    )(gb, gb, gb, gd, gd_t, bias_col, bias_row, gc, gc, norm_g, tril, triu)


def _lru_kernel(x_ref, z_ref, cw_ref, cb_ref, gw_ref, gb_ref, lam_ref, out_ref, xbuf, h_sc, *, tl):
    pad = 8

    @pl.when(pl.program_id(1) == 0)
    def _():
        xbuf[0:pad, :] = jnp.zeros((pad, GROUP), F32)
        h_sc[...] = jnp.zeros(h_sc.shape, F32)

    x = x_ref[...]
    xbuf[pad:pad + tl, :] = x
    u = cw_ref[3:4, :] * x + cb_ref[...]
    for w in range(3):
        u = u + cw_ref[w:w + 1, :] * xbuf[pad - 3 + w:pad - 3 + w + tl, :]
    xbuf[0:pad, :] = x[tl - pad:tl, :]

    ub = u.astype(BF16)
    pre = []
    for gi in range(2):
        pre.append(jnp.concatenate(
            [_dot(ub[:, n * DH:(n + 1) * DH], gw_ref[gi, n]) for n in range(NH)], axis=1) + gb_ref[gi:gi + 1, :])
    r = jax.nn.sigmoid(pre[0])
    ig = jax.nn.sigmoid(pre[1])
    neg_lam = -lam_ref[...]
    softplus = jnp.maximum(neg_lam, 0.0) + jnp.log(1.0 + jnp.exp(-jnp.abs(neg_lam)))
    a = jnp.exp(-LRU_C * r * softplus)
    bx = jnp.sqrt(1.0 - a * a) * (ig * u)

    rowi = lax.broadcasted_iota(jnp.int32, (tl, GROUP), 0)
    d = 1
    while d < tl:
        keep = rowi >= d
        a_sh = jnp.where(keep, pltpu.roll(a, d, 0), 1.0)
        b_sh = jnp.where(keep, pltpu.roll(bx, d, 0), 0.0)
        bx = a * b_sh + bx
        a = a * a_sh
        d *= 2
    hseq = a * h_sc[...] + bx
    h_sc[...] = hseq[tl - 1:tl, :]
    out_ref[...] = (hseq * _silu(z_ref[...])).astype(out_ref.dtype)


def _lru(gc, conv_w, conv_b, gate_w, gate_b, lam, batch, seq):
    tl = 256
    nt = seq // tl
    row = lambda col: (lambda b, j: (b * nt + j, col))
    full = lambda a: pl.BlockSpec(a.shape, lambda b, j: (0,) * a.ndim)
    return pl.pallas_call(
        functools.partial(_lru_kernel, tl=tl),
        out_shape=jax.ShapeDtypeStruct((batch * seq, GROUP), BF16),
        grid=(batch, nt),
        in_specs=[pl.BlockSpec((tl, GROUP), row(3)), pl.BlockSpec((tl, GROUP), row(4)),
                  full(conv_w), full(conv_b), full(gate_w), full(gate_b), full(lam)],
        out_specs=pl.BlockSpec((tl, GROUP), row(0)),
        scratch_shapes=[pltpu.VMEM((tl + 8, GROUP), F32), pltpu.VMEM((1, GROUP), F32)],
        compiler_params=_cparams(("parallel", "arbitrary")),
        name="rglru",
    )(gc, gc, conv_w, conv_b, gate_w, gate_b, lam)


def _moba_kernel(q_ref, k_ref, v_ref, z_ref, e_ref, o_ref, kmean_sc, *, seq):
    i = pl.program_id(2)
    tq = MOBA_BLOCK
    nb = seq // MOBA_BLOCK

    @pl.when(i == 0)
    def _():
        kmean_sc[...] = jnp.zeros(kmean_sc.shape, F32)
        kf = k_ref[...].astype(F32).reshape(nb, MOBA_BLOCK, DH)
        kmean_sc[0:nb, :] = jnp.mean(kf, axis=1)

    q = q_ref[...]
    gs = _dot_nt(q, kmean_sc[...].astype(BF16))
    lane = lax.broadcasted_iota(jnp.int32, (tq, LANES), 1)
    past = lane < i
    score = jnp.where(past, gs, NEG)
    lane_f = lane.astype(F32)
    sel = jnp.zeros((tq, LANES), F32)
    for _ in range(min(MOBA_TOPK, nb)):
        best = jnp.max(score, axis=-1, keepdims=True)
        first = jnp.min(jnp.where(score == best, lane_f, float(LANES)), axis=-1, keepdims=True)
        pick = lane_f == first
        sel = jnp.where(jnp.logical_and(pick, past), 1.0, sel)
        score = jnp.where(pick, -jnp.inf, score)

    key_sel = _dot(sel.astype(BF16), e_ref[...])
    kpos = lax.broadcasted_iota(jnp.int32, (tq, seq), 1)
    t = i * tq + lax.broadcasted_iota(jnp.int32, (tq, seq), 0)
    own = jnp.logical_and(kpos >= i * tq, kpos <= t)
    mask = jnp.logical_or(key_sel > 0.5, own)
    e, inv = _masked_exp(_dot_nt(q, k_ref[...]) * SCALE, mask)
    o = _dot(e.astype(BF16), v_ref[...]) * inv
    o_ref[...] = (o * _silu(z_ref[...])).astype(o_ref.dtype)


def _moba(ga, gb, gc, e_blk, batch, seq):
    tq = MOBA_BLOCK
    nq = seq // tq
    return pl.pallas_call(
        functools.partial(_moba_kernel, seq=seq),
        out_shape=jax.ShapeDtypeStruct((batch * seq, GROUP), BF16),
        grid=(batch, NH, nq),
        in_specs=[pl.BlockSpec((tq, DH), lambda b, h, i: (b * nq + i, 8 + h)),
                  pl.BlockSpec((seq, DH), lambda b, h, i: (b, 12 + h)),
                  pl.BlockSpec((seq, DH), lambda b, h, i: (b, 16 + h)),
                  pl.BlockSpec((tq, DH), lambda b, h, i: (b * nq + i, 20 + h)),
                  pl.BlockSpec(e_blk.shape, lambda b, h, i: (0, 0))],
        out_specs=pl.BlockSpec((tq, DH), lambda b, h, i: (b * nq + i, h)),
        scratch_shapes=[pltpu.VMEM((LANES, DH), F32)],
        compiler_params=_cparams(("parallel", "parallel", "arbitrary")),
        name="moba",
    )(ga, ga, gb, gc, e_blk)


def _out_kernel(ya_ref, yb_ref, yc_ref, yd_ref, w_ref, x_ref, g_ref, b_ref, o_ref, ob_ref):
    acc = DEEPNORM_ALPHA * x_ref[...]
    for p, y_ref in enumerate((ya_ref, yb_ref, yc_ref, yd_ref)):
        acc = acc + _dot(y_ref[...], w_ref[p * GROUP:(p + 1) * GROUP, :])
    mu = jnp.mean(acc, axis=-1, keepdims=True)
    var = jnp.mean(jnp.square(acc - mu), axis=-1, keepdims=True)
    y = (acc - mu) * lax.rsqrt(var + LN_EPS) * g_ref[...] + b_ref[...]
    o_ref[...] = y
    ob_ref[...] = y.astype(BF16)


def _out_proj(ys, w_out, x, ln_g, ln_b):
    m, d = x.shape
    tm = 256
    yspec = pl.BlockSpec((tm, GROUP), lambda i: (i, 0))
    full = lambda a: pl.BlockSpec(a.shape, lambda i: (0,) * a.ndim)
    xspec = pl.BlockSpec((tm, d), lambda i: (i, 0))
    return pl.pallas_call(
        _out_kernel,
        out_shape=(jax.ShapeDtypeStruct((m, d), F32), jax.ShapeDtypeStruct((m, d), BF16)),
        grid=(m // tm,),
        in_specs=[yspec] * 4 + [full(w_out), xspec, full(ln_g), full(ln_b)],
        out_specs=(xspec, xspec),
        compiler_params=_cparams(("parallel",)),
        name="out_proj_ln",
    )(*ys, w_out, x, ln_g, ln_b)


def _regroup_w_in(w):
    col = lambda name, a, b: w[:, _OFF[name] + a:_OFF[name] + b]
    zeros = lambda n: jnp.zeros((w.shape[0], n), w.dtype)
    wa = jnp.concatenate([col('nsa_q', 0, 512), col('nsa_kv', 0, 384), zeros(128),
                          col('moba_qkv', 0, 1024)], axis=1)
    wb = jnp.concatenate([col('nsa_kv', 384, 768), zeros(128), col('mlstm_qkv', 0, 1536),
                          col('moba_qkv', 1024, 1536)], axis=1)
    wc = jnp.concatenate([col('nsa_z', 0, 512), col('mlstm_o', 0, 512), col('mlstm_z', 0, 512),
                          col('lru_x', 0, 512), col('lru_z', 0, 512), col('moba_z', 0, 512)], axis=1)
    wd = jnp.concatenate([col('nsa_gate', 0, 12), col('mlstm_if', 0, 8), zeros(LANES - 20)], axis=1)
    return [a.astype(BF16) for a in (wa, wb, wc, wd)]


def _rope_tables(seq):
    half = ROT_DIM // 2
    inv_freq = jnp.power(ROPE_THETA, -jnp.arange(half, dtype=F32) * (2.0 / ROT_DIM))
    ang = jnp.arange(seq, dtype=jnp.int32).astype(F32)[:, None] * inv_freq[None, :]
    cos, sin = jnp.cos(ang), jnp.sin(ang)
    ones = jnp.ones((seq, DH - ROT_DIM), F32)
    return (jnp.concatenate([cos, cos, ones], axis=1),
            jnp.concatenate([-sin, sin, 0.0 * ones], axis=1))


def _block_indicator(block, seq):
    return (jnp.arange(LANES)[:, None] == (jnp.arange(seq) // block)[None, :]).astype(BF16)


def _cmp_overlap(seq):
    n_cmp = seq // CMP_STRIDE - 1
    n_sel = seq // SEL_BLOCK
    cs = jnp.arange(LANES)[:, None] * CMP_STRIDE
    ss = jnp.arange(LANES)[None, :] * SEL_BLOCK
    ov = (cs < ss + SEL_BLOCK) & (cs + CMP_BLOCK > ss)
    ov = ov & (jnp.arange(LANES)[:, None] < n_cmp) & (jnp.arange(LANES)[None, :] < n_sel)
    return ov.astype(BF16)


def _chunk_tri(tc):
    r = jnp.arange(tc)
    same = (r[:, None] // MLSTM_CHUNK) == (r[None, :] // MLSTM_CHUNK)
    tril = (same & (r[None, :] <= r[:, None])).astype(BF16)
    return tril, tril.T


def _layer(x, xb, batch, seq, consts, w_in, cmp_w1, cmp_w2, cmp_pe, i_bias, f_bias, norm_g,
           conv_w, conv_b, gate_w, gate_b, lam, w_out, ln_g, ln_b):
    rope, ov, e_sel, e_blk, tril, triu = consts
    wa, wb, wc, wd = _regroup_w_in(w_in)
    tm = min(1024, seq)
    ga = _project(xb, wa, BF16, tm, 512, rope=rope, seq=seq)
    gb = _project(xb, wb, BF16, tm, 512)
    gc = _project(xb, wc, F32, tm, 512)
    gd = _project(xb, wd, F32, tm, LANES)

    nb = seq // CMP_STRIDE
    tk = ga[:, 4 * DH:5 * DH].reshape(batch, nb, CMP_STRIDE * DH)
    tv = gb[:, 0:DH].reshape(batch, nb, CMP_STRIDE * DH)
    pe = cmp_pe.reshape(2, 2, CMP_STRIDE * DH)
    kc, vc = _compress(tk, tv, cmp_w1.astype(BF16), cmp_w2.astype(BF16), pe)
    y_a = _nsa(ga, gb, gc, gd, kc, vc, ov, e_sel, batch, seq)

    gd_t = jnp.transpose(gd[:, GD_I:GD_I + 2 * NH].reshape(batch, seq, 2 * NH), (0, 2, 1))
    bias = jnp.concatenate([i_bias, f_bias])
    bias_col = jnp.zeros((1, LANES), F32).at[0, GD_I:GD_I + 2 * NH].set(bias)
    bias_row = jnp.broadcast_to(bias[:, None], (2 * NH, LANES))[:, 0:1]
    y_b = _mlstm(gb, gc, gd, gd_t, bias_col, bias_row, norm_g[None, :], tril, triu, batch, seq)

    y_c = _lru(gc, conv_w, conv_b[None, :], gate_w.astype(BF16), gate_b, lam[None, :], batch, seq)

    y_d = _moba(ga, gb, gc, e_blk, batch, seq)

    return _out_proj((y_a, y_b, y_c, y_d), w_out.astype(BF16), x, ln_g[None, :], ln_b[None, :])


def kernel(x, w_in, nsa_cmp_w1, nsa_cmp_w2, nsa_cmp_pe, mlstm_i_bias, mlstm_f_bias, mlstm_norm_g,
           lru_conv_w, lru_conv_b, lru_gate_w, lru_gate_b, lru_lambda, w_out, ln_g, ln_b):
    batch, seq, d = x.shape
    tril, triu = _chunk_tri(256)
    consts = (_rope_tables(seq), _cmp_overlap(seq), _block_indicator(SEL_BLOCK, seq),
              _block_indicator(MOBA_BLOCK, seq), tril, triu)
    xf = x.reshape(batch * seq, d)
    xb = xf.astype(BF16)
    for l in range(w_in.shape[0]):
        xf, xb = _layer(xf, xb, batch, seq, consts, w_in[l], nsa_cmp_w1[l], nsa_cmp_w2[l], nsa_cmp_pe[l],
                        mlstm_i_bias[l], mlstm_f_bias[l], mlstm_norm_g[l], lru_conv_w[l], lru_conv_b[l],
                        lru_gate_w[l], lru_gate_b[l], lru_lambda[l], w_out[l], ln_g[l], ln_b[l])
    return xf.reshape(batch, seq, d)
```

```python
import functools
import math

import jax
import jax.numpy as jnp
from jax import lax
from jax.experimental import pallas as pl
from jax.experimental.pallas import tpu as pltpu

F32 = jnp.float32
BF16 = jnp.bfloat16

D_MODEL = 2048
DEPTH = 2
GROUP = 512
DH = 128
NH = 4
ROT_DIM = 32
ROPE_THETA = 500000.0

CMP_BLOCK = 32
CMP_STRIDE = 16
SEL_BLOCK = 64
SEL_TOPK = 8
WIN = 256
FORCE_SCORE = 1e9

MLSTM_CHUNK = 256
LRU_C = 8.0
MOBA_BLOCK = 256
MOBA_TOPK = 3

DEEPNORM_ALPHA = (2 * DEPTH) ** 0.25
NEG = -1e30
LN_EPS = 1e-5
SCALE = DH ** -0.5
EXP2_SCALE = SCALE * math.log2(math.e)

LANES = 128
VMEM_LIMIT = 56 * 1024 * 1024

_OFF = {}
_o = 0
for _name, _w in (('nsa_q', 512), ('nsa_kv', 768), ('nsa_gate', 12), ('nsa_z', 512), ('mlstm_qkv', 1536),
                  ('mlstm_if', 8), ('mlstm_o', 512), ('mlstm_z', 512), ('lru_x', 512), ('lru_z', 512),
                  ('moba_qkv', 1536), ('moba_z', 512)):
    _OFF[_name] = _o
    _o += _w

GD_I = 16
GD_F = 20

NT = (((1,), (1,)), ((), ()))
TN = (((0,), (0,)), ((), ()))


def _cparams(sem):
    return pltpu.CompilerParams(dimension_semantics=sem, vmem_limit_bytes=VMEM_LIMIT)


def _dot(a, b):
    return jnp.dot(a, b, preferred_element_type=F32)


def _dot_nt(a, b):
    return lax.dot_general(a, b, NT, preferred_element_type=F32)


def _split_hi_lo(a):
    hi = a.astype(BF16)
    lo = (a - hi.astype(F32)).astype(BF16)
    return hi, lo


def _silu(x):
    return x * jax.nn.sigmoid(x)


def _log_sigmoid(x):
    return jnp.minimum(x, 0.0) - jnp.log(1.0 + jnp.exp(-jnp.abs(x)))


def _proj_kernel(x_ref, w_ref, *rest, rotary):
    if rotary:
        cos_ref, sin_ref, o_ref = rest
    else:
        (o_ref,) = rest
    acc = _dot(x_ref[...], w_ref[...])
    if not rotary:
        o_ref[...] = acc.astype(o_ref.dtype)
        return
    c = cos_ref[...]
    s = sin_ref[...]
    lane = lax.broadcasted_iota(jnp.int32, c.shape, 1)
    half = ROT_DIM // 2
    for h in range(acc.shape[1] // DH):
        t = acc[:, h * DH:(h + 1) * DH]
        swapped = jnp.where(lane < half, pltpu.roll(t, DH - half, 1), pltpu.roll(t, half, 1))
        o_ref[:, h * DH:(h + 1) * DH] = (t * c + swapped * s).astype(o_ref.dtype)


def _project(xb, w, out_dtype, tm, tn, rope=None, seq=None):
    m, k = xb.shape
    n = w.shape[1]
    tn = min(tn, n)
    in_specs = [pl.BlockSpec((tm, k), lambda i, j: (i, 0)),
                pl.BlockSpec((k, tn), lambda i, j: (0, j))]
    args = [xb, w]
    if rope is not None:
        nrep = seq // tm
        in_specs += [pl.BlockSpec((tm, DH), lambda i, j: (i % nrep, 0))] * 2
        args += list(rope)
    return pl.pallas_call(
        functools.partial(_proj_kernel, rotary=rope is not None),
        out_shape=jax.ShapeDtypeStruct((m, n), out_dtype),
        grid=(m // tm, n // tn),
        in_specs=in_specs,
        out_specs=pl.BlockSpec((tm, tn), lambda i, j: (i, j)),
        compiler_params=_cparams(("parallel", "arbitrary")),
        name="in_proj_rot" if rope is not None else "in_proj",
    )(*args)


def _proj_t_kernel(wt_ref, x_ref, o_ref):
    o_ref[...] = _dot_nt(wt_ref[...], x_ref[...]).astype(o_ref.dtype)


def _project_t(xb, wt, out_dtype, tm):
    m, k = xb.shape
    n = wt.shape[0]
    return pl.pallas_call(
        _proj_t_kernel,
        out_shape=jax.ShapeDtypeStruct((n, m), out_dtype),
        grid=(m // tm,),
        in_specs=[pl.BlockSpec((n, k), lambda i: (0, 0)), pl.BlockSpec((tm, k), lambda i: (i, 0))],
        out_specs=pl.BlockSpec((n, tm), lambda i: (0, i)),
        compiler_params=_cparams(("parallel",)),
        name="in_proj_t",
    )(wt, xb)


def _compress_kernel(tk_ref, tv_ref, w1_ref, w2_ref, pe_ref, kc_ref, vc_ref, *, nb):
    half = (CMP_BLOCK // 2) * DH
    for idx, (t_ref, o_ref) in enumerate(((tk_ref, kc_ref), (tv_ref, vc_ref))):
        t = t_ref[...].astype(F32)
        lo = (t + pe_ref[idx, 0:1, :]).astype(BF16)
        hi = (t + pe_ref[idx, 1:2, :]).astype(BF16)
        a = _dot(lo, w1_ref[idx, 0:half, :])
        b = _dot(hi, w1_ref[idx, half:2 * half, :])
        hid = _silu(a + pltpu.roll(b, nb - 1, 0))
        out = _dot(hid.astype(BF16), w2_ref[idx])
        o_ref[...] = jnp.zeros(o_ref.shape, o_ref.dtype)
        o_ref[0:nb, :] = out.astype(o_ref.dtype)


def _compress(tk, tv, w1, w2, pe):
    b, nb, width = tk.shape
    blk = pl.BlockSpec((None, nb, width), lambda i: (i, 0, 0))
    full = lambda a: pl.BlockSpec(a.shape, lambda i: (0,) * a.ndim)
    out = jax.ShapeDtypeStruct((b, LANES, DH), BF16)
    return pl.pallas_call(
        functools.partial(_compress_kernel, nb=nb),
        out_shape=(out, out),
        grid=(b,),
        in_specs=[blk, blk, full(w1), full(w2), full(pe)],
        out_specs=(pl.BlockSpec((None, LANES, DH), lambda i: (i, 0, 0)),) * 2,
        compiler_params=_cparams(("parallel",)),
        name="nsa_compress",
    )(tk, tv, w1, w2, pe)


def _topk_rows(score, k_top):
    rowi = lax.broadcasted_iota(jnp.int32, score.shape, 0)
    rank = jnp.zeros(score.shape, F32)
    for k in range(score.shape[0]):
        sk = score[k:k + 1, :]
        beats = jnp.logical_or(sk > score, jnp.logical_and(sk == score, rowi > k))
        rank = rank + jnp.where(beats, 1.0, 0.0)
    return rank < k_top


def _online_init(width):
    return (jnp.full((1, width), NEG, F32), jnp.zeros((1, width), F32), jnp.zeros((DH, width), F32))


def _online_update(state, s, pv):
    m, l, acc = state
    m_new = jnp.maximum(m, jnp.max(s, axis=0, keepdims=True))
    alpha = jnp.exp2((m - m_new) * EXP2_SCALE)
    p = jnp.exp2((s - m_new) * EXP2_SCALE)
    l = alpha * l + jnp.sum(p, axis=0, keepdims=True)
    acc = alpha * acc + pv(p.astype(BF16))
    return m_new, l, acc


def _nsa_kernel(q_ref, kc_ref, vc_ref, ks_ref, vs_ref, kw_ref, vw_ref, gt_ref, z_ref, ovt_ref,
                o_ref, vst_sc, vwt_sc, bias_sc, *, seq, tq):
    i = pl.program_id(1)
    width = NH * tq
    ck = vst_sc.shape[2]
    per_chunk = ck // SEL_BLOCK
    n_cmp = seq // CMP_STRIDE - 1
    n_sel = seq // SEL_BLOCK
    n_rows = bias_sc.shape[0]

    @pl.when(i == 0)
    def _():
        for c in range(seq // ck):
            vst_sc[c] = jnp.transpose(vs_ref[c * ck:(c + 1) * ck, :].astype(F32)).astype(BF16)
        for c in range(seq // tq):
            vwt_sc[c] = jnp.transpose(vw_ref[c * tq:(c + 1) * tq, :].astype(F32)).astype(BF16)

    q = q_ref[...]
    qs = jnp.concatenate([q[:, h * DH:(h + 1) * DH] for h in range(NH)], axis=0)

    def tok(rows):
        return i * tq + (lax.broadcasted_iota(jnp.int32, (rows, width), 1) & (tq - 1))

    def row(rows, w=width):
        return lax.broadcasted_iota(jnp.int32, (rows, w), 0)

    rown = row(LANES)
    mask_c = jnp.logical_and(rown * CMP_STRIDE + (CMP_BLOCK - 1) <= tok(LANES), rown < n_cmp)
    s = jnp.where(mask_c, _dot_nt(kc_ref[...], qs), NEG)
    e = jnp.exp2((s - jnp.max(s, axis=0, keepdims=True)) * EXP2_SCALE)
    p_c = jnp.where(mask_c, e, 0.0) * (1.0 / jnp.sum(e, axis=0, keepdims=True))
    o_c = lax.dot_general(vc_ref[...], p_c.astype(BF16), TN, preferred_element_type=F32)

    p_sum = p_c[:, 0:tq]
    for h in range(1, NH):
        p_sum = p_sum + p_c[:, h * tq:(h + 1) * tq]
    p_hi, p_lo = _split_hi_lo(p_sum)
    imp = (_dot(ovt_ref[...], p_hi) + _dot(ovt_ref[...], p_lo))[0:n_rows]
    rowj = row(n_rows, tq)
    t_q = i * tq + lax.broadcasted_iota(jnp.int32, (n_rows, tq), 1)
    cur = t_q >> 6
    forced = jnp.logical_or(rowj == 0, jnp.logical_or(rowj == cur, rowj == cur - 1))
    valid = rowj * SEL_BLOCK <= t_q
    score = jnp.where(forced, FORCE_SCORE, jnp.where(valid, imp, NEG))
    score = jnp.where(rowj < n_sel, score, -jnp.inf)
    bias = jnp.where(_topk_rows(score, min(SEL_TOPK, n_sel)), 0.0, NEG)
    bias_sc[...] = jnp.concatenate([bias] * NH, axis=1)

    def chunk_scores(c):
        ks = pl.multiple_of(c * ck, ck)
        s = _dot_nt(ks_ref[pl.ds(ks, ck), :], qs)
        return jnp.concatenate(
            [s[b * SEL_BLOCK:(b + 1) * SEL_BLOCK] + bias_sc[pl.ds(c * per_chunk + b, 1), :]
             for b in range(per_chunk)], axis=0)

    diag = (i * tq) // ck
    s = jnp.where(diag * ck + row(ck) <= tok(ck), chunk_scores(diag), NEG)
    state = _online_update(_online_init(width), s, lambda p: _dot(vst_sc[diag], p))

    def body(c, st):
        return _online_update(st, chunk_scores(c), lambda p: _dot(vst_sc[c], p))

    _, l_s, acc_s = lax.fori_loop(0, diag, body, state)
    o_s = acc_s * (1.0 / l_s)

    parts, blocks = [], []
    for b in range(3):
        blk = i - 2 + b
        blk_c = jnp.maximum(blk, 0)
        s = _dot_nt(kw_ref[pl.ds(pl.multiple_of(blk_c * tq, tq), tq), :], qs)
        kpos = blk * tq + row(tq)
        diff = tok(tq) - kpos
        ok = jnp.logical_and(jnp.logical_and(diff >= 0, diff < WIN), kpos >= 0)
        parts.append(jnp.where(ok, s, NEG))
        blocks.append(blk_c)
    s = jnp.concatenate(parts, axis=0)
    e = jnp.exp2((s - jnp.max(s, axis=0, keepdims=True)) * EXP2_SCALE)
    inv_w = 1.0 / jnp.sum(e, axis=0, keepdims=True)
    e = e.astype(BF16)
    o_w = _dot(vwt_sc[blocks[0]], e[0:tq])
    for b in range(1, 3):
        o_w = o_w + _dot(vwt_sc[blocks[b]], e[b * tq:(b + 1) * tq])
    o_w = o_w * inv_w

    g = jax.nn.sigmoid(gt_ref[0:16, :])
    for h in range(NH):
        ls = slice(h * tq, (h + 1) * tq)
        cs = slice(h * DH, (h + 1) * DH)
        mix = (g[3 * h:3 * h + 1] * o_c[:, ls] + g[3 * h + 1:3 * h + 2] * o_s[:, ls]
               + g[3 * h + 2:3 * h + 3] * o_w[:, ls])
        o_ref[:, cs] = (jnp.transpose(mix) * _silu(z_ref[:, cs])).astype(o_ref.dtype)


def _nsa(ga, gb, gc, gd_t, kc, vc, ov_t, batch, seq):
    tq = 128
    ck = 256
    nq = seq // tq
    n_rows = 32
    assert seq // SEL_BLOCK <= n_rows and seq % ck == 0
    row = lambda b, i: (b * nq + i, 0)
    kv = lambda col: pl.BlockSpec((seq, DH), lambda b, i: (b, col))
    cmp_spec = pl.BlockSpec((None, LANES, DH), lambda b, i: (b, 0, 0))
    full = lambda a: pl.BlockSpec(a.shape, lambda b, i: (0,) * a.ndim)
    return pl.pallas_call(
        functools.partial(_nsa_kernel, seq=seq, tq=tq),
        out_shape=jax.ShapeDtypeStruct((batch * seq, GROUP), BF16),
        grid=(batch, nq),
        in_specs=[pl.BlockSpec((tq, GROUP), row), cmp_spec, cmp_spec,
                  kv(5), kv(1), kv(6), kv(2),
                  pl.BlockSpec((LANES, tq), lambda b, i: (0, b * nq + i)),
                  pl.BlockSpec((tq, GROUP), row), full(ov_t)],
        out_specs=pl.BlockSpec((tq, GROUP), row),
        scratch_shapes=[pltpu.VMEM((seq // ck, DH, ck), BF16), pltpu.VMEM((seq // tq, DH, tq), BF16),
                        pltpu.VMEM((n_rows, NH * tq), F32)],
        compiler_params=_cparams(("parallel", "arbitrary")),
        name="nsa_attention",
    )(ga, kc, vc, ga, gb, ga, gb, gd_t, gc, ov_t)


def _mlstm_kernel(q_ref, k_ref, v_ref, gcol_ref, grow_ref, bcol_ref, brow_ref, og_ref, z_ref, ng_ref,
                  tril_ref, triu_ref, out_ref, c_sc, n_sc, m_sc, *, tc):
    L = MLSTM_CHUNK

    @pl.when(pl.program_id(1) == 0)
    def _():
        c_sc[...] = jnp.zeros(c_sc.shape, F32)
        n_sc[...] = jnp.zeros(n_sc.shape, F32)
        m_sc[...] = jnp.full(m_sc.shape, NEG, F32)

    gcol = gcol_ref[...] + bcol_ref[...]
    grow = grow_ref[...] + brow_ref[...]
    hi, lo = _split_hi_lo(_log_sigmoid(gcol))
    bcol_all = _dot(tril_ref[...], hi) + _dot(tril_ref[...], lo)
    hi, lo = _split_hi_lo(_log_sigmoid(grow))
    brow_all = _dot(hi, triu_ref[...]) + _dot(lo, triu_ref[...])

    causal = (lax.broadcasted_iota(jnp.int32, (L, L), 1) <= lax.broadcasted_iota(jnp.int32, (L, L), 0))

    for c in range(tc // L):
        rs = slice(c * L, (c + 1) * L)
        for h in range(NH):
            cs = slice(h * DH, (h + 1) * DH)
            qh = q_ref[rs, cs]
            kh = k_ref[rs, cs]
            vh = v_ref[rs, cs]
            ig_col = gcol[rs, GD_I + h:GD_I + h + 1]
            b_col = bcol_all[rs, GD_F + h:GD_F + h + 1]
            ig_row = grow[h:h + 1, rs]
            b_row = brow_all[NH + h:NH + h + 1, rs]
            c_prev = c_sc[h]
            n_prev = n_sc[h]
            m_prev = m_sc[h][:, 0:1]

            d_log = jnp.where(causal, b_col - b_row + ig_row, NEG)
            m_intra = jnp.max(d_log, axis=-1, keepdims=True)
            m_inter = b_col + m_prev
            m_t = jnp.maximum(m_inter, m_intra)
            w_inter = jnp.exp(m_inter - m_t)
            qk = _dot_nt(qh, kh) * SCALE * jnp.exp(d_log - m_t)
            num = _dot(qk.astype(BF16), vh) + w_inter * _dot(qh, c_prev.astype(BF16))
            den = (jnp.sum(qk, axis=-1, keepdims=True)
                   + w_inter * jnp.sum(qh.astype(F32) * n_prev, axis=-1, keepdims=True))
            hh = num / jnp.maximum(jnp.abs(den), jnp.exp(-m_t))
            hh = jax.nn.sigmoid(og_ref[rs, cs]) * hh
            mu = jnp.mean(hh, axis=-1, keepdims=True)
            var = jnp.mean(jnp.square(hh - mu), axis=-1, keepdims=True)
            hn = (hh - mu) * lax.rsqrt(var + LN_EPS) * ng_ref[:, cs]
            out_ref[rs, cs] = (hn * _silu(z_ref[rs, cs])).astype(out_ref.dtype)

            b_last = b_row[:, L - 1:L]
            m_loc = jnp.max(b_last - b_row + ig_row, axis=-1, keepdims=True)
            e_col = jnp.exp(b_last - b_col + ig_col - m_loc)
            ek = (e_col * SCALE) * kh.astype(F32)
            g_c = lax.dot_general(ek.astype(BF16), vh, TN, preferred_element_type=F32)
            g_n = jnp.sum(ek, axis=0, keepdims=True)
            m_new = jnp.maximum(b_last + m_prev, m_loc)
            s_old = jnp.exp(b_last + m_prev - m_new)
            s_new = jnp.exp(m_loc - m_new)
            c_sc[h] = s_old * c_prev + s_new * g_c
            n_sc[h] = s_old * n_prev + s_new * g_n
            m_sc[h] = jnp.broadcast_to(m_new, (1, LANES))


def _mlstm(gb, gc, gd, gd_t, bias_col, bias_row, norm_g, tril, triu, batch, seq):
    tc = 256
    nt = seq // tc
    row = lambda col: (lambda b, j: (b * nt + j, col))
    full = lambda a: pl.BlockSpec(a.shape, lambda b, j: (0,) * a.ndim)
    return pl.pallas_call(
        functools.partial(_mlstm_kernel, tc=tc),
        out_shape=jax.ShapeDtypeStruct((batch * seq, GROUP), BF16),
        grid=(batch, nt),
        in_specs=[pl.BlockSpec((tc, GROUP), row(1)), pl.BlockSpec((tc, GROUP), row(2)),
                  pl.BlockSpec((tc, GROUP), row(3)),
                  pl.BlockSpec((tc, LANES), row(0)),
                  pl.BlockSpec((2 * NH, tc), lambda b, j: (GD_I // (2 * NH), b * nt + j)),
                  full(bias_col), full(bias_row),
                  pl.BlockSpec((tc, GROUP), row(1)), pl.BlockSpec((tc, GROUP), row(2)),
                  full(norm_g), full(tril), full(triu)],
        out_specs=pl.BlockSpec((tc, GROUP), row(0)),
        scratch_shapes=[pltpu.VMEM((NH, DH, DH), F32), pltpu.VMEM((NH, 1, DH), F32),
                        pltpu.VMEM((NH, 1, LANES), F32)],
        compiler_params=_cparams(("parallel", "arbitrary")),
        name="mlstm",
    )(gb, gb, gb, gd, gd_t, bias_col, bias_row, gc, gc, norm_g, tril, triu)


def _lru_kernel(x_ref, z_ref, cw_ref, cb_ref, gw_ref, gb_ref, lam_ref, out_ref, xbuf, h_sc, *, tl):
    pad = 8

    @pl.when(pl.program_id(1) == 0)
    def _():
        xbuf[0:pad, :] = jnp.zeros((pad, GROUP), F32)
        h_sc[...] = jnp.zeros(h_sc.shape, F32)

    x = x_ref[...]
    xbuf[pad:pad + tl, :] = x
    u = cw_ref[3:4, :] * x + cb_ref[...]
    for w in range(3):
        u = u + cw_ref[w:w + 1, :] * xbuf[pad - 3 + w:pad - 3 + w + tl, :]
    xbuf[0:pad, :] = x[tl - pad:tl, :]

    ub = u.astype(BF16)
    pre = []
    for gi in range(2):
        pre.append(jnp.concatenate(
            [_dot(ub[:, n * DH:(n + 1) * DH], gw_ref[gi, n]) for n in range(NH)], axis=1) + gb_ref[gi:gi + 1, :])
    r = jax.nn.sigmoid(pre[0])
    ig = jax.nn.sigmoid(pre[1])
    neg_lam = -lam_ref[...]
    softplus = jnp.maximum(neg_lam, 0.0) + jnp.log(1.0 + jnp.exp(-jnp.abs(neg_lam)))
    a = jnp.exp(-LRU_C * r * softplus)
    bx = jnp.sqrt(1.0 - a * a) * (ig * u)

    rowi = lax.broadcasted_iota(jnp.int32, (tl, GROUP), 0)
    d = 1
    while d < tl:
        keep = rowi >= d
        a_sh = jnp.where(keep, pltpu.roll(a, d, 0), 1.0)
        b_sh = jnp.where(keep, pltpu.roll(bx, d, 0), 0.0)
        bx = a * b_sh + bx
        a = a * a_sh
        d *= 2
    hseq = a * h_sc[...] + bx
    h_sc[...] = hseq[tl - 1:tl, :]
    out_ref[...] = (hseq * _silu(z_ref[...])).astype(out_ref.dtype)


def _lru(gc, conv_w, conv_b, gate_w, gate_b, lam, batch, seq):
    tl = 256
    nt = seq // tl
    row = lambda col: (lambda b, j: (b * nt + j, col))
    full = lambda a: pl.BlockSpec(a.shape, lambda b, j: (0,) * a.ndim)
    return pl.pallas_call(
        functools.partial(_lru_kernel, tl=tl),
        out_shape=jax.ShapeDtypeStruct((batch * seq, GROUP), BF16),
        grid=(batch, nt),
        in_specs=[pl.BlockSpec((tl, GROUP), row(3)), pl.BlockSpec((tl, GROUP), row(4)),
                  full(conv_w), full(conv_b), full(gate_w), full(gate_b), full(lam)],
        out_specs=pl.BlockSpec((tl, GROUP), row(0)),
        scratch_shapes=[pltpu.VMEM((tl + 8, GROUP), F32), pltpu.VMEM((1, GROUP), F32)],
        compiler_params=_cparams(("parallel", "arbitrary")),
        name="rglru",
    )(gc, gc, conv_w, conv_b, gate_w, gate_b, lam)


def _moba_kernel(q_ref, k_ref, v_ref, z_ref, o_ref, kmean_sc, bias_sc, vt_sc, *, seq):
    i = pl.program_id(1)
    tq = MOBA_BLOCK
    nb = seq // MOBA_BLOCK
    ncand = bias_sc.shape[1]

    @pl.when(i == 0)
    def _():
        kmean_sc[...] = jnp.zeros(kmean_sc.shape, F32)
        for h in range(NH):
            kf = k_ref[:, h * DH:(h + 1) * DH].astype(F32).reshape(nb, MOBA_BLOCK, DH)
            kmean_sc[h, 0:nb, :] = jnp.mean(kf, axis=1)
            for c in range(nb):
                vt_sc[h, c] = jnp.transpose(
                    v_ref[c * tq:(c + 1) * tq, h * DH:(h + 1) * DH].astype(F32)).astype(BF16)

    past = lax.broadcasted_iota(jnp.int32, (ncand, tq), 0) < i
    causal = (lax.broadcasted_iota(jnp.int32, (tq, tq), 0) <= lax.broadcasted_iota(jnp.int32, (tq, tq), 1))
    own = pl.multiple_of(i * tq, tq)
    heads = [slice(h * DH, (h + 1) * DH) for h in range(NH)]
    qs = [q_ref[:, cs] for cs in heads]
    states = []
    for h, cs in enumerate(heads):
        gate = _dot_nt(kmean_sc[h].astype(BF16), qs[h])[0:ncand]
        score = jnp.where(past, gate, NEG)
        sel = jnp.logical_and(_topk_rows(score, min(MOBA_TOPK, nb)), past)
        bias_sc[h] = jnp.where(sel, 0.0, NEG)
        s = jnp.where(causal, _dot_nt(k_ref[pl.ds(own, tq), cs], qs[h]), NEG)
        states.append(_online_update(_online_init(tq), s, lambda p, h=h: _dot(vt_sc[h, i], p)))

    def body(j, states):
        ks = pl.multiple_of(j * tq, tq)
        out = []
        for h, cs in enumerate(heads):
            s = _dot_nt(k_ref[pl.ds(ks, tq), cs], qs[h]) + bias_sc[h, pl.ds(j, 1), :]
            out.append(_online_update(states[h], s, lambda p, h=h: _dot(vt_sc[h, j], p)))
        return tuple(out)

    states = lax.fori_loop(0, i, body, tuple(states))
    for h, cs in enumerate(heads):
        _, l, acc = states[h]
        o = jnp.transpose(acc * (1.0 / l))
        o_ref[:, cs] = (o * _silu(z_ref[:, cs])).astype(o_ref.dtype)


def _moba(ga, gb, gc, batch, seq):
    tq = MOBA_BLOCK
    nq = seq // tq
    ncand = 8
    assert nq <= ncand
    row = lambda col: (lambda b, i: (b * nq + i, col))
    return pl.pallas_call(
        functools.partial(_moba_kernel, seq=seq),
        out_shape=jax.ShapeDtypeStruct((batch * seq, GROUP), BF16),
        grid=(batch, nq),
        in_specs=[pl.BlockSpec((tq, GROUP), row(2)),
                  pl.BlockSpec((seq, GROUP), lambda b, i: (b, 3)),
                  pl.BlockSpec((seq, GROUP), lambda b, i: (b, 4)),
                  pl.BlockSpec((tq, GROUP), row(5))],
        out_specs=pl.BlockSpec((tq, GROUP), row(0)),
        scratch_shapes=[pltpu.VMEM((NH, LANES, DH), F32), pltpu.VMEM((NH, ncand, tq), F32),
                        pltpu.VMEM((NH, nq, DH, tq), BF16)],
        compiler_params=_cparams(("parallel", "arbitrary")),
        name="moba",
    )(ga, ga, gb, gc)


def _out_kernel(ya_ref, yb_ref, yc_ref, yd_ref, w_ref, x_ref, g_ref, b_ref, o_ref, ob_ref):
    acc = DEEPNORM_ALPHA * x_ref[...]
    for p, y_ref in enumerate((ya_ref, yb_ref, yc_ref, yd_ref)):
        acc = acc + _dot(y_ref[...], w_ref[p * GROUP:(p + 1) * GROUP, :])
    mu = jnp.mean(acc, axis=-1, keepdims=True)
    var = jnp.mean(jnp.square(acc - mu), axis=-1, keepdims=True)
    y = (acc - mu) * lax.rsqrt(var + LN_EPS) * g_ref[...] + b_ref[...]
    o_ref[...] = y
    ob_ref[...] = y.astype(BF16)


def _out_proj(ys, w_out, x, ln_g, ln_b):
    m, d = x.shape
    tm = 256
    yspec = pl.BlockSpec((tm, GROUP), lambda i: (i, 0))
    full = lambda a: pl.BlockSpec(a.shape, lambda i: (0,) * a.ndim)
    xspec = pl.BlockSpec((tm, d), lambda i: (i, 0))
    return pl.pallas_call(
        _out_kernel,
        out_shape=(jax.ShapeDtypeStruct((m, d), F32), jax.ShapeDtypeStruct((m, d), BF16)),
        grid=(m // tm,),
        in_specs=[yspec] * 4 + [full(w_out), xspec, full(ln_g), full(ln_b)],
        out_specs=(xspec, xspec),
        compiler_params=_cparams(("parallel",)),
        name="out_proj_ln",
    )(*ys, w_out, x, ln_g, ln_b)


_GROUPS = (
    (('nsa_q', 0, 512), ('nsa_kv', 0, 384), (None, 0, 128), ('moba_qkv', 0, 1024)),
    (('nsa_kv', 384, 384), (None, 0, 128), ('mlstm_qkv', 0, 1536), ('moba_qkv', 1024, 512)),
    (('nsa_z', 0, 512), ('mlstm_o', 0, 512), ('mlstm_z', 0, 512), ('lru_x', 0, 512), ('lru_z', 0, 512),
     ('moba_z', 0, 512)),
    (('nsa_gate', 0, 12), (None, 0, 4), ('mlstm_if', 0, 8), (None, 0, LANES - 24)),
)


def _regroup_kernel(w_ref, *out_refs):
    for pieces, o_ref in zip(_GROUPS, out_refs):
        dst = 0
        for name, off, width in pieces:
            if name is None:
                o_ref[:, dst:dst + width] = jnp.zeros((o_ref.shape[0], width), o_ref.dtype)
            else:
                src = _OFF[name] + off
                o_ref[:, dst:dst + width] = w_ref[:, src:src + width].astype(o_ref.dtype)
            dst += width


def _regroup_w_in(w_in):
    depth, d, n_in = w_in.shape
    tk = 256
    widths = [sum(p[2] for p in pieces) for pieces in _GROUPS]
    return pl.pallas_call(
        _regroup_kernel,
        out_shape=tuple(jax.ShapeDtypeStruct((depth, d, n), BF16) for n in widths),
        grid=(depth, d // tk),
        in_specs=[pl.BlockSpec((None, tk, n_in), lambda l, i: (l, i, 0))],
        out_specs=tuple(pl.BlockSpec((None, tk, n), lambda l, i: (l, i, 0)) for n in widths),
        compiler_params=_cparams(("parallel", "parallel")),
        name="regroup_w_in",
    )(w_in)


def _rope_tables(seq):
    half = ROT_DIM // 2
    inv_freq = jnp.power(ROPE_THETA, -jnp.arange(half, dtype=F32) * (2.0 / ROT_DIM))
    ang = jnp.arange(seq, dtype=jnp.int32).astype(F32)[:, None] * inv_freq[None, :]
    cos, sin = jnp.cos(ang), jnp.sin(ang)
    ones = jnp.ones((seq, DH - ROT_DIM), F32)
    return (jnp.concatenate([cos, cos, ones], axis=1),
            jnp.concatenate([-sin, sin, 0.0 * ones], axis=1))


def _cmp_overlap_t(seq):
    n_cmp = seq // CMP_STRIDE - 1
    n_sel = seq // SEL_BLOCK
    cs = jnp.arange(LANES)[None, :] * CMP_STRIDE
    ss = jnp.arange(LANES)[:, None] * SEL_BLOCK
    ov = (cs < ss + SEL_BLOCK) & (cs + CMP_BLOCK > ss)
    ov = ov & (jnp.arange(LANES)[None, :] < n_cmp) & (jnp.arange(LANES)[:, None] < n_sel)
    return ov.astype(BF16)


def _chunk_tri(tc):
    r = jnp.arange(tc)
    same = (r[:, None] // MLSTM_CHUNK) == (r[None, :] // MLSTM_CHUNK)
    tril = (same & (r[None, :] <= r[:, None])).astype(BF16)
    return tril, tril.T


def _layer(x, xb, batch, seq, consts, w_groups, cmp_w1, cmp_w2, cmp_pe, i_bias, f_bias, norm_g,
           conv_w, conv_b, gate_w, gate_b, lam, w_out, ln_g, ln_b):
    rope, ov_t, tril, triu = consts
    wa, wb, wc, wd = w_groups
    tm = min(1024, seq)
    ga = _project(xb, wa, BF16, tm, 512, rope=rope, seq=seq)
    gb = _project(xb, wb, BF16, tm, 512)
    gc = _project(xb, wc, F32, tm, 512)
    gd = _project(xb, wd, F32, tm, LANES)
    gd_t = _project_t(xb, jnp.transpose(wd), F32, tm)

    nb = seq // CMP_STRIDE
    tk = ga[:, 4 * DH:5 * DH].reshape(batch, nb, CMP_STRIDE * DH)
    tv = gb[:, 0:DH].reshape(batch, nb, CMP_STRIDE * DH)
    pe = cmp_pe.reshape(2, 2, CMP_STRIDE * DH)
    kc, vc = _compress(tk, tv, cmp_w1.astype(BF16), cmp_w2.astype(BF16), pe)
    y_a = _nsa(ga, gb, gc, gd_t, kc, vc, ov_t, batch, seq)

    bias = jnp.concatenate([i_bias, f_bias])
    bias_col = jnp.zeros((1, LANES), F32).at[0, GD_I:GD_I + 2 * NH].set(bias)
    bias_row = bias[:, None]
    y_b = _mlstm(gb, gc, gd, gd_t, bias_col, bias_row, norm_g[None, :], tril, triu, batch, seq)

    y_c = _lru(gc, conv_w, conv_b[None, :], gate_w.astype(BF16), gate_b, lam[None, :], batch, seq)

    y_d = _moba(ga, gb, gc, batch, seq)

    return _out_proj((y_a, y_b, y_c, y_d), w_out.astype(BF16), x, ln_g[None, :], ln_b[None, :])


def kernel(x, w_in, nsa_cmp_w1, nsa_cmp_w2, nsa_cmp_pe, mlstm_i_bias, mlstm_f_bias, mlstm_norm_g,
           lru_conv_w, lru_conv_b, lru_gate_w, lru_gate_b, lru_lambda, w_out, ln_g, ln_b):
    batch, seq, d = x.shape
    tril, triu = _chunk_tri(256)
    consts = (_rope_tables(seq), _cmp_overlap_t(seq), tril, triu)
    xf = x.reshape(batch * seq, d)
    xb = xf.astype(BF16)
    w_groups = _regroup_w_in(w_in)
    for l in range(w_in.shape[0]):
        xf, xb = _layer(xf, xb, batch, seq, consts, [w[l] for w in w_groups], nsa_cmp_w1[l], nsa_cmp_w2[l], nsa_cmp_pe[l],
                        mlstm_i_bias[l], mlstm_f_bias[l], mlstm_norm_g[l], lru_conv_w[l], lru_conv_b[l],
                        lru_gate_w[l], lru_gate_b[l], lru_lambda[l], w_out[l], ln_g[l], ln_b[l])
    return xf.reshape(batch, seq, d)
```

```python
import functools
import math

import jax
import jax.numpy as jnp
from jax import lax
from jax.experimental import pallas as pl
from jax.experimental.pallas import tpu as pltpu

F32 = jnp.float32
BF16 = jnp.bfloat16

D_MODEL = 2048
DEPTH = 2
GROUP = 512
DH = 128
NH = 4
ROT_DIM = 32
ROPE_THETA = 500000.0

CMP_BLOCK = 32
CMP_STRIDE = 16
SEL_BLOCK = 64
SEL_TOPK = 8
WIN = 256
FORCE_SCORE = 1e9

MLSTM_CHUNK = 256
LRU_C = 8.0
MOBA_BLOCK = 256
MOBA_TOPK = 3

DEEPNORM_ALPHA = (2 * DEPTH) ** 0.25
NEG = -1e30
LN_EPS = 1e-5
SCALE = DH ** -0.5
EXP2_SCALE = SCALE * math.log2(math.e)

LANES = 128
VMEM_LIMIT = 56 * 1024 * 1024

_OFF = {}
_o = 0
for _name, _w in (('nsa_q', 512), ('nsa_kv', 768), ('nsa_gate', 12), ('nsa_z', 512), ('mlstm_qkv', 1536),
                  ('mlstm_if', 8), ('mlstm_o', 512), ('mlstm_z', 512), ('lru_x', 512), ('lru_z', 512),
                  ('moba_qkv', 1536), ('moba_z', 512)):
    _OFF[_name] = _o
    _o += _w

GD_I = 16
GD_F = 20

NT = (((1,), (1,)), ((), ()))
TN = (((0,), (0,)), ((), ()))


def _cparams(sem):
    return pltpu.CompilerParams(dimension_semantics=sem, vmem_limit_bytes=VMEM_LIMIT)


def _dot(a, b):
    return jnp.dot(a, b, preferred_element_type=F32)


def _dot_nt(a, b):
    return lax.dot_general(a, b, NT, preferred_element_type=F32)


def _split_hi_lo(a):
    hi = a.astype(BF16)
    lo = (a - hi.astype(F32)).astype(BF16)
    return hi, lo


def _silu(x):
    return x * jax.nn.sigmoid(x)


def _log_sigmoid(x):
    return jnp.minimum(x, 0.0) - jnp.log(1.0 + jnp.exp(-jnp.abs(x)))


def _proj_kernel(x_ref, w_ref, *rest, rotary):
    if rotary:
        cos_ref, sin_ref, o_ref = rest
    else:
        (o_ref,) = rest
    acc = _dot_nt(x_ref[...], w_ref[...])
    if not rotary:
        o_ref[...] = acc.astype(o_ref.dtype)
        return
    c = cos_ref[...]
    s = sin_ref[...]
    lane = lax.broadcasted_iota(jnp.int32, c.shape, 1)
    half = ROT_DIM // 2
    for h in range(acc.shape[1] // DH):
        t = acc[:, h * DH:(h + 1) * DH]
        swapped = jnp.where(lane < half, pltpu.roll(t, DH - half, 1), pltpu.roll(t, half, 1))
        o_ref[:, h * DH:(h + 1) * DH] = (t * c + swapped * s).astype(o_ref.dtype)


def _project(xb, w, out_dtype, tm, tn, rope=None, seq=None):
    m, k = xb.shape
    n = w.shape[0]
    tn = min(tn, n)
    in_specs = [pl.BlockSpec((tm, k), lambda i, j: (i, 0)),
                pl.BlockSpec((tn, k), lambda i, j: (j, 0))]
    args = [xb, w]
    if rope is not None:
        nrep = seq // tm
        in_specs += [pl.BlockSpec((tm, DH), lambda i, j: (i % nrep, 0))] * 2
        args += list(rope)
    return pl.pallas_call(
        functools.partial(_proj_kernel, rotary=rope is not None),
        out_shape=jax.ShapeDtypeStruct((m, n), out_dtype),
        grid=(m // tm, n // tn),
        in_specs=in_specs,
        out_specs=pl.BlockSpec((tm, tn), lambda i, j: (i, j)),
        compiler_params=_cparams(("parallel", "arbitrary")),
        name="in_proj_rot" if rope is not None else "in_proj",
    )(*args)


def _proj_t_kernel(wt_ref, x_ref, o_ref):
    o_ref[...] = _dot_nt(wt_ref[...], x_ref[...]).astype(o_ref.dtype)


def _project_t(xb, wt, out_dtype, tm):
    m, k = xb.shape
    n = wt.shape[0]
    return pl.pallas_call(
        _proj_t_kernel,
        out_shape=jax.ShapeDtypeStruct((n, m), out_dtype),
        grid=(m // tm,),
        in_specs=[pl.BlockSpec((n, k), lambda i: (0, 0)), pl.BlockSpec((tm, k), lambda i: (i, 0))],
        out_specs=pl.BlockSpec((n, tm), lambda i: (0, i)),
        compiler_params=_cparams(("parallel",)),
        name="in_proj_t",
    )(wt, xb)


def _compress_kernel(tk_ref, tv_ref, w1_ref, w2_ref, pe_ref, kc_ref, vc_ref, *, nb):
    half = (CMP_BLOCK // 2) * DH
    for idx, (t_ref, o_ref) in enumerate(((tk_ref, kc_ref), (tv_ref, vc_ref))):
        t = t_ref[...].astype(F32)
        lo = (t + pe_ref[idx, 0:1, :]).astype(BF16)
        hi = (t + pe_ref[idx, 1:2, :]).astype(BF16)
        a = _dot(lo, w1_ref[idx, 0:half, :])
        b = _dot(hi, w1_ref[idx, half:2 * half, :])
        hid = _silu(a + pltpu.roll(b, nb - 1, 0))
        out = _dot(hid.astype(BF16), w2_ref[idx])
        o_ref[...] = jnp.zeros(o_ref.shape, o_ref.dtype)
        o_ref[0:nb, :] = out.astype(o_ref.dtype)


def _compress(tk, tv, w1, w2, pe):
    b, nb, width = tk.shape
    blk = pl.BlockSpec((None, nb, width), lambda i: (i, 0, 0))
    full = lambda a: pl.BlockSpec(a.shape, lambda i: (0,) * a.ndim)
    out = jax.ShapeDtypeStruct((b, LANES, DH), BF16)
    return pl.pallas_call(
        functools.partial(_compress_kernel, nb=nb),
        out_shape=(out, out),
        grid=(b,),
        in_specs=[blk, blk, full(w1), full(w2), full(pe)],
        out_specs=(pl.BlockSpec((None, LANES, DH), lambda i: (i, 0, 0)),) * 2,
        compiler_params=_cparams(("parallel",)),
        name="nsa_compress",
    )(tk, tv, w1, w2, pe)


def _topk_rows(score, k_top):
    rowi = lax.broadcasted_iota(jnp.int32, score.shape, 0)
    rank = jnp.zeros(score.shape, F32)
    for k in range(score.shape[0]):
        sk = score[k:k + 1, :]
        beats = jnp.logical_or(sk > score, jnp.logical_and(sk == score, rowi > k))
        rank = rank + jnp.where(beats, 1.0, 0.0)
    return rank < k_top


def _online_init(width):
    return (jnp.full((1, width), NEG, F32), jnp.zeros((1, width), F32), jnp.zeros((DH, width), F32))


def _online_update(state, s, pv):
    m, l, acc = state
    m_new = jnp.maximum(m, jnp.max(s, axis=0, keepdims=True))
    alpha = jnp.exp2((m - m_new) * EXP2_SCALE)
    p = jnp.exp2((s - m_new) * EXP2_SCALE)
    l = alpha * l + jnp.sum(p, axis=0, keepdims=True)
    acc = alpha * acc + pv(p.astype(BF16))
    return m_new, l, acc


def _nsa_kernel(q_ref, kc_ref, vc_ref, ks_ref, vs_ref, kw_ref, vw_ref, gt_ref, z_ref, ovt_ref, diagb_ref,
                winb_ref, o_ref, vst_sc, vwt_sc, bias_sc, *, seq, tq):
    i = pl.program_id(1)
    width = NH * tq
    ck = vst_sc.shape[2]
    per_chunk = ck // SEL_BLOCK
    n_cmp = seq // CMP_STRIDE - 1
    n_sel = seq // SEL_BLOCK
    n_rows = bias_sc.shape[0]

    @pl.when(i == 0)
    def _():
        for c in range(seq // ck):
            vst_sc[c] = jnp.transpose(vs_ref[c * ck:(c + 1) * ck, :].astype(F32)).astype(BF16)
        for c in range(seq // tq):
            vwt_sc[c] = jnp.transpose(vw_ref[c * tq:(c + 1) * tq, :].astype(F32)).astype(BF16)

    q = q_ref[...]
    qs = jnp.concatenate([q[:, h * DH:(h + 1) * DH] for h in range(NH)], axis=0)

    def tok(rows):
        return i * tq + (lax.broadcasted_iota(jnp.int32, (rows, width), 1) & (tq - 1))

    def row(rows, w=width):
        return lax.broadcasted_iota(jnp.int32, (rows, w), 0)

    rown = row(LANES)
    mask_c = jnp.logical_and(rown * CMP_STRIDE + (CMP_BLOCK - 1) <= tok(LANES), rown < n_cmp)
    s = jnp.where(mask_c, _dot_nt(kc_ref[...], qs), NEG)
    e = jnp.exp2((s - jnp.max(s, axis=0, keepdims=True)) * EXP2_SCALE)
    p_c = jnp.where(mask_c, e, 0.0) * (1.0 / jnp.sum(e, axis=0, keepdims=True))
    o_c = lax.dot_general(vc_ref[...], p_c.astype(BF16), TN, preferred_element_type=F32)

    p_sum = p_c[:, 0:tq]
    for h in range(1, NH):
        p_sum = p_sum + p_c[:, h * tq:(h + 1) * tq]
    p_hi, p_lo = _split_hi_lo(p_sum)
    imp = (_dot(ovt_ref[...], p_hi) + _dot(ovt_ref[...], p_lo))[0:n_rows]
    rowj = row(n_rows, tq)
    t_q = i * tq + lax.broadcasted_iota(jnp.int32, (n_rows, tq), 1)
    cur = t_q >> 6
    forced = jnp.logical_or(rowj == 0, jnp.logical_or(rowj == cur, rowj == cur - 1))
    valid = rowj * SEL_BLOCK <= t_q
    score = jnp.where(forced, FORCE_SCORE, jnp.where(valid, imp, NEG))
    score = jnp.where(rowj < n_sel, score, -jnp.inf)
    bias = jnp.where(_topk_rows(score, min(SEL_TOPK, n_sel)), 0.0, NEG)
    bias_sc[...] = jnp.concatenate([bias] * NH, axis=1)

    def chunk_scores(c):
        ks = pl.multiple_of(c * ck, ck)
        s = _dot_nt(ks_ref[pl.ds(ks, ck), :], qs)
        return jnp.concatenate(
            [s[b * SEL_BLOCK:(b + 1) * SEL_BLOCK] + bias_sc[pl.ds(c * per_chunk + b, 1), :]
             for b in range(per_chunk)], axis=0)

    diag = (i * tq) // ck
    s = chunk_scores(diag) + diagb_ref[i % (ck // tq)]
    state = _online_update(_online_init(width), s, lambda p: _dot(vst_sc[diag], p))

    def body(c, st):
        return _online_update(st, chunk_scores(c), lambda p: _dot(vst_sc[c], p))

    _, l_s, acc_s = lax.fori_loop(0, diag, body, state)
    o_s = acc_s * (1.0 / l_s)

    parts, blocks = [], []
    for b in range(3):
        blk = i - 2 + b
        blk_c = jnp.maximum(blk, 0)
        s = _dot_nt(kw_ref[pl.ds(pl.multiple_of(blk_c * tq, tq), tq), :], qs)
        parts.append(s + winb_ref[b] + jnp.where(blk >= 0, 0.0, NEG))
        blocks.append(blk_c)
    s = jnp.concatenate(parts, axis=0)
    e = jnp.exp2((s - jnp.max(s, axis=0, keepdims=True)) * EXP2_SCALE)
    inv_w = 1.0 / jnp.sum(e, axis=0, keepdims=True)
    e = e.astype(BF16)
    o_w = _dot(vwt_sc[blocks[0]], e[0:tq])
    for b in range(1, 3):
        o_w = o_w + _dot(vwt_sc[blocks[b]], e[b * tq:(b + 1) * tq])
    o_w = o_w * inv_w

    g = jax.nn.sigmoid(gt_ref[0:16, :])
    for h in range(NH):
        ls = slice(h * tq, (h + 1) * tq)
        cs = slice(h * DH, (h + 1) * DH)
        mix = (g[3 * h:3 * h + 1] * o_c[:, ls] + g[3 * h + 1:3 * h + 2] * o_s[:, ls]
               + g[3 * h + 2:3 * h + 3] * o_w[:, ls])
        o_ref[:, cs] = (jnp.transpose(mix) * _silu(z_ref[:, cs].astype(F32))).astype(o_ref.dtype)


def _nsa_bias_tables(tq, ck):
    t = jnp.arange(NH * tq)[None, :] % tq
    r = jnp.arange(ck)[:, None]
    diag = jnp.stack([jnp.where(r <= ph * tq + t, 0.0, NEG) for ph in range(ck // tq)])
    r = jnp.arange(tq)[:, None]
    win = []
    for b in range(3):
        diff = t - (r + (b - 2) * tq)
        win.append(jnp.where((diff >= 0) & (diff < WIN), 0.0, NEG))
    return diag.astype(F32), jnp.stack(win).astype(F32)


def _nsa(ga, gb, gc, gd_t, kc, vc, ov_t, batch, seq):
    tq = 128
    ck = 256
    nq = seq // tq
    n_rows = 32
    assert seq // SEL_BLOCK <= n_rows and seq % ck == 0
    row = lambda b, i: (b * nq + i, 0)
    kv = lambda col: pl.BlockSpec((seq, DH), lambda b, i: (b, col))
    cmp_spec = pl.BlockSpec((None, LANES, DH), lambda b, i: (b, 0, 0))
    full = lambda a: pl.BlockSpec(a.shape, lambda b, i: (0,) * a.ndim)
    diag_b, win_b = _nsa_bias_tables(tq, ck)
    return pl.pallas_call(
        functools.partial(_nsa_kernel, seq=seq, tq=tq),
        out_shape=jax.ShapeDtypeStruct((batch * seq, GROUP), BF16),
        grid=(batch, nq),
        in_specs=[pl.BlockSpec((tq, GROUP), row), cmp_spec, cmp_spec,
                  kv(5), kv(1), kv(6), kv(2),
                  pl.BlockSpec((LANES, tq), lambda b, i: (0, b * nq + i)),
                  pl.BlockSpec((tq, GROUP), row), full(ov_t), full(diag_b), full(win_b)],
        out_specs=pl.BlockSpec((tq, GROUP), row),
        scratch_shapes=[pltpu.VMEM((seq // ck, DH, ck), BF16), pltpu.VMEM((seq // tq, DH, tq), BF16),
                        pltpu.VMEM((n_rows, NH * tq), F32)],
        compiler_params=_cparams(("parallel", "arbitrary")),
        name="nsa_attention",
    )(ga, kc, vc, ga, gb, ga, gb, gd_t, gc, ov_t, diag_b, win_b)


def _mlstm_kernel(q_ref, k_ref, v_ref, gcol_ref, grow_ref, bcol_ref, brow_ref, og_ref, z_ref, ng_ref,
                  tril_ref, triu_ref, out_ref, c_sc, n_sc, m_sc, *, tc):
    L = MLSTM_CHUNK

    @pl.when(pl.program_id(1) == 0)
    def _():
        c_sc[...] = jnp.zeros(c_sc.shape, F32)
        n_sc[...] = jnp.zeros(n_sc.shape, F32)
        m_sc[...] = jnp.full(m_sc.shape, NEG, F32)

    gcol = gcol_ref[...] + bcol_ref[...]
    grow = grow_ref[...] + brow_ref[...]
    hi, lo = _split_hi_lo(_log_sigmoid(gcol))
    bcol_all = _dot(tril_ref[...], hi) + _dot(tril_ref[...], lo)
    hi, lo = _split_hi_lo(_log_sigmoid(grow))
    brow_all = _dot(hi, triu_ref[...]) + _dot(lo, triu_ref[...])

    causal = (lax.broadcasted_iota(jnp.int32, (L, L), 1) <= lax.broadcasted_iota(jnp.int32, (L, L), 0))

    for c in range(tc // L):
        rs = slice(c * L, (c + 1) * L)
        for h in range(NH):
            cs = slice(h * DH, (h + 1) * DH)
            qh = q_ref[rs, cs]
            kh = k_ref[rs, cs]
            vh = v_ref[rs, cs]
            ig_col = gcol[rs, GD_I + h:GD_I + h + 1]
            b_col = bcol_all[rs, GD_F + h:GD_F + h + 1]
            ig_row = grow[h:h + 1, rs]
            b_row = brow_all[NH + h:NH + h + 1, rs]
            c_prev = c_sc[h]
            n_prev = n_sc[h]
            m_prev = m_sc[h][:, 0:1]

            d_log = jnp.where(causal, b_col - b_row + ig_row, NEG)
            m_intra = jnp.max(d_log, axis=-1, keepdims=True)
            m_inter = b_col + m_prev
            m_t = jnp.maximum(m_inter, m_intra)
            w_inter = jnp.exp(m_inter - m_t)
            qk = _dot_nt(qh, kh) * SCALE * jnp.exp(d_log - m_t)
            num = _dot(qk.astype(BF16), vh) + w_inter * _dot(qh, c_prev.astype(BF16))
            den = (jnp.sum(qk, axis=-1, keepdims=True)
                   + w_inter * jnp.sum(qh.astype(F32) * n_prev, axis=-1, keepdims=True))
            hh = num / jnp.maximum(jnp.abs(den), jnp.exp(-m_t))
            hh = jax.nn.sigmoid(og_ref[rs, cs].astype(F32)) * hh
            mu = jnp.mean(hh, axis=-1, keepdims=True)
            var = jnp.mean(jnp.square(hh - mu), axis=-1, keepdims=True)
            hn = (hh - mu) * lax.rsqrt(var + LN_EPS) * ng_ref[:, cs]
            out_ref[rs, cs] = (hn * _silu(z_ref[rs, cs].astype(F32))).astype(out_ref.dtype)

            b_last = b_row[:, L - 1:L]
            m_loc = jnp.max(b_last - b_row + ig_row, axis=-1, keepdims=True)
            e_col = jnp.exp(b_last - b_col + ig_col - m_loc)
            ek = (e_col * SCALE) * kh.astype(F32)
            g_c = lax.dot_general(ek.astype(BF16), vh, TN, preferred_element_type=F32)
            g_n = jnp.sum(ek, axis=0, keepdims=True)
            m_new = jnp.maximum(b_last + m_prev, m_loc)
            s_old = jnp.exp(b_last + m_prev - m_new)
            s_new = jnp.exp(m_loc - m_new)
            c_sc[h] = s_old * c_prev + s_new * g_c
            n_sc[h] = s_old * n_prev + s_new * g_n
            m_sc[h] = jnp.broadcast_to(m_new, (1, LANES))


def _mlstm(gb, gc, gd, gd_t, bias_col, bias_row, norm_g, tril, triu, batch, seq):
    tc = 256
    nt = seq // tc
    row = lambda col: (lambda b, j: (b * nt + j, col))
    full = lambda a: pl.BlockSpec(a.shape, lambda b, j: (0,) * a.ndim)
    return pl.pallas_call(
        functools.partial(_mlstm_kernel, tc=tc),
        out_shape=jax.ShapeDtypeStruct((batch * seq, GROUP), BF16),
        grid=(batch, nt),
        in_specs=[pl.BlockSpec((tc, GROUP), row(1)), pl.BlockSpec((tc, GROUP), row(2)),
                  pl.BlockSpec((tc, GROUP), row(3)),
                  pl.BlockSpec((tc, LANES), row(0)),
                  pl.BlockSpec((2 * NH, tc), lambda b, j: (GD_I // (2 * NH), b * nt + j)),
                  full(bias_col), full(bias_row),
                  pl.BlockSpec((tc, GROUP), row(1)), pl.BlockSpec((tc, GROUP), row(2)),
                  full(norm_g), full(tril), full(triu)],
        out_specs=pl.BlockSpec((tc, GROUP), row(0)),
        scratch_shapes=[pltpu.VMEM((NH, DH, DH), F32), pltpu.VMEM((NH, 1, DH), F32),
                        pltpu.VMEM((NH, 1, LANES), F32)],
        compiler_params=_cparams(("parallel", "arbitrary")),
        name="mlstm",
    )(gb, gb, gb, gd, gd_t, bias_col, bias_row, gc, gc, norm_g, tril, triu)


def _lru_kernel(x_ref, z_ref, cw_ref, cb_ref, gw_ref, gb_ref, lam_ref, out_ref, xbuf, h_sc, *, tl):
    pad = 8

    @pl.when(pl.program_id(1) == 0)
    def _():
        xbuf[0:pad, :] = jnp.zeros((pad, GROUP), F32)
        h_sc[...] = jnp.zeros(h_sc.shape, F32)

    x = x_ref[...].astype(F32)
    xbuf[pad:pad + tl, :] = x
    u = cw_ref[3:4, :] * x + cb_ref[...]
    for w in range(3):
        u = u + cw_ref[w:w + 1, :] * xbuf[pad - 3 + w:pad - 3 + w + tl, :]
    xbuf[0:pad, :] = x[tl - pad:tl, :]

    ub = u.astype(BF16)
    pre = []
    for gi in range(2):
        pre.append(jnp.concatenate(
            [_dot(ub[:, n * DH:(n + 1) * DH], gw_ref[gi, n]) for n in range(NH)], axis=1) + gb_ref[gi:gi + 1, :])
    r = jax.nn.sigmoid(pre[0])
    ig = jax.nn.sigmoid(pre[1])
    neg_lam = -lam_ref[...]
    softplus = jnp.maximum(neg_lam, 0.0) + jnp.log(1.0 + jnp.exp(-jnp.abs(neg_lam)))
    a = jnp.exp(-LRU_C * r * softplus)
    bx = jnp.sqrt(1.0 - a * a) * (ig * u)

    rowi = lax.broadcasted_iota(jnp.int32, (tl, GROUP), 0)
    d = 1
    while d < tl:
        keep = rowi >= d
        a_sh = jnp.where(keep, pltpu.roll(a, d, 0), 1.0)
        b_sh = jnp.where(keep, pltpu.roll(bx, d, 0), 0.0)
        bx = a * b_sh + bx
        a = a * a_sh
        d *= 2
    hseq = a * h_sc[...] + bx
    h_sc[...] = hseq[tl - 1:tl, :]
    out_ref[...] = (hseq * _silu(z_ref[...].astype(F32))).astype(out_ref.dtype)


def _lru(gc, conv_w, conv_b, gate_w, gate_b, lam, batch, seq):
    tl = 256
    nt = seq // tl
    row = lambda col: (lambda b, j: (b * nt + j, col))
    full = lambda a: pl.BlockSpec(a.shape, lambda b, j: (0,) * a.ndim)
    return pl.pallas_call(
        functools.partial(_lru_kernel, tl=tl),
        out_shape=jax.ShapeDtypeStruct((batch * seq, GROUP), BF16),
        grid=(batch, nt),
        in_specs=[pl.BlockSpec((tl, GROUP), row(3)), pl.BlockSpec((tl, GROUP), row(4)),
                  full(conv_w), full(conv_b), full(gate_w), full(gate_b), full(lam)],
        out_specs=pl.BlockSpec((tl, GROUP), row(0)),
        scratch_shapes=[pltpu.VMEM((tl + 8, GROUP), F32), pltpu.VMEM((1, GROUP), F32)],
        compiler_params=_cparams(("parallel", "arbitrary")),
        name="rglru",
    )(gc, gc, conv_w, conv_b, gate_w, gate_b, lam)


def _moba_kernel(q_ref, k_ref, v_ref, z_ref, o_ref, kmean_sc, bias_sc, vt_sc, *, seq):
    i = pl.program_id(1)
    tq = MOBA_BLOCK
    nb = seq // MOBA_BLOCK
    ncand = bias_sc.shape[1]
    pair = 2 * tq

    @pl.when(i == 0)
    def _():
        kmean_sc[...] = jnp.zeros(kmean_sc.shape, F32)
        for h in range(NH):
            kf = k_ref[:, h * DH:(h + 1) * DH].astype(F32).reshape(nb, MOBA_BLOCK, DH)
            kmean_sc[h, 0:nb, :] = jnp.mean(kf, axis=1)
            for c in range(nb // 2):
                vt_sc[h, c] = jnp.transpose(
                    v_ref[c * pair:(c + 1) * pair, h * DH:(h + 1) * DH].astype(F32)).astype(BF16)

    past = lax.broadcasted_iota(jnp.int32, (ncand, tq), 0) < i
    causal = (lax.broadcasted_iota(jnp.int32, (tq, tq), 0) <= lax.broadcasted_iota(jnp.int32, (tq, tq), 1))
    own = pl.multiple_of(i * tq, tq)
    heads = [slice(h * DH, (h + 1) * DH) for h in range(NH)]
    qs = [q_ref[:, cs] for cs in heads]
    states = []
    for h, cs in enumerate(heads):
        gate = _dot_nt(kmean_sc[h].astype(BF16), qs[h])[0:ncand]
        score = jnp.where(past, gate, NEG)
        sel = jnp.logical_and(_topk_rows(score, min(MOBA_TOPK, nb)), past)
        bias_sc[h] = jnp.where(sel, 0.0, NEG)
        s = jnp.where(causal, _dot_nt(k_ref[pl.ds(own, tq), cs], qs[h]), NEG)
        states.append(_online_update(
            _online_init(tq), s,
            lambda p, cs=cs: lax.dot_general(v_ref[pl.ds(own, tq), cs], p, TN, preferred_element_type=F32)))

    def body(j, states):
        ks = pl.multiple_of(j * pair, pair)
        out = []
        for h, cs in enumerate(heads):
            s = _dot_nt(k_ref[pl.ds(ks, pair), cs], qs[h])
            s = jnp.concatenate([s[0:tq] + bias_sc[h, pl.ds(2 * j, 1), :],
                                 s[tq:pair] + bias_sc[h, pl.ds(2 * j + 1, 1), :]], axis=0)
            out.append(_online_update(states[h], s, lambda p, h=h: _dot(vt_sc[h, j], p)))
        return tuple(out)

    states = lax.fori_loop(0, (i + 1) // 2, body, tuple(states))
    for h, cs in enumerate(heads):
        _, l, acc = states[h]
        o = jnp.transpose(acc * (1.0 / l))
        o_ref[:, cs] = (o * _silu(z_ref[:, cs].astype(F32))).astype(o_ref.dtype)


def _moba(ga, gb, gc, batch, seq):
    tq = MOBA_BLOCK
    nq = seq // tq
    ncand = 8
    assert nq <= ncand and nq % 2 == 0
    row = lambda col: (lambda b, i: (b * nq + i, col))
    return pl.pallas_call(
        functools.partial(_moba_kernel, seq=seq),
        out_shape=jax.ShapeDtypeStruct((batch * seq, GROUP), BF16),
        grid=(batch, nq),
        in_specs=[pl.BlockSpec((tq, GROUP), row(2)),
                  pl.BlockSpec((seq, GROUP), lambda b, i: (b, 3)),
                  pl.BlockSpec((seq, GROUP), lambda b, i: (b, 4)),
                  pl.BlockSpec((tq, GROUP), row(5))],
        out_specs=pl.BlockSpec((tq, GROUP), row(0)),
        scratch_shapes=[pltpu.VMEM((NH, LANES, DH), F32), pltpu.VMEM((NH, ncand, tq), F32),
                        pltpu.VMEM((NH, nq // 2, DH, 2 * tq), BF16)],
        compiler_params=_cparams(("parallel", "arbitrary")),
        name="moba",
    )(ga, ga, gb, gc)


def _out_kernel(ya_ref, yb_ref, yc_ref, yd_ref, w_ref, x_ref, g_ref, b_ref, o_ref, ob_ref):
    acc = DEEPNORM_ALPHA * x_ref[...]
    for p, y_ref in enumerate((ya_ref, yb_ref, yc_ref, yd_ref)):
        acc = acc + _dot(y_ref[...], w_ref[p * GROUP:(p + 1) * GROUP, :])
    mu = jnp.mean(acc, axis=-1, keepdims=True)
    var = jnp.mean(jnp.square(acc - mu), axis=-1, keepdims=True)
    y = (acc - mu) * lax.rsqrt(var + LN_EPS) * g_ref[...] + b_ref[...]
    o_ref[...] = y
    ob_ref[...] = y.astype(BF16)


def _out_proj(ys, w_out, x, ln_g, ln_b):
    m, d = x.shape
    tm = 256
    yspec = pl.BlockSpec((tm, GROUP), lambda i: (i, 0))
    full = lambda a: pl.BlockSpec(a.shape, lambda i: (0,) * a.ndim)
    xspec = pl.BlockSpec((tm, d), lambda i: (i, 0))
    return pl.pallas_call(
        _out_kernel,
        out_shape=(jax.ShapeDtypeStruct((m, d), F32), jax.ShapeDtypeStruct((m, d), BF16)),
        grid=(m // tm,),
        in_specs=[yspec] * 4 + [full(w_out), xspec, full(ln_g), full(ln_b)],
        out_specs=(xspec, xspec),
        compiler_params=_cparams(("parallel",)),
        name="out_proj_ln",
    )(*ys, w_out, x, ln_g, ln_b)


_GROUPS = (
    (('nsa_q', 0, 512), ('nsa_kv', 0, 384), (None, 0, 128), ('moba_qkv', 0, 1024)),
    (('nsa_kv', 384, 384), (None, 0, 128), ('mlstm_qkv', 0, 1536), ('moba_qkv', 1024, 512)),
    (('nsa_z', 0, 512), ('mlstm_o', 0, 512), ('mlstm_z', 0, 512), ('lru_x', 0, 512), ('lru_z', 0, 512),
     ('moba_z', 0, 512)),
)
_GATE_PIECES = (('nsa_gate', 0, 12), (None, 0, 4), ('mlstm_if', 0, 8), (None, 0, LANES - 24))


def _regroup_kernel(w_ref, *out_refs):
    for pieces, o_ref in zip(_GROUPS, out_refs):
        dst = 0
        for name, off, width in pieces:
            if name is None:
                o_ref[dst:dst + width, :] = jnp.zeros((width, o_ref.shape[1]), o_ref.dtype)
            else:
                src = _OFF[name] + off
                o_ref[dst:dst + width, :] = w_ref[src:src + width, :].astype(o_ref.dtype)
            dst += width


def _regroup_w_in(w_in):
    depth, d, n_in = w_in.shape
    wt = jnp.transpose(w_in, (0, 2, 1))
    tk = 256
    widths = [sum(p[2] for p in pieces) for pieces in _GROUPS]
    groups = pl.pallas_call(
        _regroup_kernel,
        out_shape=tuple(jax.ShapeDtypeStruct((depth, n, d), BF16) for n in widths),
        grid=(depth, d // tk),
        in_specs=[pl.BlockSpec((None, n_in, tk), lambda l, i: (l, 0, i))],
        out_specs=tuple(pl.BlockSpec((None, n, tk), lambda l, i: (l, 0, i)) for n in widths),
        compiler_params=_cparams(("parallel", "parallel")),
        name="regroup_w_in",
    )(wt)
    gate_rows = [jnp.zeros((depth, width, d), w_in.dtype) if name is None
                 else jnp.transpose(w_in[:, :, _OFF[name] + off:_OFF[name] + off + width], (0, 2, 1))
                 for name, off, width in _GATE_PIECES]
    return list(groups) + [jnp.concatenate(gate_rows, axis=1).astype(BF16)]


def _rope_tables(seq):
    half = ROT_DIM // 2
    inv_freq = jnp.power(ROPE_THETA, -jnp.arange(half, dtype=F32) * (2.0 / ROT_DIM))
    ang = jnp.arange(seq, dtype=jnp.int32).astype(F32)[:, None] * inv_freq[None, :]
    cos, sin = jnp.cos(ang), jnp.sin(ang)
    ones = jnp.ones((seq, DH - ROT_DIM), F32)
    return (jnp.concatenate([cos, cos, ones], axis=1),
            jnp.concatenate([-sin, sin, 0.0 * ones], axis=1))


def _cmp_overlap_t(seq):
    n_cmp = seq // CMP_STRIDE - 1
    n_sel = seq // SEL_BLOCK
    cs = jnp.arange(LANES)[None, :] * CMP_STRIDE
    ss = jnp.arange(LANES)[:, None] * SEL_BLOCK
    ov = (cs < ss + SEL_BLOCK) & (cs + CMP_BLOCK > ss)
    ov = ov & (jnp.arange(LANES)[None, :] < n_cmp) & (jnp.arange(LANES)[:, None] < n_sel)
    return ov.astype(BF16)


def _chunk_tri(tc):
    r = jnp.arange(tc)
    same = (r[:, None] // MLSTM_CHUNK) == (r[None, :] // MLSTM_CHUNK)
    tril = (same & (r[None, :] <= r[:, None])).astype(BF16)
    return tril, tril.T


def _layer(x, xb, batch, seq, consts, w_groups, cmp_w1, cmp_w2, cmp_pe, i_bias, f_bias, norm_g,
           conv_w, conv_b, gate_w, gate_b, lam, w_out, ln_g, ln_b):
    rope, ov_t, tril, triu = consts
    wa, wb, wc, wd = w_groups
    tm = min(2048, seq)
    ga = _project(xb, wa, BF16, tm, 512, rope=rope, seq=seq)
    gb = _project(xb, wb, BF16, tm, 512)
    gc = _project(xb, wc, BF16, tm, 512)
    gd = _project(xb, wd, F32, tm, LANES)
    gd_t = _project_t(xb, wd, F32, tm)

    nb = seq // CMP_STRIDE
    tk = ga[:, 4 * DH:5 * DH].reshape(batch, nb, CMP_STRIDE * DH)
    tv = gb[:, 0:DH].reshape(batch, nb, CMP_STRIDE * DH)
    pe = cmp_pe.reshape(2, 2, CMP_STRIDE * DH)
    kc, vc = _compress(tk, tv, cmp_w1.astype(BF16), cmp_w2.astype(BF16), pe)
    y_a = _nsa(ga, gb, gc, gd_t, kc, vc, ov_t, batch, seq)

    bias = jnp.concatenate([i_bias, f_bias])
    bias_col = jnp.zeros((1, LANES), F32).at[0, GD_I:GD_I + 2 * NH].set(bias)
    bias_row = bias[:, None]
    y_b = _mlstm(gb, gc, gd, gd_t, bias_col, bias_row, norm_g[None, :], tril, triu, batch, seq)

    y_c = _lru(gc, conv_w, conv_b[None, :], gate_w.astype(BF16), gate_b, lam[None, :], batch, seq)

    y_d = _moba(ga, gb, gc, batch, seq)

    return _out_proj((y_a, y_b, y_c, y_d), w_out.astype(BF16), x, ln_g[None, :], ln_b[None, :])


def kernel(x, w_in, nsa_cmp_w1, nsa_cmp_w2, nsa_cmp_pe, mlstm_i_bias, mlstm_f_bias, mlstm_norm_g,
           lru_conv_w, lru_conv_b, lru_gate_w, lru_gate_b, lru_lambda, w_out, ln_g, ln_b):
    batch, seq, d = x.shape
    tril, triu = _chunk_tri(256)
    consts = (_rope_tables(seq), _cmp_overlap_t(seq), tril, triu)
    xf = x.reshape(batch * seq, d)
    xb = xf.astype(BF16)
    w_groups = _regroup_w_in(w_in)
    for l in range(w_in.shape[0]):
        xf, xb = _layer(xf, xb, batch, seq, consts, [w[l] for w in w_groups], nsa_cmp_w1[l], nsa_cmp_w2[l], nsa_cmp_pe[l],
                        mlstm_i_bias[l], mlstm_f_bias[l], mlstm_norm_g[l], lru_conv_w[l], lru_conv_b[l],
                        lru_gate_w[l], lru_gate_b[l], lru_lambda[l], w_out[l], ln_g[l], ln_b[l])
    return xf.reshape(batch, seq, d)
```

```python
import functools
import math

import jax
import jax.numpy as jnp
from jax import lax
from jax.experimental import pallas as pl
from jax.experimental.pallas import tpu as pltpu

F32 = jnp.float32
BF16 = jnp.bfloat16

D_MODEL = 2048
DEPTH = 2
GROUP = 512
DH = 128
NH = 4
ROT_DIM = 32
ROPE_THETA = 500000.0

CMP_BLOCK = 32
CMP_STRIDE = 16
SEL_BLOCK = 64
SEL_TOPK = 8
WIN = 256
FORCE_SCORE = 1e9

MLSTM_CHUNK = 512
LRU_C = 8.0
MOBA_BLOCK = 256
MOBA_TOPK = 3

DEEPNORM_ALPHA = (2 * DEPTH) ** 0.25
NEG = -1e30
LN_EPS = 1e-5
SCALE = DH ** -0.5
EXP2_SCALE = SCALE * math.log2(math.e)

LANES = 128
VMEM_LIMIT = 56 * 1024 * 1024

_OFF = {}
_o = 0
for _name, _w in (('nsa_q', 512), ('nsa_kv', 768), ('nsa_gate', 12), ('nsa_z', 512), ('mlstm_qkv', 1536),
                  ('mlstm_if', 8), ('mlstm_o', 512), ('mlstm_z', 512), ('lru_x', 512), ('lru_z', 512),
                  ('moba_qkv', 1536), ('moba_z', 512)):
    _OFF[_name] = _o
    _o += _w

GD_I = 16
GD_F = 20

NT = (((1,), (1,)), ((), ()))
TN = (((0,), (0,)), ((), ()))


def _cparams(sem):
    return pltpu.CompilerParams(dimension_semantics=sem, vmem_limit_bytes=VMEM_LIMIT)


def _dot(a, b):
    return jnp.dot(a, b, preferred_element_type=F32)


def _dot_nt(a, b):
    return lax.dot_general(a, b, NT, preferred_element_type=F32)


def _split_hi_lo(a):
    hi = a.astype(BF16)
    lo = (a - hi.astype(F32)).astype(BF16)
    return hi, lo


def _silu(x):
    return x * jax.nn.sigmoid(x)


def _log_sigmoid(x):
    return jnp.minimum(x, 0.0) - jnp.log(1.0 + jnp.exp(-jnp.abs(x)))


def _proj_kernel(x_ref, w_ref, *rest, rotary):
    if rotary:
        cos_ref, sin_ref, o_ref = rest
    else:
        (o_ref,) = rest
    acc = _dot_nt(x_ref[...], w_ref[...])
    if not rotary:
        o_ref[...] = acc.astype(o_ref.dtype)
        return
    c = cos_ref[...]
    s = sin_ref[...]
    lane = lax.broadcasted_iota(jnp.int32, c.shape, 1)
    half = ROT_DIM // 2
    for h in range(acc.shape[1] // DH):
        t = acc[:, h * DH:(h + 1) * DH]
        swapped = jnp.where(lane < half, pltpu.roll(t, DH - half, 1), pltpu.roll(t, half, 1))
        o_ref[:, h * DH:(h + 1) * DH] = (t * c + swapped * s).astype(o_ref.dtype)


def _project(xb, w, out_dtype, tm, tn, rope=None, seq=None):
    m, k = xb.shape
    n = w.shape[0]
    tn = min(tn, n)
    in_specs = [pl.BlockSpec((tm, k), lambda i, j: (i, 0)),
                pl.BlockSpec((tn, k), lambda i, j: (j, 0))]
    args = [xb, w]
    if rope is not None:
        nrep = seq // tm
        in_specs += [pl.BlockSpec((tm, DH), lambda i, j: (i % nrep, 0))] * 2
        args += list(rope)
    return pl.pallas_call(
        functools.partial(_proj_kernel, rotary=rope is not None),
        out_shape=jax.ShapeDtypeStruct((m, n), out_dtype),
        grid=(m // tm, n // tn),
        in_specs=in_specs,
        out_specs=pl.BlockSpec((tm, tn), lambda i, j: (i, j)),
        compiler_params=_cparams(("parallel", "arbitrary")),
        name="in_proj_rot" if rope is not None else "in_proj",
    )(*args)


def _proj_t_kernel(wt_ref, x_ref, o_ref):
    o_ref[...] = _dot_nt(wt_ref[...], x_ref[...]).astype(o_ref.dtype)


def _project_t(xb, wt, out_dtype, tm):
    m, k = xb.shape
    n = wt.shape[0]
    return pl.pallas_call(
        _proj_t_kernel,
        out_shape=jax.ShapeDtypeStruct((n, m), out_dtype),
        grid=(m // tm,),
        in_specs=[pl.BlockSpec((n, k), lambda i: (0, 0)), pl.BlockSpec((tm, k), lambda i: (i, 0))],
        out_specs=pl.BlockSpec((n, tm), lambda i: (0, i)),
        compiler_params=_cparams(("parallel",)),
        name="in_proj_t",
    )(wt, xb)


def _compress_kernel(tk_ref, tv_ref, w1_ref, w2_ref, pe_ref, kc_ref, vc_ref, *, nb):
    half = (CMP_BLOCK // 2) * DH
    for idx, (t_ref, o_ref) in enumerate(((tk_ref, kc_ref), (tv_ref, vc_ref))):
        t = t_ref[...].astype(F32)
        lo = (t + pe_ref[idx, 0:1, :]).astype(BF16)
        hi = (t + pe_ref[idx, 1:2, :]).astype(BF16)
        a = _dot(lo, w1_ref[idx, 0:half, :])
        b = _dot(hi, w1_ref[idx, half:2 * half, :])
        hid = _silu(a + pltpu.roll(b, nb - 1, 0))
        out = _dot(hid.astype(BF16), w2_ref[idx])
        o_ref[...] = jnp.zeros(o_ref.shape, o_ref.dtype)
        o_ref[0:nb, :] = out.astype(o_ref.dtype)


def _compress(tk, tv, w1, w2, pe):
    b, nb, width = tk.shape
    blk = pl.BlockSpec((None, nb, width), lambda i: (i, 0, 0))
    full = lambda a: pl.BlockSpec(a.shape, lambda i: (0,) * a.ndim)
    out = jax.ShapeDtypeStruct((b, LANES, DH), BF16)
    return pl.pallas_call(
        functools.partial(_compress_kernel, nb=nb),
        out_shape=(out, out),
        grid=(b,),
        in_specs=[blk, blk, full(w1), full(w2), full(pe)],
        out_specs=(pl.BlockSpec((None, LANES, DH), lambda i: (i, 0, 0)),) * 2,
        compiler_params=_cparams(("parallel",)),
        name="nsa_compress",
    )(tk, tv, w1, w2, pe)


def _topk_rows(score, k_top):
    rowi = lax.broadcasted_iota(jnp.int32, score.shape, 0)
    rank = jnp.zeros(score.shape, F32)
    for k in range(score.shape[0]):
        sk = score[k:k + 1, :]
        beats = jnp.logical_or(sk > score, jnp.logical_and(sk == score, rowi > k))
        rank = rank + jnp.where(beats, 1.0, 0.0)
    return rank < k_top


def _online_init(width):
    return (jnp.full((1, width), NEG, F32), jnp.zeros((1, width), F32), jnp.zeros((DH, width), F32))


def _online_update(state, s, pv):
    m, l, acc = state
    m_new = jnp.maximum(m, jnp.max(s, axis=0, keepdims=True))
    alpha = jnp.exp2((m - m_new) * EXP2_SCALE)
    p = jnp.exp2((s - m_new) * EXP2_SCALE)
    l = alpha * l + jnp.sum(p, axis=0, keepdims=True)
    acc = alpha * acc + pv(p.astype(BF16))
    return m_new, l, acc


def _nsa_kernel(q_ref, kc_ref, vc_ref, ks_ref, vs_ref, kw_ref, vw_ref, gt_ref, z_ref, ovt_ref, diagb_ref,
                winb_ref, o_ref, vst_sc, vwt_sc, bias_sc, *, seq, tq):
    i = pl.program_id(1)
    width = NH * tq
    ck = vst_sc.shape[2]
    per_chunk = ck // SEL_BLOCK
    n_cmp = seq // CMP_STRIDE - 1
    n_sel = seq // SEL_BLOCK
    n_rows = bias_sc.shape[0]

    @pl.when(i == 0)
    def _():
        for c in range(seq // ck):
            vst_sc[c] = jnp.transpose(vs_ref[c * ck:(c + 1) * ck, :].astype(F32)).astype(BF16)
        for c in range(seq // tq):
            vwt_sc[c] = jnp.transpose(vw_ref[c * tq:(c + 1) * tq, :].astype(F32)).astype(BF16)

    q = q_ref[...]
    qs = jnp.concatenate([q[:, h * DH:(h + 1) * DH] for h in range(NH)], axis=0)

    def tok(rows):
        return i * tq + (lax.broadcasted_iota(jnp.int32, (rows, width), 1) & (tq - 1))

    def row(rows, w=width):
        return lax.broadcasted_iota(jnp.int32, (rows, w), 0)

    rown = row(LANES)
    mask_c = jnp.logical_and(rown * CMP_STRIDE + (CMP_BLOCK - 1) <= tok(LANES), rown < n_cmp)
    s = jnp.where(mask_c, _dot_nt(kc_ref[...], qs), NEG)
    e = jnp.exp2((s - jnp.max(s, axis=0, keepdims=True)) * EXP2_SCALE)
    p_c = jnp.where(mask_c, e, 0.0) * (1.0 / jnp.sum(e, axis=0, keepdims=True))
    o_c = lax.dot_general(vc_ref[...], p_c.astype(BF16), TN, preferred_element_type=F32)

    p_sum = p_c[:, 0:tq]
    for h in range(1, NH):
        p_sum = p_sum + p_c[:, h * tq:(h + 1) * tq]
    p_hi, p_lo = _split_hi_lo(p_sum)
    imp = (_dot(ovt_ref[...], p_hi) + _dot(ovt_ref[...], p_lo))[0:n_rows]
    rowj = row(n_rows, tq)
    t_q = i * tq + lax.broadcasted_iota(jnp.int32, (n_rows, tq), 1)
    cur = t_q >> 6
    forced = jnp.logical_or(rowj == 0, jnp.logical_or(rowj == cur, rowj == cur - 1))
    valid = rowj * SEL_BLOCK <= t_q
    score = jnp.where(forced, FORCE_SCORE, jnp.where(valid, imp, NEG))
    score = jnp.where(rowj < n_sel, score, -jnp.inf)
    bias = jnp.where(_topk_rows(score, min(SEL_TOPK, n_sel)), 0.0, NEG)
    bias_sc[...] = jnp.concatenate([bias] * NH, axis=1)

    def chunk_scores(c):
        ks = pl.multiple_of(c * ck, ck)
        s = _dot_nt(ks_ref[pl.ds(ks, ck), :], qs)
        return jnp.concatenate(
            [s[b * SEL_BLOCK:(b + 1) * SEL_BLOCK] + bias_sc[pl.ds(c * per_chunk + b, 1), :]
             for b in range(per_chunk)], axis=0)

    diag = (i * tq) // ck
    s = chunk_scores(diag) + diagb_ref[i % (ck // tq)]
    state = _online_update(_online_init(width), s, lambda p: _dot(vst_sc[diag], p))

    def body(c, st):
        return _online_update(st, chunk_scores(c), lambda p: _dot(vst_sc[c], p))

    _, l_s, acc_s = lax.fori_loop(0, diag, body, state)
    o_s = acc_s * (1.0 / l_s)

    n_wblk = winb_ref.shape[0]
    parts, blocks = [], []
    for b in range(n_wblk):
        blk = i - (n_wblk - 1) + b
        blk_c = jnp.maximum(blk, 0)
        s = _dot_nt(kw_ref[pl.ds(pl.multiple_of(blk_c * tq, tq), tq), :], qs)
        parts.append(s + winb_ref[b] + jnp.where(blk >= 0, 0.0, NEG))
        blocks.append(blk_c)
    s = jnp.concatenate(parts, axis=0)
    e = jnp.exp2((s - jnp.max(s, axis=0, keepdims=True)) * EXP2_SCALE)
    inv_w = 1.0 / jnp.sum(e, axis=0, keepdims=True)
    e = e.astype(BF16)
    o_w = _dot(vwt_sc[blocks[0]], e[0:tq])
    for b in range(1, n_wblk):
        o_w = o_w + _dot(vwt_sc[blocks[b]], e[b * tq:(b + 1) * tq])
    o_w = o_w * inv_w

    g = jax.nn.sigmoid(gt_ref[0:16, :])
    for h in range(NH):
        ls = slice(h * tq, (h + 1) * tq)
        cs = slice(h * DH, (h + 1) * DH)
        mix = (g[3 * h:3 * h + 1] * o_c[:, ls] + g[3 * h + 1:3 * h + 2] * o_s[:, ls]
               + g[3 * h + 2:3 * h + 3] * o_w[:, ls])
        o_ref[:, cs] = (jnp.transpose(mix) * _silu(z_ref[:, cs].astype(F32))).astype(o_ref.dtype)


def _nsa_bias_tables(tq, ck):
    t = jnp.arange(NH * tq)[None, :] % tq
    r = jnp.arange(ck)[:, None]
    diag = jnp.stack([jnp.where(r <= ph * tq + t, 0.0, NEG) for ph in range(ck // tq)])
    r = jnp.arange(tq)[:, None]
    n_wblk = WIN // tq + 1
    win = []
    for b in range(n_wblk):
        diff = t - (r + (b - (n_wblk - 1)) * tq)
        win.append(jnp.where((diff >= 0) & (diff < WIN), 0.0, NEG))
    return diag.astype(F32), jnp.stack(win).astype(F32)


def _nsa(ga, gb, gc, gd_t, kc, vc, ov_t, batch, seq):
    tq = 256
    ck = 256
    nq = seq // tq
    n_rows = 32
    assert seq // SEL_BLOCK <= n_rows and seq % ck == 0
    row = lambda b, i: (b * nq + i, 0)
    kv = lambda col: pl.BlockSpec((seq, DH), lambda b, i: (b, col))
    cmp_spec = pl.BlockSpec((None, LANES, DH), lambda b, i: (b, 0, 0))
    full = lambda a: pl.BlockSpec(a.shape, lambda b, i: (0,) * a.ndim)
    diag_b, win_b = _nsa_bias_tables(tq, ck)
    return pl.pallas_call(
        functools.partial(_nsa_kernel, seq=seq, tq=tq),
        out_shape=jax.ShapeDtypeStruct((batch * seq, GROUP), BF16),
        grid=(batch, nq),
        in_specs=[pl.BlockSpec((tq, GROUP), row), cmp_spec, cmp_spec,
                  kv(5), kv(1), kv(6), kv(2),
                  pl.BlockSpec((LANES, tq), lambda b, i: (0, b * nq + i)),
                  pl.BlockSpec((tq, GROUP), row), full(ov_t), full(diag_b), full(win_b)],
        out_specs=pl.BlockSpec((tq, GROUP), row),
        scratch_shapes=[pltpu.VMEM((seq // ck, DH, ck), BF16), pltpu.VMEM((seq // tq, DH, tq), BF16),
                        pltpu.VMEM((n_rows, NH * tq), F32)],
        compiler_params=_cparams(("parallel", "arbitrary")),
        name="nsa_attention",
    )(ga, kc, vc, ga, gb, ga, gb, gd_t, gc, ov_t, diag_b, win_b)


def _mlstm_kernel(q_ref, k_ref, v_ref, gcol_ref, grow_ref, bcol_ref, brow_ref, og_ref, z_ref, ng_ref,
                  tril_ref, triu_ref, out_ref, c_sc, n_sc, m_sc, *, tc):
    L = MLSTM_CHUNK

    @pl.when(pl.program_id(1) == 0)
    def _():
        c_sc[...] = jnp.zeros(c_sc.shape, F32)
        n_sc[...] = jnp.zeros(n_sc.shape, F32)
        m_sc[...] = jnp.full(m_sc.shape, NEG, F32)

    gcol = gcol_ref[...] + bcol_ref[...]
    grow = grow_ref[...] + brow_ref[...]
    hi, lo = _split_hi_lo(_log_sigmoid(gcol))
    bcol_all = _dot(tril_ref[...], hi) + _dot(tril_ref[...], lo)
    hi, lo = _split_hi_lo(_log_sigmoid(grow))
    brow_all = _dot(hi, triu_ref[...]) + _dot(lo, triu_ref[...])

    causal = (lax.broadcasted_iota(jnp.int32, (L, L), 1) <= lax.broadcasted_iota(jnp.int32, (L, L), 0))

    for c in range(tc // L):
        rs = slice(c * L, (c + 1) * L)
        for h in range(NH):
            cs = slice(h * DH, (h + 1) * DH)
            qh = q_ref[rs, cs]
            kh = k_ref[rs, cs]
            vh = v_ref[rs, cs]
            ig_col = gcol[rs, GD_I + h:GD_I + h + 1]
            b_col = bcol_all[rs, GD_F + h:GD_F + h + 1]
            ig_row = grow[h:h + 1, rs]
            b_row = brow_all[NH + h:NH + h + 1, rs]
            c_prev = c_sc[h]
            n_prev = n_sc[h]
            m_prev = m_sc[h][:, 0:1]

            d_log = jnp.where(causal, b_col - b_row + ig_row, NEG)
            m_intra = jnp.max(d_log, axis=-1, keepdims=True)
            m_inter = b_col + m_prev
            m_t = jnp.maximum(m_inter, m_intra)
            w_inter = jnp.exp(m_inter - m_t)
            qk = _dot_nt(qh, kh) * SCALE * jnp.exp(d_log - m_t)
            num = _dot(qk.astype(BF16), vh) + w_inter * _dot(qh, c_prev.astype(BF16))
            den = (jnp.sum(qk, axis=-1, keepdims=True)
                   + w_inter * jnp.sum(qh.astype(F32) * n_prev, axis=-1, keepdims=True))
            hh = num / jnp.maximum(jnp.abs(den), jnp.exp(-m_t))
            hh = jax.nn.sigmoid(og_ref[rs, cs].astype(F32)) * hh
            mu = jnp.mean(hh, axis=-1, keepdims=True)
            var = jnp.mean(jnp.square(hh - mu), axis=-1, keepdims=True)
            hn = (hh - mu) * lax.rsqrt(var + LN_EPS) * ng_ref[:, cs]
            out_ref[rs, cs] = (hn * _silu(z_ref[rs, cs].astype(F32))).astype(out_ref.dtype)

            b_last = b_row[:, L - 1:L]
            m_loc = jnp.max(b_last - b_row + ig_row, axis=-1, keepdims=True)
            e_col = jnp.exp(b_last - b_col + ig_col - m_loc)
            ek = (e_col * SCALE) * kh.astype(F32)
            g_c = lax.dot_general(ek.astype(BF16), vh, TN, preferred_element_type=F32)
            g_n = jnp.sum(ek, axis=0, keepdims=True)
            m_new = jnp.maximum(b_last + m_prev, m_loc)
            s_old = jnp.exp(b_last + m_prev - m_new)
            s_new = jnp.exp(m_loc - m_new)
            c_sc[h] = s_old * c_prev + s_new * g_c
            n_sc[h] = s_old * n_prev + s_new * g_n
            m_sc[h] = jnp.broadcast_to(m_new, (1, LANES))


def _mlstm(gb, gc, gd, gd_t, bias_col, bias_row, norm_g, tril, triu, batch, seq):
    tc = MLSTM_CHUNK
    nt = seq // tc
    row = lambda col: (lambda b, j: (b * nt + j, col))
    full = lambda a: pl.BlockSpec(a.shape, lambda b, j: (0,) * a.ndim)
    return pl.pallas_call(
        functools.partial(_mlstm_kernel, tc=tc),
        out_shape=jax.ShapeDtypeStruct((batch * seq, GROUP), BF16),
        grid=(batch, nt),
        in_specs=[pl.BlockSpec((tc, GROUP), row(1)), pl.BlockSpec((tc, GROUP), row(2)),
                  pl.BlockSpec((tc, GROUP), row(3)),
                  pl.BlockSpec((tc, LANES), row(0)),
                  pl.BlockSpec((2 * NH, tc), lambda b, j: (GD_I // (2 * NH), b * nt + j)),
                  full(bias_col), full(bias_row),
                  pl.BlockSpec((tc, GROUP), row(1)), pl.BlockSpec((tc, GROUP), row(2)),
                  full(norm_g), full(tril), full(triu)],
        out_specs=pl.BlockSpec((tc, GROUP), row(0)),
        scratch_shapes=[pltpu.VMEM((NH, DH, DH), F32), pltpu.VMEM((NH, 1, DH), F32),
                        pltpu.VMEM((NH, 1, LANES), F32)],
        compiler_params=_cparams(("parallel", "arbitrary")),
        name="mlstm",
    )(gb, gb, gb, gd, gd_t, bias_col, bias_row, gc, gc, norm_g, tril, triu)


def _lru_kernel(x_ref, z_ref, cw_ref, cb_ref, gw_ref, gb_ref, lam_ref, out_ref, xbuf, h_sc, *, tl):
    pad = 8

    @pl.when(pl.program_id(1) == 0)
    def _():
        xbuf[0:pad, :] = jnp.zeros((pad, GROUP), F32)
        h_sc[...] = jnp.zeros(h_sc.shape, F32)

    x = x_ref[...].astype(F32)
    xbuf[pad:pad + tl, :] = x
    u = cw_ref[3:4, :] * x + cb_ref[...]
    for w in range(3):
        u = u + cw_ref[w:w + 1, :] * xbuf[pad - 3 + w:pad - 3 + w + tl, :]
    xbuf[0:pad, :] = x[tl - pad:tl, :]

    ub = u.astype(BF16)
    pre = []
    for gi in range(2):
        pre.append(jnp.concatenate(
            [_dot(ub[:, n * DH:(n + 1) * DH], gw_ref[gi, n]) for n in range(NH)], axis=1) + gb_ref[gi:gi + 1, :])
    r = jax.nn.sigmoid(pre[0])
    ig = jax.nn.sigmoid(pre[1])
    neg_lam = -lam_ref[...]
    softplus = jnp.maximum(neg_lam, 0.0) + jnp.log(1.0 + jnp.exp(-jnp.abs(neg_lam)))
    a = jnp.exp(-LRU_C * r * softplus)
    bx = jnp.sqrt(1.0 - a * a) * (ig * u)

    rowi = lax.broadcasted_iota(jnp.int32, (tl, GROUP), 0)
    d = 1
    while d < tl:
        keep = rowi >= d
        a_sh = jnp.where(keep, pltpu.roll(a, d, 0), 1.0)
        b_sh = jnp.where(keep, pltpu.roll(bx, d, 0), 0.0)
        bx = a * b_sh + bx
        a = a * a_sh
        d *= 2
    hseq = a * h_sc[...] + bx
    h_sc[...] = hseq[tl - 1:tl, :]
    out_ref[...] = (hseq * _silu(z_ref[...].astype(F32))).astype(out_ref.dtype)


def _lru(gc, conv_w, conv_b, gate_w, gate_b, lam, batch, seq):
    tl = 256
    nt = seq // tl
    row = lambda col: (lambda b, j: (b * nt + j, col))
    full = lambda a: pl.BlockSpec(a.shape, lambda b, j: (0,) * a.ndim)
    return pl.pallas_call(
        functools.partial(_lru_kernel, tl=tl),
        out_shape=jax.ShapeDtypeStruct((batch * seq, GROUP), BF16),
        grid=(batch, nt),
        in_specs=[pl.BlockSpec((tl, GROUP), row(3)), pl.BlockSpec((tl, GROUP), row(4)),
                  full(conv_w), full(conv_b), full(gate_w), full(gate_b), full(lam)],
        out_specs=pl.BlockSpec((tl, GROUP), row(0)),
        scratch_shapes=[pltpu.VMEM((tl + 8, GROUP), F32), pltpu.VMEM((1, GROUP), F32)],
        compiler_params=_cparams(("parallel", "arbitrary")),
        name="rglru",
    )(gc, gc, conv_w, conv_b, gate_w, gate_b, lam)


def _moba_kernel(q_ref, k_ref, v_ref, z_ref, o_ref, kmean_sc, bias_sc, vt_sc, *, seq):
    i = pl.program_id(1)
    tq = MOBA_BLOCK
    nb = seq // MOBA_BLOCK
    ncand = bias_sc.shape[1]
    pair = 2 * tq

    @pl.when(i == 0)
    def _():
        kmean_sc[...] = jnp.zeros(kmean_sc.shape, F32)
        for h in range(NH):
            kf = k_ref[:, h * DH:(h + 1) * DH].astype(F32).reshape(nb, MOBA_BLOCK, DH)
            kmean_sc[h, 0:nb, :] = jnp.mean(kf, axis=1)
            for c in range(nb // 2):
                vt_sc[h, c] = jnp.transpose(
                    v_ref[c * pair:(c + 1) * pair, h * DH:(h + 1) * DH].astype(F32)).astype(BF16)

    past = lax.broadcasted_iota(jnp.int32, (ncand, tq), 0) < i
    causal = (lax.broadcasted_iota(jnp.int32, (tq, tq), 0) <= lax.broadcasted_iota(jnp.int32, (tq, tq), 1))
    own = pl.multiple_of(i * tq, tq)
    heads = [slice(h * DH, (h + 1) * DH) for h in range(NH)]
    qs = [q_ref[:, cs] for cs in heads]
    states = []
    for h, cs in enumerate(heads):
        gate = _dot_nt(kmean_sc[h].astype(BF16), qs[h])[0:ncand]
        score = jnp.where(past, gate, NEG)
        sel = jnp.logical_and(_topk_rows(score, min(MOBA_TOPK, nb)), past)
        bias_sc[h] = jnp.where(sel, 0.0, NEG)
        s = jnp.where(causal, _dot_nt(k_ref[pl.ds(own, tq), cs], qs[h]), NEG)
        states.append(_online_update(
            _online_init(tq), s,
            lambda p, cs=cs: lax.dot_general(v_ref[pl.ds(own, tq), cs], p, TN, preferred_element_type=F32)))

    def body(j, states):
        ks = pl.multiple_of(j * pair, pair)
        out = []
        for h, cs in enumerate(heads):
            s = _dot_nt(k_ref[pl.ds(ks, pair), cs], qs[h])
            s = jnp.concatenate([s[0:tq] + bias_sc[h, pl.ds(2 * j, 1), :],
                                 s[tq:pair] + bias_sc[h, pl.ds(2 * j + 1, 1), :]], axis=0)
            out.append(_online_update(states[h], s, lambda p, h=h: _dot(vt_sc[h, j], p)))
        return tuple(out)

    states = lax.fori_loop(0, (i + 1) // 2, body, tuple(states))
    for h, cs in enumerate(heads):
        _, l, acc = states[h]
        o = jnp.transpose(acc * (1.0 / l))
        o_ref[:, cs] = (o * _silu(z_ref[:, cs].astype(F32))).astype(o_ref.dtype)


def _moba(ga, gb, gc, batch, seq):
    tq = MOBA_BLOCK
    nq = seq // tq
    ncand = 8
    assert nq <= ncand and nq % 2 == 0
    row = lambda col: (lambda b, i: (b * nq + i, col))
    return pl.pallas_call(
        functools.partial(_moba_kernel, seq=seq),
        out_shape=jax.ShapeDtypeStruct((batch * seq, GROUP), BF16),
        grid=(batch, nq),
        in_specs=[pl.BlockSpec((tq, GROUP), row(2)),
                  pl.BlockSpec((seq, GROUP), lambda b, i: (b, 3)),
                  pl.BlockSpec((seq, GROUP), lambda b, i: (b, 4)),
                  pl.BlockSpec((tq, GROUP), row(5))],
        out_specs=pl.BlockSpec((tq, GROUP), row(0)),
        scratch_shapes=[pltpu.VMEM((NH, LANES, DH), F32), pltpu.VMEM((NH, ncand, tq), F32),
                        pltpu.VMEM((NH, nq // 2, DH, 2 * tq), BF16)],
        compiler_params=_cparams(("parallel", "arbitrary")),
        name="moba",
    )(ga, ga, gb, gc)


def _out_kernel(ya_ref, yb_ref, yc_ref, yd_ref, w_ref, x_ref, g_ref, b_ref, o_ref, ob_ref):
    acc = DEEPNORM_ALPHA * x_ref[...]
    for p, y_ref in enumerate((ya_ref, yb_ref, yc_ref, yd_ref)):
        acc = acc + _dot(y_ref[...], w_ref[p * GROUP:(p + 1) * GROUP, :])
    mu = jnp.mean(acc, axis=-1, keepdims=True)
    var = jnp.mean(jnp.square(acc - mu), axis=-1, keepdims=True)
    y = (acc - mu) * lax.rsqrt(var + LN_EPS) * g_ref[...] + b_ref[...]
    o_ref[...] = y
    ob_ref[...] = y.astype(BF16)


def _out_proj(ys, w_out, x, ln_g, ln_b):
    m, d = x.shape
    tm = 512
    yspec =pl.BlockSpec((tm, GROUP), lambda i: (i, 0))
    full = lambda a: pl.BlockSpec(a.shape, lambda i: (0,) * a.ndim)
    xspec = pl.BlockSpec((tm, d), lambda i: (i, 0))
    return pl.pallas_call(
        _out_kernel,
        out_shape=(jax.ShapeDtypeStruct((m, d), F32), jax.ShapeDtypeStruct((m, d), BF16)),
        grid=(m // tm,),
        in_specs=[yspec] * 4 + [full(w_out), xspec, full(ln_g), full(ln_b)],
        out_specs=(xspec, xspec),
        compiler_params=_cparams(("parallel",)),
        name="out_proj_ln",
    )(*ys, w_out, x, ln_g, ln_b)


_GROUPS = (
    (('nsa_q', 0, 512), ('nsa_kv', 0, 384), (None, 0, 128), ('moba_qkv', 0, 1024)),
    (('nsa_kv', 384, 384), (None, 0, 128), ('mlstm_qkv', 0, 1536), ('moba_qkv', 1024, 512)),
    (('nsa_z', 0, 512), ('mlstm_o', 0, 512), ('mlstm_z', 0, 512), ('lru_x', 0, 512), ('lru_z', 0, 512),
     ('moba_z', 0, 512)),
)
_GATE_PIECES = (('nsa_gate', 0, 12), (None, 0, 4), ('mlstm_if', 0, 8), (None, 0, LANES - 24))


def _regroup_kernel(w_ref, *out_refs):
    for pieces, o_ref in zip(_GROUPS + (_GATE_PIECES,), out_refs):
        dst = 0
        for name, off, width in pieces:
            if name is None:
                o_ref[dst:dst + width, :] = jnp.zeros((width, o_ref.shape[1]), o_ref.dtype)
            else:
                src = _OFF[name] + off
                o_ref[dst:dst + width, :] = w_ref[src:src + width, :].astype(o_ref.dtype)
            dst += width


def _regroup_w_in(w_in):
    depth, d, n_in = w_in.shape
    wt = jnp.transpose(w_in, (0, 2, 1))
    tk = 256
    widths = [sum(p[2] for p in pieces) for pieces in _GROUPS] + [LANES]
    dtypes = [BF16] * len(_GROUPS) + [F32]
    groups = pl.pallas_call(
        _regroup_kernel,
        out_shape=tuple(jax.ShapeDtypeStruct((depth, n, d), dt) for n, dt in zip(widths, dtypes)),
        grid=(depth, d // tk),
        in_specs=[pl.BlockSpec((None, n_in, tk), lambda l, i: (l, 0, i))],
        out_specs=tuple(pl.BlockSpec((None, n, tk), lambda l, i: (l, 0, i)) for n in widths),
        compiler_params=_cparams(("parallel", "parallel")),
        name="regroup_w_in",
    )(wt)
    return list(groups[:-1]) + [groups[-1].astype(BF16)]


def _rope_tables(seq):
    half = ROT_DIM // 2
    inv_freq = jnp.power(ROPE_THETA, -jnp.arange(half, dtype=F32) * (2.0 / ROT_DIM))
    ang = jnp.arange(seq, dtype=jnp.int32).astype(F32)[:, None] * inv_freq[None, :]
    cos, sin = jnp.cos(ang), jnp.sin(ang)
    ones = jnp.ones((seq, DH - ROT_DIM), F32)
    return (jnp.concatenate([cos, cos, ones], axis=1),
            jnp.concatenate([-sin, sin, 0.0 * ones], axis=1))


def _cmp_overlap_t(seq):
    n_cmp = seq // CMP_STRIDE - 1
    n_sel = seq // SEL_BLOCK
    cs = jnp.arange(LANES)[None, :] * CMP_STRIDE
    ss = jnp.arange(LANES)[:, None] * SEL_BLOCK
    ov = (cs < ss + SEL_BLOCK) & (cs + CMP_BLOCK > ss)
    ov = ov & (jnp.arange(LANES)[None, :] < n_cmp) & (jnp.arange(LANES)[:, None] < n_sel)
    return ov.astype(BF16)


def _chunk_tri(tc):
    r = jnp.arange(tc)
    same = (r[:, None] // MLSTM_CHUNK) == (r[None, :] // MLSTM_CHUNK)
    tril = (same & (r[None, :] <= r[:, None])).astype(BF16)
    return tril, tril.T


def _layer(x, xb, batch, seq, consts, w_groups, cmp_w1, cmp_w2, cmp_pe, i_bias, f_bias, norm_g,
           conv_w, conv_b, gate_w, gate_b, lam, w_out, ln_g, ln_b):
    rope, ov_t, tril, triu = consts
    wa, wb, wc, wd = w_groups
    tm = min(2048, seq)
    ga = _project(xb, wa, BF16, tm, 512, rope=rope, seq=seq)
    gb = _project(xb, wb, BF16, tm, 512)
    gc = _project(xb, wc, BF16, tm, 512)
    gd = _project(xb, wd, F32, tm, LANES)
    gd_t = _project_t(xb, wd, F32, tm)

    nb = seq // CMP_STRIDE
    tk = ga[:, 4 * DH:5 * DH].reshape(batch, nb, CMP_STRIDE * DH)
    tv = gb[:, 0:DH].reshape(batch, nb, CMP_STRIDE * DH)
    pe = cmp_pe.reshape(2, 2, CMP_STRIDE * DH)
    kc, vc = _compress(tk, tv, cmp_w1.astype(BF16), cmp_w2.astype(BF16), pe)
    y_a = _nsa(ga, gb, gc, gd_t, kc, vc, ov_t, batch, seq)

    bias = jnp.concatenate([i_bias, f_bias])
    bias_col = jnp.zeros((1, LANES), F32).at[0, GD_I:GD_I + 2 * NH].set(bias)
    bias_row = bias[:, None]
    y_b = _mlstm(gb, gc, gd, gd_t, bias_col, bias_row, norm_g[None, :], tril, triu, batch, seq)

    y_c = _lru(gc, conv_w, conv_b[None, :], gate_w.astype(BF16), gate_b, lam[None, :], batch, seq)

    y_d = _moba(ga, gb, gc, batch, seq)

    return _out_proj((y_a, y_b, y_c, y_d), w_out.astype(BF16), x, ln_g[None, :], ln_b[None, :])


def kernel(x, w_in, nsa_cmp_w1, nsa_cmp_w2, nsa_cmp_pe, mlstm_i_bias, mlstm_f_bias, mlstm_norm_g,
           lru_conv_w, lru_conv_b, lru_gate_w, lru_gate_b, lru_lambda, w_out, ln_g, ln_b):
    batch, seq, d = x.shape
    tril, triu = _chunk_tri(MLSTM_CHUNK)
    consts = (_rope_tables(seq), _cmp_overlap_t(seq), tril, triu)
    xf = x.reshape(batch * seq, d)
    xb = xf.astype(BF16)
    w_groups = _regroup_w_in(w_in)
    for l in range(w_in.shape[0]):
        xf, xb = _layer(xf, xb, batch, seq, consts, [w[l] for w in w_groups], nsa_cmp_w1[l], nsa_cmp_w2[l], nsa_cmp_pe[l],
                        mlstm_i_bias[l], mlstm_f_bias[l], mlstm_norm_g[l], lru_conv_w[l], lru_conv_b[l],
                        lru_gate_w[l], lru_gate_b[l], lru_lambda[l], w_out[l], ln_g[l], ln_b[l])
    return xf.reshape(batch, seq, d)
```

```python
import functools
import math

import jax
import jax.numpy as jnp
from jax import lax
from jax.experimental import pallas as pl
from jax.experimental.pallas import tpu as pltpu

F32 = jnp.float32
BF16 = jnp.bfloat16

D_MODEL = 2048
DEPTH = 2
GROUP = 512
DH = 128
NH = 4
ROT_DIM = 32
ROPE_THETA = 500000.0

CMP_BLOCK = 32
CMP_STRIDE = 16
SEL_BLOCK = 64
SEL_TOPK = 8
WIN = 256
FORCE_SCORE = 1e9

MLSTM_CHUNK = 512
LRU_C = 8.0
MOBA_BLOCK = 256
MOBA_TOPK = 3

DEEPNORM_ALPHA = (2 * DEPTH) ** 0.25
NEG = -1e30
LN_EPS = 1e-5
SCALE = DH ** -0.5
EXP2_SCALE = SCALE * math.log2(math.e)

LANES = 128
VMEM_LIMIT = 56 * 1024 * 1024

_OFF = {}
_o = 0
for _name, _w in (('nsa_q', 512), ('nsa_kv', 768), ('nsa_gate', 12), ('nsa_z', 512), ('mlstm_qkv', 1536),
                  ('mlstm_if', 8), ('mlstm_o', 512), ('mlstm_z', 512), ('lru_x', 512), ('lru_z', 512),
                  ('moba_qkv', 1536), ('moba_z', 512)):
    _OFF[_name] = _o
    _o += _w

GD_I = 16
GD_F = 20

NT = (((1,), (1,)), ((), ()))
TN = (((0,), (0,)), ((), ()))


def _cparams(sem):
    return pltpu.CompilerParams(dimension_semantics=sem, vmem_limit_bytes=VMEM_LIMIT)


def _dot(a, b):
    return jnp.dot(a, b, preferred_element_type=F32)


def _dot_nt(a, b):
    return lax.dot_general(a, b, NT, preferred_element_type=F32)


def _split_hi_lo(a):
    hi = a.astype(BF16)
    lo = (a - hi.astype(F32)).astype(BF16)
    return hi, lo


def _silu(x):
    return x * jax.nn.sigmoid(x)


def _log_sigmoid(x):
    return jnp.minimum(x, 0.0) - jnp.log(1.0 + jnp.exp(-jnp.abs(x)))


def _proj_kernel(x_ref, w_ref, *rest, rotary):
    if rotary:
        cos_ref, sin_ref, o_ref = rest
    else:
        (o_ref,) = rest
    acc = _dot_nt(x_ref[...], w_ref[...])
    if not rotary:
        o_ref[...] = acc.astype(o_ref.dtype)
        return
    c = cos_ref[...]
    s = sin_ref[...]
    lane = lax.broadcasted_iota(jnp.int32, c.shape, 1)
    half = ROT_DIM // 2
    for h in range(acc.shape[1] // DH):
        t = acc[:, h * DH:(h + 1) * DH]
        swapped = jnp.where(lane < half, pltpu.roll(t, DH - half, 1), pltpu.roll(t, half, 1))
        o_ref[:, h * DH:(h + 1) * DH] = (t * c + swapped * s).astype(o_ref.dtype)


def _project(xb, w, out_dtype, tm, tn, rope=None, seq=None):
    m, k = xb.shape
    n = w.shape[0]
    tn = min(tn, n)
    in_specs = [pl.BlockSpec((tm, k), lambda i, j: (i, 0)),
                pl.BlockSpec((tn, k), lambda i, j: (j, 0))]
    args = [xb, w]
    if rope is not None:
        nrep = seq // tm
        in_specs += [pl.BlockSpec((tm, DH), lambda i, j: (i % nrep, 0))] * 2
        args += list(rope)
    return pl.pallas_call(
        functools.partial(_proj_kernel, rotary=rope is not None),
        out_shape=jax.ShapeDtypeStruct((m, n), out_dtype),
        grid=(m // tm, n // tn),
        in_specs=in_specs,
        out_specs=pl.BlockSpec((tm, tn), lambda i, j: (i, j)),
        compiler_params=_cparams(("parallel", "arbitrary")),
        name="in_proj_rot" if rope is not None else "in_proj",
    )(*args)


def _proj_t_kernel(wt_ref, x_ref, o_ref):
    o_ref[...] = _dot_nt(wt_ref[...], x_ref[...]).astype(o_ref.dtype)


def _project_t(xb, wt, out_dtype, tm):
    m, k = xb.shape
    n = wt.shape[0]
    return pl.pallas_call(
        _proj_t_kernel,
        out_shape=jax.ShapeDtypeStruct((n, m), out_dtype),
        grid=(m // tm,),
        in_specs=[pl.BlockSpec((n, k), lambda i: (0, 0)), pl.BlockSpec((tm, k), lambda i: (i, 0))],
        out_specs=pl.BlockSpec((n, tm), lambda i: (0, i)),
        compiler_params=_cparams(("parallel",)),
        name="in_proj_t",
    )(wt, xb)


def _compress_kernel(tk_ref, tv_ref, w1_ref, w2_ref, pe_ref, kc_ref, vc_ref, *, nb):
    half = (CMP_BLOCK // 2) * DH
    for idx, (t_ref, o_ref) in enumerate(((tk_ref, kc_ref), (tv_ref, vc_ref))):
        t = t_ref[...].astype(F32)
        lo = (t + pe_ref[idx, 0:1, :]).astype(BF16)
        hi = (t + pe_ref[idx, 1:2, :]).astype(BF16)
        a = _dot(lo, w1_ref[idx, 0:half, :])
        b = _dot(hi, w1_ref[idx, half:2 * half, :])
        hid = _silu(a + pltpu.roll(b, nb - 1, 0))
        out = _dot(hid.astype(BF16), w2_ref[idx])
        o_ref[...] = jnp.zeros(o_ref.shape, o_ref.dtype)
        o_ref[0:nb, :] = out.astype(o_ref.dtype)


def _compress(tk, tv, w1, w2, pe):
    b, nb, width = tk.shape
    blk = pl.BlockSpec((None, nb, width), lambda i: (i, 0, 0))
    full = lambda a: pl.BlockSpec(a.shape, lambda i: (0,) * a.ndim)
    out = jax.ShapeDtypeStruct((b, LANES, DH), BF16)
    return pl.pallas_call(
        functools.partial(_compress_kernel, nb=nb),
        out_shape=(out, out),
        grid=(b,),
        in_specs=[blk, blk, full(w1), full(w2), full(pe)],
        out_specs=(pl.BlockSpec((None, LANES, DH), lambda i: (i, 0, 0)),) * 2,
        compiler_params=_cparams(("parallel",)),
        name="nsa_compress",
    )(tk, tv, w1, w2, pe)


def _topk_rows(score, k_top):
    rowi = lax.broadcasted_iota(jnp.int32, score.shape, 0)
    rank = jnp.zeros(score.shape, F32)
    for k in range(score.shape[0]):
        sk = score[k:k + 1, :]
        beats = jnp.logical_or(sk > score, jnp.logical_and(sk == score, rowi > k))
        rank = rank + jnp.where(beats, 1.0, 0.0)
    return rank < k_top


def _softmax_rows(s):
    p = jnp.exp2((s - jnp.max(s, axis=0, keepdims=True)) * EXP2_SCALE)
    return p.astype(BF16), 1.0 / jnp.sum(p, axis=0, keepdims=True)


def _nsa_tile(i, q_ref, kc_ref, vc_ref, ks_ref, kw_ref, gt_ref, z_ref, ovt_ref, diagb_ref, winb_ref,
              o_ref, vst_sc, vwt_sc, *, seq, tq):
    width = NH * tq
    n_cmp = seq // CMP_STRIDE - 1
    n_sel = seq // SEL_BLOCK
    n_rows = 32
    q = q_ref[...]
    qs = jnp.concatenate([q[:, h * DH:(h + 1) * DH] for h in range(NH)], axis=0)

    rown = lax.broadcasted_iota(jnp.int32, (LANES, width), 0)
    tok = i * tq + (lax.broadcasted_iota(jnp.int32, (LANES, width), 1) & (tq - 1))
    mask_c = jnp.logical_and(rown * CMP_STRIDE + (CMP_BLOCK - 1) <= tok, rown < n_cmp)
    s = jnp.where(mask_c, _dot_nt(kc_ref[...], qs), NEG)
    e = jnp.exp2((s - jnp.max(s, axis=0, keepdims=True)) * EXP2_SCALE)
    p_c = jnp.where(mask_c, e, 0.0) * (1.0 / jnp.sum(e, axis=0, keepdims=True))
    o_c = lax.dot_general(vc_ref[...], p_c.astype(BF16), TN, preferred_element_type=F32)

    p_sum = p_c[:, 0:tq]
    for h in range(1, NH):
        p_sum = p_sum + p_c[:, h * tq:(h + 1) * tq]
    p_hi, p_lo = _split_hi_lo(p_sum)
    imp = (_dot(ovt_ref[...], p_hi) + _dot(ovt_ref[...], p_lo))[0:n_rows]
    rowj = lax.broadcasted_iota(jnp.int32, (n_rows, tq), 0)
    t_q = i * tq + lax.broadcasted_iota(jnp.int32, (n_rows, tq), 1)
    cur = t_q >> 6
    forced = jnp.logical_or(rowj == 0, jnp.logical_or(rowj == cur, rowj == cur - 1))
    valid = rowj * SEL_BLOCK <= t_q
    score = jnp.where(forced, FORCE_SCORE, jnp.where(valid, imp, NEG))
    score = jnp.where(rowj < n_sel, score, -jnp.inf)
    bias = jnp.where(_topk_rows(score, min(SEL_TOPK, n_sel)), 0.0, NEG)
    bias = jnp.concatenate([bias] * NH, axis=1)

    nk = (i + 1) * tq
    s = _dot_nt(ks_ref[0:nk, :], qs)
    parts = [s[j * SEL_BLOCK:(j + 1) * SEL_BLOCK] + bias[j:j + 1] for j in range(nk // SEL_BLOCK)]
    own = jnp.concatenate(parts[i * tq // SEL_BLOCK:], axis=0) + diagb_ref[...]
    p, inv = _softmax_rows(jnp.concatenate(parts[:i * tq // SEL_BLOCK] + [own], axis=0))
    o_s = _dot(vst_sc[:, 0:nk], p) * inv

    n_wblk = winb_ref.shape[0]
    blocks = [b for b in range(i - n_wblk + 1, i + 1) if b >= 0]
    parts = [_dot_nt(kw_ref[b * tq:(b + 1) * tq, :], qs) + winb_ref[b - i + n_wblk - 1] for b in blocks]
    p, inv = _softmax_rows(jnp.concatenate(parts, axis=0))
    o_w = _dot(vwt_sc[:, blocks[0] * tq:nk], p) * inv

    g = jax.nn.sigmoid(gt_ref[0:16, :])
    for h in range(NH):
        ls = slice(h * tq, (h + 1) * tq)
        cs = slice(h * DH, (h + 1) * DH)
        mix = (g[3 * h:3 * h + 1] * o_c[:, ls] + g[3 * h + 1:3 * h + 2] * o_s[:, ls]
               + g[3 * h + 2:3 * h + 3] * o_w[:, ls])
        o_ref[:, cs] = (jnp.transpose(mix) * _silu(z_ref[:, cs].astype(F32))).astype(o_ref.dtype)


def _nsa_kernel(q_ref, kc_ref, vc_ref, ks_ref, vs_ref, kw_ref, vw_ref, gt_ref, z_ref, ovt_ref, diagb_ref,
                winb_ref, o_ref, vst_sc, vwt_sc, *, seq, tq):
    i = pl.program_id(1)

    @pl.when(i == 0)
    def _():
        for c in range(seq // tq):
            cols = slice(c * tq, (c + 1) * tq)
            vst_sc[:, cols] = jnp.transpose(vs_ref[cols, :].astype(F32)).astype(BF16)
            vwt_sc[:, cols] = jnp.transpose(vw_ref[cols, :].astype(F32)).astype(BF16)

    for c in range(seq // tq):
        pl.when(i == c)(functools.partial(
            _nsa_tile, c, q_ref, kc_ref, vc_ref, ks_ref, kw_ref, gt_ref, z_ref, ovt_ref, diagb_ref, winb_ref,
            o_ref, vst_sc, vwt_sc, seq=seq, tq=tq))


def _nsa_bias_tables(tq):
    t = jnp.arange(NH * tq)[None, :] % tq
    r = jnp.arange(tq)[:, None]
    diag = jnp.where(r <= t, 0.0, NEG)
    n_wblk = WIN // tq + 1
    win = []
    for b in range(n_wblk):
        diff = t - (r + (b - (n_wblk - 1)) * tq)
        win.append(jnp.where((diff >= 0) & (diff < WIN), 0.0, NEG))
    return diag.astype(F32), jnp.stack(win).astype(F32)


def _nsa(ga, gb, gc, gd_t, kc, vc, ov_t, batch, seq):
    tq = 256
    nq = seq // tq
    assert seq // SEL_BLOCK <= 32 and tq % SEL_BLOCK == 0
    row = lambda b, i: (b * nq + i, 0)
    kv = lambda col: pl.BlockSpec((seq, DH), lambda b, i: (b, col))
    cmp_spec = pl.BlockSpec((None, LANES, DH), lambda b, i: (b, 0, 0))
    full = lambda a: pl.BlockSpec(a.shape, lambda b, i: (0,) * a.ndim)
    diag_b, win_b = _nsa_bias_tables(tq)
    return pl.pallas_call(
        functools.partial(_nsa_kernel, seq=seq, tq=tq),
        out_shape=jax.ShapeDtypeStruct((batch * seq, GROUP), BF16),
        grid=(batch, nq),
        in_specs=[pl.BlockSpec((tq, GROUP), row), cmp_spec, cmp_spec,
                  kv(5), kv(1), kv(6), kv(2),
                  pl.BlockSpec((LANES, tq), lambda b, i: (0, b * nq + i)),
                  pl.BlockSpec((tq, GROUP), row), full(ov_t), full(diag_b), full(win_b)],
        out_specs=pl.BlockSpec((tq, GROUP), row),
        scratch_shapes=[pltpu.VMEM((DH, seq), BF16), pltpu.VMEM((DH, seq), BF16)],
        compiler_params=_cparams(("parallel", "arbitrary")),
        name="nsa_attention",
    )(ga, kc, vc, ga, gb, ga, gb, gd_t, gc, ov_t, diag_b, win_b)


def _mlstm_kernel(q_ref, k_ref, v_ref, gcol_ref, grow_ref, bcol_ref, brow_ref, og_ref, z_ref, ng_ref,
                  tril_ref, triu_ref, out_ref, c_sc, n_sc, m_sc, *, tc):
    L = MLSTM_CHUNK

    @pl.when(pl.program_id(1) == 0)
    def _():
        c_sc[...] = jnp.zeros(c_sc.shape, F32)
        n_sc[...] = jnp.zeros(n_sc.shape, F32)
        m_sc[...] = jnp.full(m_sc.shape, NEG, F32)

    gcol = gcol_ref[...] + bcol_ref[...]
    grow = grow_ref[...] + brow_ref[...]
    hi, lo = _split_hi_lo(_log_sigmoid(gcol))
    bcol_all = _dot(tril_ref[...], hi) + _dot(tril_ref[...], lo)
    hi, lo = _split_hi_lo(_log_sigmoid(grow))
    brow_all = _dot(hi, triu_ref[...]) + _dot(lo, triu_ref[...])

    causal = (lax.broadcasted_iota(jnp.int32, (L, L), 1) <= lax.broadcasted_iota(jnp.int32, (L, L), 0))

    for c in range(tc // L):
        rs = slice(c * L, (c + 1) * L)
        for h in range(NH):
            cs = slice(h * DH, (h + 1) * DH)
            qh = q_ref[rs, cs]
            kh = k_ref[rs, cs]
            vh = v_ref[rs, cs]
            ig_col = gcol[rs, GD_I + h:GD_I + h + 1]
            b_col = bcol_all[rs, GD_F + h:GD_F + h + 1]
            ig_row = grow[h:h + 1, rs]
            b_row = brow_all[NH + h:NH + h + 1, rs]
            c_prev = c_sc[h]
            n_prev = n_sc[h]
            m_prev = m_sc[h][:, 0:1]

            d_log = jnp.where(causal, b_col - b_row + ig_row, NEG)
            m_intra = jnp.max(d_log, axis=-1, keepdims=True)
            m_inter = b_col + m_prev
            m_t = jnp.maximum(m_inter, m_intra)
            w_inter = jnp.exp(m_inter - m_t)
            qk = _dot_nt(qh, kh) * SCALE * jnp.exp(d_log - m_t)
            num = _dot(qk.astype(BF16), vh) + w_inter * _dot(qh, c_prev.astype(BF16))
            den = (jnp.sum(qk, axis=-1, keepdims=True)
                   + w_inter * jnp.sum(qh.astype(F32) * n_prev, axis=-1, keepdims=True))
            hh = num / jnp.maximum(jnp.abs(den), jnp.exp(-m_t))
            hh = jax.nn.sigmoid(og_ref[rs, cs].astype(F32)) * hh
            mu = jnp.mean(hh, axis=-1, keepdims=True)
            var = jnp.mean(jnp.square(hh - mu), axis=-1, keepdims=True)
            hn = (hh - mu) * lax.rsqrt(var + LN_EPS) * ng_ref[:, cs]
            out_ref[rs, cs] = (hn * _silu(z_ref[rs, cs].astype(F32))).astype(out_ref.dtype)

            b_last = b_row[:, L - 1:L]
            m_loc = jnp.max(b_last - b_row + ig_row, axis=-1, keepdims=True)
            e_col = jnp.exp(b_last - b_col + ig_col - m_loc)
            ek = (e_col * SCALE) * kh.astype(F32)
            g_c = lax.dot_general(ek.astype(BF16), vh, TN, preferred_element_type=F32)
            g_n = jnp.sum(ek, axis=0, keepdims=True)
            m_new = jnp.maximum(b_last + m_prev, m_loc)
            s_old = jnp.exp(b_last + m_prev - m_new)
            s_new = jnp.exp(m_loc - m_new)
            c_sc[h] = s_old * c_prev + s_new * g_c
            n_sc[h] = s_old * n_prev + s_new * g_n
            m_sc[h] = jnp.broadcast_to(m_new, (1, LANES))


def _mlstm(gb, gc, gd, gd_t, bias_col, bias_row, norm_g, tril, triu, batch, seq):
    tc = MLSTM_CHUNK
    nt = seq // tc
    row = lambda col: (lambda b, j: (b * nt + j, col))
    full = lambda a: pl.BlockSpec(a.shape, lambda b, j: (0,) * a.ndim)
    return pl.pallas_call(
        functools.partial(_mlstm_kernel, tc=tc),
        out_shape=jax.ShapeDtypeStruct((batch * seq, GROUP), BF16),
        grid=(batch, nt),
        in_specs=[pl.BlockSpec((tc, GROUP), row(1)), pl.BlockSpec((tc, GROUP), row(2)),
                  pl.BlockSpec((tc, GROUP), row(3)),
                  pl.BlockSpec((tc, LANES), row(0)),
                  pl.BlockSpec((2 * NH, tc), lambda b, j: (GD_I // (2 * NH), b * nt + j)),
                  full(bias_col), full(bias_row),
                  pl.BlockSpec((tc, GROUP), row(1)), pl.BlockSpec((tc, GROUP), row(2)),
                  full(norm_g), full(tril), full(triu)],
        out_specs=pl.BlockSpec((tc, GROUP), row(0)),
        scratch_shapes=[pltpu.VMEM((NH, DH, DH), F32), pltpu.VMEM((NH, 1, DH), F32),
                        pltpu.VMEM((NH, 1, LANES), F32)],
        compiler_params=_cparams(("parallel", "arbitrary")),
        name="mlstm",
    )(gb, gb, gb, gd, gd_t, bias_col, bias_row, gc, gc, norm_g, tril, triu)


def _lru_kernel(x_ref, z_ref, cw_ref, cb_ref, gw_ref, gb_ref, lam_ref, out_ref, xbuf, h_sc, *, tl):
    pad = 8

    @pl.when(pl.program_id(1) == 0)
    def _():
        xbuf[0:pad, :] = jnp.zeros((pad, GROUP), F32)
        h_sc[...] = jnp.zeros(h_sc.shape, F32)

    x = x_ref[...].astype(F32)
    xbuf[pad:pad + tl, :] = x
    u = cw_ref[3:4, :] * x + cb_ref[...]
    for w in range(3):
        u = u + cw_ref[w:w + 1, :] * xbuf[pad - 3 + w:pad - 3 + w + tl, :]
    xbuf[0:pad, :] = x[tl - pad:tl, :]

    ub = u.astype(BF16)
    pre = []
    for gi in range(2):
        pre.append(jnp.concatenate(
            [_dot(ub[:, n * DH:(n + 1) * DH], gw_ref[gi, n]) for n in range(NH)], axis=1) + gb_ref[gi:gi + 1, :])
    r = jax.nn.sigmoid(pre[0])
    ig = jax.nn.sigmoid(pre[1])
    neg_lam = -lam_ref[...]
    softplus = jnp.maximum(neg_lam, 0.0) + jnp.log(1.0 + jnp.exp(-jnp.abs(neg_lam)))
    a = jnp.exp(-LRU_C * r * softplus)
    bx = jnp.sqrt(1.0 - a * a) * (ig * u)

    rowi = lax.broadcasted_iota(jnp.int32, (tl, GROUP), 0)
    d = 1
    while d < tl:
        keep = rowi >= d
        a_sh = jnp.where(keep, pltpu.roll(a, d, 0), 1.0)
        b_sh = jnp.where(keep, pltpu.roll(bx, d, 0), 0.0)
        bx = a * b_sh + bx
        a = a * a_sh
        d *= 2
    hseq = a * h_sc[...] + bx
    h_sc[...] = hseq[tl - 1:tl, :]
    out_ref[...] = (hseq * _silu(z_ref[...].astype(F32))).astype(out_ref.dtype)


def _lru(gc, conv_w, conv_b, gate_w, gate_b, lam, batch, seq):
    tl = 256
    nt = seq // tl
    row = lambda col: (lambda b, j: (b * nt + j, col))
    full = lambda a: pl.BlockSpec(a.shape, lambda b, j: (0,) * a.ndim)
    return pl.pallas_call(
        functools.partial(_lru_kernel, tl=tl),
        out_shape=jax.ShapeDtypeStruct((batch * seq, GROUP), BF16),
        grid=(batch, nt),
        in_specs=[pl.BlockSpec((tl, GROUP), row(3)), pl.BlockSpec((tl, GROUP), row(4)),
                  full(conv_w), full(conv_b), full(gate_w), full(gate_b), full(lam)],
        out_specs=pl.BlockSpec((tl, GROUP), row(0)),
        scratch_shapes=[pltpu.VMEM((tl + 8, GROUP), F32), pltpu.VMEM((1, GROUP), F32)],
        compiler_params=_cparams(("parallel", "arbitrary")),
        name="rglru",
    )(gc, gc, conv_w, conv_b, gate_w, gate_b, lam)


def _moba_tile(i, q_ref, k_ref, z_ref, o_ref, kmean_sc, vt_sc, *, seq):
    tq = MOBA_BLOCK
    nb = seq // MOBA_BLOCK
    ncand = 8
    nk = (i + 1) * tq
    past = lax.broadcasted_iota(jnp.int32, (ncand, tq), 0) < i
    causal = (lax.broadcasted_iota(jnp.int32, (tq, tq), 0) <= lax.broadcasted_iota(jnp.int32, (tq, tq), 1))
    for h in range(NH):
        cs = slice(h * DH, (h + 1) * DH)
        qh = q_ref[:, cs]
        s = _dot_nt(k_ref[0:nk, cs], qh)
        parts = [jnp.where(causal, s[i * tq:nk], NEG)]
        if i > 0:
            gate = _dot_nt(kmean_sc[h].astype(BF16), qh)[0:ncand]
            score = jnp.where(past, gate, NEG)
            sel = jnp.logical_and(_topk_rows(score, min(MOBA_TOPK, nb)), past)
            bias = jnp.where(sel, 0.0, NEG)
            parts = [s[j * tq:(j + 1) * tq] + bias[j:j + 1] for j in range(i)] + parts
        p, inv = _softmax_rows(jnp.concatenate(parts, axis=0))
        o = jnp.transpose(_dot(vt_sc[h, :, 0:nk], p) * inv)
        o_ref[:, cs] = (o * _silu(z_ref[:, cs].astype(F32))).astype(o_ref.dtype)


def _moba_kernel(q_ref, k_ref, v_ref, z_ref, o_ref, kmean_sc, vt_sc, *, seq):
    i = pl.program_id(1)
    tq = MOBA_BLOCK
    nb = seq // MOBA_BLOCK

    @pl.when(i == 0)
    def _():
        kmean_sc[...] = jnp.zeros(kmean_sc.shape, F32)
        for h in range(NH):
            cs = slice(h * DH, (h + 1) * DH)
            kmean_sc[h, 0:nb, :] = jnp.mean(k_ref[:, cs].astype(F32).reshape(nb, MOBA_BLOCK, DH), axis=1)
            for c in range(nb):
                rows = slice(c * tq, (c + 1) * tq)
                vt_sc[h, :, rows] = jnp.transpose(v_ref[rows, cs].astype(F32)).astype(BF16)

    for c in range(nb):
        pl.when(i == c)(functools.partial(_moba_tile, c, q_ref, k_ref, z_ref, o_ref, kmean_sc, vt_sc, seq=seq))


def _moba(ga, gb, gc, batch, seq):
    tq = MOBA_BLOCK
    nq = seq // tq
    assert nq <= 8
    row = lambda col: (lambda b, i: (b * nq + i, col))
    return pl.pallas_call(
        functools.partial(_moba_kernel, seq=seq),
        out_shape=jax.ShapeDtypeStruct((batch * seq, GROUP), BF16),
        grid=(batch, nq),
        in_specs=[pl.BlockSpec((tq, GROUP), row(2)),
                  pl.BlockSpec((seq, GROUP), lambda b, i: (b, 3)),
                  pl.BlockSpec((seq, GROUP), lambda b, i: (b, 4)),
                  pl.BlockSpec((tq, GROUP), row(5))],
        out_specs=pl.BlockSpec((tq, GROUP), row(0)),
        scratch_shapes=[pltpu.VMEM((NH, LANES, DH), F32), pltpu.VMEM((NH, DH, seq), BF16)],
        compiler_params=_cparams(("parallel", "arbitrary")),
        name="moba",
    )(ga, ga, gb, gc)


def _out_kernel(ya_ref, yb_ref, yc_ref, yd_ref, w_ref, x_ref, g_ref, b_ref, o_ref, ob_ref):
    acc = DEEPNORM_ALPHA * x_ref[...]
    for p, y_ref in enumerate((ya_ref, yb_ref, yc_ref, yd_ref)):
        acc = acc + _dot(y_ref[...], w_ref[p * GROUP:(p + 1) * GROUP, :])
    mu = jnp.mean(acc, axis=-1, keepdims=True)
    var = jnp.mean(jnp.square(acc - mu), axis=-1, keepdims=True)
    y = (acc - mu) * lax.rsqrt(var + LN_EPS) * g_ref[...] + b_ref[...]
    o_ref[...] = y
    ob_ref[...] = y.astype(BF16)


def _out_proj(ys, w_out, x, ln_g, ln_b):
    m, d = x.shape
    tm = 512
    yspec = pl.BlockSpec((tm, GROUP), lambda i: (i, 0))
    full = lambda a: pl.BlockSpec(a.shape, lambda i: (0,) * a.ndim)
    xspec = pl.BlockSpec((tm, d), lambda i: (i, 0))
    return pl.pallas_call(
        _out_kernel,
        out_shape=(jax.ShapeDtypeStruct((m, d), F32), jax.ShapeDtypeStruct((m, d), BF16)),
        grid=(m // tm,),
        in_specs=[yspec] * 4 + [full(w_out), xspec, full(ln_g), full(ln_b)],
        out_specs=(xspec, xspec),
        compiler_params=_cparams(("parallel",)),
        name="out_proj_ln",
    )(*ys, w_out, x, ln_g, ln_b)


_GROUPS = (
    (('nsa_q', 0, 512), ('nsa_kv', 0, 384), (None, 0, 128), ('moba_qkv', 0, 1024)),
    (('nsa_kv', 384, 384), (None, 0, 128), ('mlstm_qkv', 0, 1536), ('moba_qkv', 1024, 512)),
    (('nsa_z', 0, 512), ('mlstm_o', 0, 512), ('mlstm_z', 0, 512), ('lru_x', 0, 512), ('lru_z', 0, 512),
     ('moba_z', 0, 512)),
)
_GATE_PIECES = (('nsa_gate', 0, 12), (None, 0, 4), ('mlstm_if', 0, 8), (None, 0, LANES - 24))


def _regroup_kernel(w_ref, *out_refs):
    for pieces, o_ref in zip(_GROUPS + (_GATE_PIECES,), out_refs):
        dst = 0
        for name, off, width in pieces:
            if name is None:
                o_ref[dst:dst + width, :] = jnp.zeros((width, o_ref.shape[1]), o_ref.dtype)
            else:
                src = _OFF[name] + off
                o_ref[dst:dst + width, :] = w_ref[src:src + width, :].astype(o_ref.dtype)
            dst += width


def _regroup_w_in(w_in):
    depth, d, n_in = w_in.shape
    wt = jnp.transpose(w_in, (0, 2, 1))
    tk = 256
    widths = [sum(p[2] for p in pieces) for pieces in _GROUPS] + [LANES]
    dtypes = [BF16] * len(_GROUPS) + [F32]
    groups = pl.pallas_call(
        _regroup_kernel,
        out_shape=tuple(jax.ShapeDtypeStruct((depth, n, d), dt) for n, dt in zip(widths, dtypes)),
        grid=(depth, d // tk),
        in_specs=[pl.BlockSpec((None, n_in, tk), lambda l, i: (l, 0, i))],
        out_specs=tuple(pl.BlockSpec((None, n, tk), lambda l, i: (l, 0, i)) for n in widths),
        compiler_params=_cparams(("parallel", "parallel")),
        name="regroup_w_in",
    )(wt)
    return list(groups[:-1]) + [groups[-1].astype(BF16)]


def _rope_tables(seq):
    half = ROT_DIM // 2
    inv_freq = jnp.power(ROPE_THETA, -jnp.arange(half, dtype=F32) * (2.0 / ROT_DIM))
    ang = jnp.arange(seq, dtype=jnp.int32).astype(F32)[:, None] * inv_freq[None, :]
    cos, sin = jnp.cos(ang), jnp.sin(ang)
    ones = jnp.ones((seq, DH - ROT_DIM), F32)
    return (jnp.concatenate([cos, cos, ones], axis=1),
            jnp.concatenate([-sin, sin, 0.0 * ones], axis=1))


def _cmp_overlap_t(seq):
    n_cmp = seq // CMP_STRIDE - 1
    n_sel = seq // SEL_BLOCK
    cs = jnp.arange(LANES)[None, :] * CMP_STRIDE
    ss = jnp.arange(LANES)[:, None] * SEL_BLOCK
    ov = (cs < ss + SEL_BLOCK) & (cs + CMP_BLOCK > ss)
    ov = ov & (jnp.arange(LANES)[None, :] < n_cmp) & (jnp.arange(LANES)[:, None] < n_sel)
    return ov.astype(BF16)


def _chunk_tri(tc):
    r = jnp.arange(tc)
    same = (r[:, None] // MLSTM_CHUNK) == (r[None, :] // MLSTM_CHUNK)
    tril = (same & (r[None, :] <= r[:, None])).astype(BF16)
    return tril, tril.T


def _layer(x, xb, batch, seq, consts, w_groups, cmp_w1, cmp_w2, cmp_pe, i_bias, f_bias, norm_g,
           conv_w, conv_b, gate_w, gate_b, lam, w_out, ln_g, ln_b):
    rope, ov_t, tril, triu = consts
    wa, wb, wc, wd = w_groups
    tm = min(2048, seq)
    ga = _project(xb, wa, BF16, tm, 512, rope=rope, seq=seq)
    gb = _project(xb, wb, BF16, tm, 512)
    gc = _project(xb, wc, BF16, tm, 512)
    gd = _project(xb, wd, F32, tm, LANES)
    gd_t = _project_t(xb, wd, F32, tm)

    nb = seq // CMP_STRIDE
    tk = ga[:, 4 * DH:5 * DH].reshape(batch, nb, CMP_STRIDE * DH)
    tv = gb[:, 0:DH].reshape(batch, nb, CMP_STRIDE * DH)
    pe = cmp_pe.reshape(2, 2, CMP_STRIDE * DH)
    kc, vc = _compress(tk, tv, cmp_w1.astype(BF16), cmp_w2.astype(BF16), pe)
    y_a = _nsa(ga, gb, gc, gd_t, kc, vc, ov_t, batch, seq)

    bias = jnp.concatenate([i_bias, f_bias])
    bias_col = jnp.zeros((1, LANES), F32).at[0, GD_I:GD_I + 2 * NH].set(bias)
    bias_row = bias[:, None]
    y_b = _mlstm(gb, gc, gd, gd_t, bias_col, bias_row, norm_g[None, :], tril, triu, batch, seq)

    y_c = _lru(gc, conv_w, conv_b[None, :], gate_w.astype(BF16), gate_b, lam[None, :], batch, seq)

    y_d = _moba(ga, gb, gc, batch, seq)

    return _out_proj((y_a, y_b, y_c, y_d), w_out.astype(BF16), x, ln_g[None, :], ln_b[None, :])


def kernel(x, w_in, nsa_cmp_w1, nsa_cmp_w2, nsa_cmp_pe, mlstm_i_bias, mlstm_f_bias, mlstm_norm_g,
           lru_conv_w, lru_conv_b, lru_gate_w, lru_gate_b, lru_lambda, w_out, ln_g, ln_b):
    batch, seq, d = x.shape
    tril, triu = _chunk_tri(MLSTM_CHUNK)
    consts = (_rope_tables(seq), _cmp_overlap_t(seq), tril, triu)
    xf = x.reshape(batch * seq, d)
    xb = xf.astype(BF16)
    w_groups = _regroup_w_in(w_in)
    for l in range(w_in.shape[0]):
        xf, xb = _layer(xf, xb, batch, seq, consts, [w[l] for w in w_groups], nsa_cmp_w1[l], nsa_cmp_w2[l],
                        nsa_cmp_pe[l], mlstm_i_bias[l], mlstm_f_bias[l], mlstm_norm_g[l], lru_conv_w[l],
                        lru_conv_b[l], lru_gate_w[l], lru_gate_b[l], lru_lambda[l], w_out[l], ln_g[l], ln_b[l])
    return xf.reshape(batch, seq, d)
```

```python
import functools
import math

import jax
import jax.numpy as jnp
from jax import lax
from jax.experimental import pallas as pl
from jax.experimental.pallas import tpu as pltpu

F32 = jnp.float32
BF16 = jnp.bfloat16

D_MODEL = 2048
DEPTH = 2
GROUP = 512
DH = 128
NH = 4
ROT_DIM = 32
ROPE_THETA = 500000.0

CMP_BLOCK = 32
CMP_STRIDE = 16
SEL_BLOCK = 64
SEL_TOPK = 8
WIN = 256
FORCE_SCORE = 1e9

MLSTM_CHUNK = 512
LRU_C = 8.0
MOBA_BLOCK = 256
MOBA_TOPK = 3

DEEPNORM_ALPHA = (2 * DEPTH) ** 0.25
NEG = -1e30
LN_EPS = 1e-5
SCALE = DH ** -0.5
EXP2_SCALE = SCALE * math.log2(math.e)

LANES = 128
VMEM_LIMIT = 56 * 1024 * 1024

_OFF = {}
_o = 0
for _name, _w in (('nsa_q', 512), ('nsa_kv', 768), ('nsa_gate', 12), ('nsa_z', 512), ('mlstm_qkv', 1536),
                  ('mlstm_if', 8), ('mlstm_o', 512), ('mlstm_z', 512), ('lru_x', 512), ('lru_z', 512),
                  ('moba_qkv', 1536), ('moba_z', 512)):
    _OFF[_name] = _o
    _o += _w

GD_I = 16
GD_F = 20
GC0 = 5

NT = (((1,), (1,)), ((), ()))
TN = (((0,), (0,)), ((), ()))


def _cparams(sem):
    return pltpu.CompilerParams(dimension_semantics=sem, vmem_limit_bytes=VMEM_LIMIT)


def _dot(a, b):
    return jnp.dot(a, b, preferred_element_type=F32)


def _dot_nt(a, b):
    return lax.dot_general(a, b, NT, preferred_element_type=F32)


def _split_hi_lo(a):
    hi = a.astype(BF16)
    lo = (a - hi.astype(F32)).astype(BF16)
    return hi, lo


def _silu(x):
    return x * jax.nn.sigmoid(x)


def _log_sigmoid(x):
    return jnp.minimum(x, 0.0) - jnp.log(1.0 + jnp.exp(-jnp.abs(x)))


def _proj_kernel(x_ref, w_ref, wg_ref, cos_ref, sin_ref, swap_ref, ga_ref, gbc_ref, gd_ref, gdt_ref, *, n_rot,
                 n_tiles):
    j = pl.program_id(1)
    x = x_ref[...]
    acc = _dot_nt(x, w_ref[...])
    gbc_ref[...] = acc.astype(gbc_ref.dtype)

    @pl.when(j < n_rot)
    def _():
        c = cos_ref[...]
        s = sin_ref[...]
        for h in range(acc.shape[1] // DH):
            t = acc[:, h * DH:(h + 1) * DH]
            swapped = _dot(t.astype(BF16), swap_ref[...])
            ga_ref[:, h * DH:(h + 1) * DH] = (t * c + swapped * s).astype(ga_ref.dtype)

    @pl.when(j == n_tiles - 1)
    def _():
        gd = _dot_nt(x, wg_ref[...])
        gd_ref[...] = gd
        gdt_ref[...] = jnp.transpose(gd)


def _project(xb, w_all, w_gate, rope, seq, tm, tn):
    m, k = xb.shape
    widths = [sum(p[2] for p in pieces) for pieces in _GROUPS]
    assert tn == GROUP and widths[1] == GC0 * GROUP
    n_rot = widths[0] // tn
    n_tiles = sum(widths) // tn
    nrep = seq // tm
    rope_spec = pl.BlockSpec((tm, DH), lambda i, j: (i % nrep, 0))
    cos, sin, swap = rope
    return pl.pallas_call(
        functools.partial(_proj_kernel, n_rot=n_rot, n_tiles=n_tiles),
        out_shape=(jax.ShapeDtypeStruct((m, widths[0]), BF16), jax.ShapeDtypeStruct((m, widths[1] + widths[2]), BF16),
                   jax.ShapeDtypeStruct((m, LANES), F32), jax.ShapeDtypeStruct((LANES, m), F32)),
        grid=(m // tm, n_tiles),
        in_specs=[pl.BlockSpec((tm, k), lambda i, j: (i, 0)), pl.BlockSpec((tn, k), lambda i, j: (j, 0)),
                  pl.BlockSpec((LANES, k), lambda i, j: (0, 0)), rope_spec, rope_spec,
                  pl.BlockSpec((DH, DH), lambda i, j: (0, 0))],
        out_specs=(pl.BlockSpec((tm, tn), lambda i, j: (i, jnp.minimum(j, n_rot - 1))),
                   pl.BlockSpec((tm, tn), lambda i, j: (i, jnp.maximum(j - n_rot, 0))),
                   pl.BlockSpec((tm, LANES), lambda i, j: (i, 0)), pl.BlockSpec((LANES, tm), lambda i, j: (0, i))),
        compiler_params=_cparams(("parallel", "arbitrary")),
        name="in_proj",
    )(xb, w_all, w_gate, cos, sin, swap)


def _compress_kernel(tk_ref, tv_ref, w1_ref, w2_ref, pe_ref, kc_ref, vc_ref, *, nb):
    half = (CMP_BLOCK // 2) * DH
    for idx, (t_ref, o_ref) in enumerate(((tk_ref, kc_ref), (tv_ref, vc_ref))):
        t = t_ref[...].astype(F32)
        lo = (t + pe_ref[idx, 0:1, :]).astype(BF16)
        hi = (t + pe_ref[idx, 1:2, :]).astype(BF16)
        a = _dot(lo, w1_ref[idx, 0:half, :])
        b = _dot(hi, w1_ref[idx, half:2 * half, :])
        hid = _silu(a + pltpu.roll(b, nb - 1, 0))
        out = _dot(hid.astype(BF16), w2_ref[idx])
        o_ref[...] = jnp.zeros(o_ref.shape, o_ref.dtype)
        o_ref[0:nb, :] = out.astype(o_ref.dtype)


def _compress(tk, tv, w1, w2, pe):
    b, nb, width = tk.shape
    blk = pl.BlockSpec((None, nb, width), lambda i: (i, 0, 0))
    full = lambda a: pl.BlockSpec(a.shape, lambda i: (0,) * a.ndim)
    out = jax.ShapeDtypeStruct((b, LANES, DH), BF16)
    return pl.pallas_call(
        functools.partial(_compress_kernel, nb=nb),
        out_shape=(out, out),
        grid=(b,),
        in_specs=[blk, blk, full(w1), full(w2), full(pe)],
        out_specs=(pl.BlockSpec((None, LANES, DH), lambda i: (i, 0, 0)),) * 2,
        compiler_params=_cparams(("parallel",)),
        name="nsa_compress",
    )(tk, tv, w1, w2, pe)


def _topk_rows(score, k_top):
    rowi = lax.broadcasted_iota(jnp.int32, score.shape, 0)
    rank = jnp.zeros(score.shape, F32)
    for k in range(score.shape[0]):
        sk = score[k:k + 1, :]
        beats = jnp.logical_or(sk > score, jnp.logical_and(sk == score, rowi > k))
        rank = rank + jnp.where(beats, 1.0, 0.0)
    return rank < k_top


def _softmax_rows(s):
    p = jnp.exp2((s - jnp.max(s, axis=0, keepdims=True)) * EXP2_SCALE)
    return p.astype(BF16), 1.0 / jnp.sum(p, axis=0, keepdims=True)


def _nsa_tile(i, q_ref, kc_ref, vc_ref, ks_ref, kw_ref, gt_ref, z_ref, ovt_ref, diagb_ref, winb_ref,
              o_ref, vst_sc, vwt_sc, *, seq, tq):
    width = NH * tq
    n_cmp = seq // CMP_STRIDE - 1
    n_sel = seq // SEL_BLOCK
    n_rows = 32
    q = q_ref[...]
    qs = jnp.concatenate([q[:, h * DH:(h + 1) * DH] for h in range(NH)], axis=0)

    rown = lax.broadcasted_iota(jnp.int32, (LANES, width), 0)
    tok = i * tq + (lax.broadcasted_iota(jnp.int32, (LANES, width), 1) & (tq - 1))
    mask_c = jnp.logical_and(rown * CMP_STRIDE + (CMP_BLOCK - 1) <= tok, rown < n_cmp)
    s = jnp.where(mask_c, _dot_nt(kc_ref[...], qs), NEG)
    e = jnp.exp2((s - jnp.max(s, axis=0, keepdims=True)) * EXP2_SCALE)
    p_c = jnp.where(mask_c, e, 0.0) * (1.0 / jnp.sum(e, axis=0, keepdims=True))
    o_c = lax.dot_general(vc_ref[...], p_c.astype(BF16), TN, preferred_element_type=F32)

    p_sum = p_c[:, 0:tq]
    for h in range(1, NH):
        p_sum = p_sum + p_c[:, h * tq:(h + 1) * tq]
    p_hi, p_lo = _split_hi_lo(p_sum)
    imp = (_dot(ovt_ref[...], p_hi) + _dot(ovt_ref[...], p_lo))[0:n_rows]
    rowj = lax.broadcasted_iota(jnp.int32, (n_rows, tq), 0)
    t_q = i * tq + lax.broadcasted_iota(jnp.int32, (n_rows, tq), 1)
    cur = t_q >> 6
    forced = jnp.logical_or(rowj == 0, jnp.logical_or(rowj == cur, rowj == cur - 1))
    valid = rowj * SEL_BLOCK <= t_q
    score = jnp.where(forced, FORCE_SCORE, jnp.where(valid, imp, NEG))
    score = jnp.where(rowj < n_sel, score, -jnp.inf)
    bias = jnp.where(_topk_rows(score, min(SEL_TOPK, n_sel)), 0.0, NEG)
    bias = jnp.concatenate([bias] * NH, axis=1)

    nk = (i + 1) * tq
    s = _dot_nt(ks_ref[0:nk, :], qs)
    parts = [s[j * SEL_BLOCK:(j + 1) * SEL_BLOCK] + bias[j:j + 1] for j in range(nk // SEL_BLOCK)]
    own = jnp.concatenate(parts[i * tq // SEL_BLOCK:], axis=0) + diagb_ref[...]
    p, inv = _softmax_rows(jnp.concatenate(parts[:i * tq // SEL_BLOCK] + [own], axis=0))
    o_s = _dot(vst_sc[:, 0:nk], p) * inv

    n_wblk = winb_ref.shape[0]
    blocks = [b for b in range(i - n_wblk + 1, i + 1) if b >= 0]
    parts = [_dot_nt(kw_ref[b * tq:(b + 1) * tq, :], qs) + winb_ref[b - i + n_wblk - 1] for b in blocks]
    p, inv = _softmax_rows(jnp.concatenate(parts, axis=0))
    o_w = _dot(vwt_sc[:, blocks[0] * tq:nk], p) * inv

    g = jax.nn.sigmoid(gt_ref[0:16, :])
    for h in range(NH):
        ls = slice(h * tq, (h + 1) * tq)
        cs = slice(h * DH, (h + 1) * DH)
        mix = (g[3 * h:3 * h + 1] * o_c[:, ls] + g[3 * h + 1:3 * h + 2] * o_s[:, ls]
               + g[3 * h + 2:3 * h + 3] * o_w[:, ls])
        o_ref[:, cs] = (jnp.transpose(mix) * _silu(z_ref[:, cs].astype(F32))).astype(o_ref.dtype)


def _nsa_kernel(q_ref, kc_ref, vc_ref, ks_ref, vs_ref, kw_ref, vw_ref, gt_ref, z_ref, ovt_ref, diagb_ref,
                winb_ref, o_ref, vst_sc, vwt_sc, *, seq, tq):
    i = pl.program_id(1)

    @pl.when(i == 0)
    def _():
        for c in range(seq // tq):
            cols = slice(c * tq, (c + 1) * tq)
            vst_sc[:, cols] = jnp.transpose(vs_ref[cols, :].astype(F32)).astype(BF16)
            vwt_sc[:, cols] = jnp.transpose(vw_ref[cols, :].astype(F32)).astype(BF16)

    for c in range(seq // tq):
        pl.when(i == c)(functools.partial(
            _nsa_tile, c, q_ref, kc_ref, vc_ref, ks_ref, kw_ref, gt_ref, z_ref, ovt_ref, diagb_ref, winb_ref,
            o_ref, vst_sc, vwt_sc, seq=seq, tq=tq))


def _nsa_bias_tables(tq):
    t = jnp.arange(NH * tq)[None, :] % tq
    r = jnp.arange(tq)[:, None]
    diag = jnp.where(r <= t, 0.0, NEG)
    n_wblk = WIN // tq + 1
    win = []
    for b in range(n_wblk):
        diff = t - (r + (b - (n_wblk - 1)) * tq)
        win.append(jnp.where((diff >= 0) & (diff < WIN), 0.0, NEG))
    return diag.astype(F32), jnp.stack(win).astype(F32)


def _nsa(ga, gb, gc, gd_t, kc, vc, ov_t, batch, seq):
    tq = 256
    nq = seq // tq
    assert seq // SEL_BLOCK <= 32 and tq % SEL_BLOCK == 0
    row = lambda b, i: (b * nq + i, 0)
    kv = lambda col: pl.BlockSpec((seq, DH), lambda b, i: (b, col))
    cmp_spec = pl.BlockSpec((None, LANES, DH), lambda b, i: (b, 0, 0))
    full = lambda a: pl.BlockSpec(a.shape, lambda b, i: (0,) * a.ndim)
    diag_b, win_b = _nsa_bias_tables(tq)
    return pl.pallas_call(
        functools.partial(_nsa_kernel, seq=seq, tq=tq),
        out_shape=jax.ShapeDtypeStruct((batch * seq, GROUP), BF16),
        grid=(batch, nq),
        in_specs=[pl.BlockSpec((tq, GROUP), row), cmp_spec, cmp_spec,
                  kv(5), kv(1), kv(6), kv(2),
                  pl.BlockSpec((LANES, tq), lambda b, i: (0, b * nq + i)),
                  pl.BlockSpec((tq, GROUP), lambda b, i: (b * nq + i, GC0)), full(ov_t), full(diag_b), full(win_b)],
        out_specs=pl.BlockSpec((tq, GROUP), row),
        scratch_shapes=[pltpu.VMEM((DH, seq), BF16), pltpu.VMEM((DH, seq), BF16)],
        compiler_params=_cparams(("parallel", "arbitrary")),
        name="nsa_attention",
    )(ga, kc, vc, ga, gb, ga, gb, gd_t, gc, ov_t, diag_b, win_b)


def _mlstm_kernel(q_ref, k_ref, v_ref, gcol_ref, grow_ref, bcol_ref, brow_ref, og_ref, z_ref, ng_ref,
                  tril_ref, triu_ref, out_ref, c_sc, n_sc, m_sc, *, tc):
    L = MLSTM_CHUNK

    @pl.when(pl.program_id(1) == 0)
    def _():
        c_sc[...] = jnp.zeros(c_sc.shape, F32)
        n_sc[...] = jnp.zeros(n_sc.shape, F32)
        m_sc[...] = jnp.full(m_sc.shape, NEG, F32)

    gcol = gcol_ref[...] + bcol_ref[...]
    grow = grow_ref[...] + brow_ref[...]
    hi, lo = _split_hi_lo(_log_sigmoid(gcol))
    bcol_all = _dot(tril_ref[...], hi) + _dot(tril_ref[...], lo)
    hi, lo = _split_hi_lo(_log_sigmoid(grow))
    brow_all = _dot(hi, triu_ref[...]) + _dot(lo, triu_ref[...])

    causal = (lax.broadcasted_iota(jnp.int32, (L, L), 1) <= lax.broadcasted_iota(jnp.int32, (L, L), 0))

    for c in range(tc // L):
        rs = slice(c * L, (c + 1) * L)
        for h in range(NH):
            cs = slice(h * DH, (h + 1) * DH)
            qh = q_ref[rs, cs]
            kh = k_ref[rs, cs]
            vh = v_ref[rs, cs]
            ig_col = gcol[rs, GD_I + h:GD_I + h + 1]
            b_col = bcol_all[rs, GD_F + h:GD_F + h + 1]
            ig_row = grow[h:h + 1, rs]
            b_row = brow_all[NH + h:NH + h + 1, rs]
            c_prev = c_sc[h]
            n_prev = n_sc[h]
            m_prev = m_sc[h][:, 0:1]

            d_log = jnp.where(causal, b_col - b_row + ig_row, NEG)
            m_intra = jnp.max(d_log, axis=-1, keepdims=True)
            m_inter = b_col + m_prev
            m_t = jnp.maximum(m_inter, m_intra)
            w_inter = jnp.exp(m_inter - m_t)
            qk = _dot_nt(qh, kh) * SCALE * jnp.exp(d_log - m_t)
            num = _dot(qk.astype(BF16), vh) + w_inter * _dot(qh, c_prev.astype(BF16))
            den = (jnp.sum(qk, axis=-1, keepdims=True)
                   + w_inter * jnp.sum(qh.astype(F32) * n_prev, axis=-1, keepdims=True))
            hh = num / jnp.maximum(jnp.abs(den), jnp.exp(-m_t))
            hh = jax.nn.sigmoid(og_ref[rs, cs].astype(F32)) * hh
            mu = jnp.mean(hh, axis=-1, keepdims=True)
            var = jnp.mean(jnp.square(hh - mu), axis=-1, keepdims=True)
            hn = (hh - mu) * lax.rsqrt(var + LN_EPS) * ng_ref[:, cs]
            out_ref[rs, cs] = (hn * _silu(z_ref[rs, cs].astype(F32))).astype(out_ref.dtype)

            b_last = b_row[:, L - 1:L]
            m_loc = jnp.max(b_last - b_row + ig_row, axis=-1, keepdims=True)
            e_col = jnp.exp(b_last - b_col + ig_col - m_loc)
            ek = (e_col * SCALE) * kh.astype(F32)
            g_c = lax.dot_general(ek.astype(BF16), vh, TN, preferred_element_type=F32)
            g_n = jnp.sum(ek, axis=0, keepdims=True)
            m_new = jnp.maximum(b_last + m_prev, m_loc)
            s_old = jnp.exp(b_last + m_prev - m_new)
            s_new = jnp.exp(m_loc - m_new)
            c_sc[h] = s_old * c_prev + s_new * g_c
            n_sc[h] = s_old * n_prev + s_new * g_n
            m_sc[h] = jnp.broadcast_to(m_new, (1, LANES))


def _mlstm(gb, gc, gd, gd_t, bias_col, bias_row, norm_g, tril, triu, batch, seq):
    tc = MLSTM_CHUNK
    nt = seq // tc
    row = lambda col: (lambda b, j: (b * nt + j, col))
    full = lambda a: pl.BlockSpec(a.shape, lambda b, j: (0,) * a.ndim)
    return pl.pallas_call(
        functools.partial(_mlstm_kernel, tc=tc),
        out_shape=jax.ShapeDtypeStruct((batch * seq, GROUP), BF16),
        grid=(batch, nt),
        in_specs=[pl.BlockSpec((tc, GROUP), row(1)), pl.BlockSpec((tc, GROUP), row(2)),
                  pl.BlockSpec((tc, GROUP), row(3)),
                  pl.BlockSpec((tc, LANES), row(0)),
                  pl.BlockSpec((2 * NH, tc), lambda b, j: (GD_I // (2 * NH), b * nt + j)),
                  full(bias_col), full(bias_row),
                  pl.BlockSpec((tc, GROUP), row(GC0 + 1)), pl.BlockSpec((tc, GROUP), row(GC0 + 2)),
                  full(norm_g), full(tril), full(triu)],
        out_specs=pl.BlockSpec((tc, GROUP), row(0)),
        scratch_shapes=[pltpu.VMEM((NH, DH, DH), F32), pltpu.VMEM((NH, 1, DH), F32),
                        pltpu.VMEM((NH, 1, LANES), F32)],
        compiler_params=_cparams(("parallel", "arbitrary")),
        name="mlstm",
    )(gb, gb, gb, gd, gd_t, bias_col, bias_row, gc, gc, norm_g, tril, triu)


def _lru_kernel(x_ref, z_ref, cw_ref, cb_ref, gw_ref, gb_ref, lam_ref, out_ref, xbuf, h_sc, *, tl):
    pad = 8

    @pl.when(pl.program_id(1) == 0)
    def _():
        xbuf[0:pad, :] = jnp.zeros((pad, GROUP), F32)
        h_sc[...] = jnp.zeros(h_sc.shape, F32)

    x = x_ref[...].astype(F32)
    xbuf[pad:pad + tl, :] = x
    u = cw_ref[3:4, :] * x + cb_ref[...]
    for w in range(3):
        u = u + cw_ref[w:w + 1, :] * xbuf[pad - 3 + w:pad - 3 + w + tl, :]
    xbuf[0:pad, :] = x[tl - pad:tl, :]

    ub = u.astype(BF16)
    pre = []
    for gi in range(2):
        pre.append(jnp.concatenate(
            [_dot(ub[:, n * DH:(n + 1) * DH], gw_ref[gi, n]) for n in range(NH)], axis=1) + gb_ref[gi:gi + 1, :])
    r = jax.nn.sigmoid(pre[0])
    ig = jax.nn.sigmoid(pre[1])
    neg_lam = -lam_ref[...]
    softplus = jnp.maximum(neg_lam, 0.0) + jnp.log(1.0 + jnp.exp(-jnp.abs(neg_lam)))
    a = jnp.exp(-LRU_C * r * softplus)
    bx = jnp.sqrt(1.0 - a * a) * (ig * u)

    rowi = lax.broadcasted_iota(jnp.int32, (tl, GROUP), 0)
    d = 1
    while d < tl:
        keep = rowi >= d
        a_sh = jnp.where(keep, pltpu.roll(a, d, 0), 1.0)
        b_sh = jnp.where(keep, pltpu.roll(bx, d, 0), 0.0)
        bx = a * b_sh + bx
        a = a * a_sh
        d *= 2
    hseq = a * h_sc[...] + bx
    h_sc[...] = hseq[tl - 1:tl, :]
    out_ref[...] = (hseq * _silu(z_ref[...].astype(F32))).astype(out_ref.dtype)


def _lru(gc, conv_w, conv_b, gate_w, gate_b, lam, batch, seq):
    tl = 256
    nt = seq // tl
    row = lambda col: (lambda b, j: (b * nt + j, col))
    full = lambda a: pl.BlockSpec(a.shape, lambda b, j: (0,) * a.ndim)
    return pl.pallas_call(
        functools.partial(_lru_kernel, tl=tl),
        out_shape=jax.ShapeDtypeStruct((batch * seq, GROUP), BF16),
        grid=(batch, nt),
        in_specs=[pl.BlockSpec((tl, GROUP), row(GC0 + 3)), pl.BlockSpec((tl, GROUP), row(GC0 + 4)),
                  full(conv_w), full(conv_b), full(gate_w), full(gate_b), full(lam)],
        out_specs=pl.BlockSpec((tl, GROUP), row(0)),
        scratch_shapes=[pltpu.VMEM((tl + 8, GROUP), F32), pltpu.VMEM((1, GROUP), F32)],
        compiler_params=_cparams(("parallel", "arbitrary")),
        name="rglru",
    )(gc, gc, conv_w, conv_b, gate_w, gate_b, lam)


def _moba_tile(i, q_ref, k_ref, z_ref, o_ref, kmean_sc, vt_sc, *, seq):
    tq = MOBA_BLOCK
    nb = seq // MOBA_BLOCK
    ncand = 8
    nk = (i + 1) * tq
    past = lax.broadcasted_iota(jnp.int32, (ncand, tq), 0) < i
    causal = (lax.broadcasted_iota(jnp.int32, (tq, tq), 0) <= lax.broadcasted_iota(jnp.int32, (tq, tq), 1))
    for h in range(NH):
        cs = slice(h * DH, (h + 1) * DH)
        qh = q_ref[:, cs]
        s = _dot_nt(k_ref[0:nk, cs], qh)
        parts = [jnp.where(causal, s[i * tq:nk], NEG)]
        if i > 0:
            gate = _dot_nt(kmean_sc[h].astype(BF16), qh)[0:ncand]
            score = jnp.where(past, gate, NEG)
            sel = jnp.logical_and(_topk_rows(score, min(MOBA_TOPK, nb)), past)
            bias = jnp.where(sel, 0.0, NEG)
            parts = [s[j * tq:(j + 1) * tq] + bias[j:j + 1] for j in range(i)] + parts
        p, inv = _softmax_rows(jnp.concatenate(parts, axis=0))
        o = jnp.transpose(_dot(vt_sc[h, :, 0:nk], p) * inv)
        o_ref[:, cs] = (o * _silu(z_ref[:, cs].astype(F32))).astype(o_ref.dtype)


def _moba_kernel(q_ref, k_ref, v_ref, z_ref, o_ref, kmean_sc, vt_sc, *, seq):
    i = pl.program_id(1)
    tq = MOBA_BLOCK
    nb = seq // MOBA_BLOCK

    @pl.when(i == 0)
    def _():
        kmean_sc[...] = jnp.zeros(kmean_sc.shape, F32)
        for h in range(NH):
            cs = slice(h * DH, (h + 1) * DH)
            kmean_sc[h, 0:nb, :] = jnp.mean(k_ref[:, cs].astype(F32).reshape(nb, MOBA_BLOCK, DH), axis=1)
            for c in range(nb):
                rows = slice(c * tq, (c + 1) * tq)
                vt_sc[h, :, rows] = jnp.transpose(v_ref[rows, cs].astype(F32)).astype(BF16)

    for c in range(nb):
        pl.when(i == c)(functools.partial(_moba_tile, c, q_ref, k_ref, z_ref, o_ref, kmean_sc, vt_sc, seq=seq))


def _moba(ga, gb, gc, batch, seq):
    tq = MOBA_BLOCK
    nq = seq // tq
    assert nq <= 8
    row = lambda col: (lambda b, i: (b * nq + i, col))
    return pl.pallas_call(
        functools.partial(_moba_kernel, seq=seq),
        out_shape=jax.ShapeDtypeStruct((batch * seq, GROUP), BF16),
        grid=(batch, nq),
        in_specs=[pl.BlockSpec((tq, GROUP), row(2)),
                  pl.BlockSpec((seq, GROUP), lambda b, i: (b, 3)),
                  pl.BlockSpec((seq, GROUP), lambda b, i: (b, 4)),
                  pl.BlockSpec((tq, GROUP), row(GC0 + 5))],
        out_specs=pl.BlockSpec((tq, GROUP), row(0)),
        scratch_shapes=[pltpu.VMEM((NH, LANES, DH), F32), pltpu.VMEM((NH, DH, seq), BF16)],
        compiler_params=_cparams(("parallel", "arbitrary")),
        name="moba",
    )(ga, ga, gb, gc)


def _out_kernel(ya_ref, yb_ref, yc_ref, yd_ref, w_ref, x_ref, g_ref, b_ref, o_ref, ob_ref):
    acc = DEEPNORM_ALPHA * x_ref[...]
    for p, y_ref in enumerate((ya_ref, yb_ref, yc_ref, yd_ref)):
        acc = acc + _dot(y_ref[...], w_ref[p * GROUP:(p + 1) * GROUP, :])
    mu = jnp.mean(acc, axis=-1, keepdims=True)
    var = jnp.mean(jnp.square(acc - mu), axis=-1, keepdims=True)
    y = (acc - mu) * lax.rsqrt(var + LN_EPS) * g_ref[...] + b_ref[...]
    o_ref[...] = y
    ob_ref[...] = y.astype(BF16)


def _out_proj(ys, w_out, x, ln_g, ln_b):
    m, d = x.shape
    tm = 512
    yspec = pl.BlockSpec((tm, GROUP), lambda i: (i, 0))
    full = lambda a: pl.BlockSpec(a.shape, lambda i: (0,) * a.ndim)
    xspec = pl.BlockSpec((tm, d), lambda i: (i, 0))
    return pl.pallas_call(
        _out_kernel,
        out_shape=(jax.ShapeDtypeStruct((m, d), F32), jax.ShapeDtypeStruct((m, d), BF16)),
        grid=(m // tm,),
        in_specs=[yspec] * 4 + [full(w_out), xspec, full(ln_g), full(ln_b)],
        out_specs=(xspec, xspec),
        compiler_params=_cparams(("parallel",)),
        name="out_proj_ln",
    )(*ys, w_out, x, ln_g, ln_b)


_GROUPS = (
    (('nsa_q', 0, 512), ('nsa_kv', 0, 384), (None, 0, 128), ('moba_qkv', 0, 1024)),
    (('nsa_kv', 384, 384), (None, 0, 128), ('mlstm_qkv', 0, 1536), ('moba_qkv', 1024, 512)),
    (('nsa_z', 0, 512), ('mlstm_o', 0, 512), ('mlstm_z', 0, 512), ('lru_x', 0, 512), ('lru_z', 0, 512),
     ('moba_z', 0, 512)),
)
_GATE_PIECES = (('nsa_gate', 0, 12), (None, 0, 4), ('mlstm_if', 0, 8), (None, 0, LANES - 24))


def _regroup_kernel(w_ref, all_ref, gate_ref):
    for pieces, o_ref in ((sum(_GROUPS, ()), all_ref), (_GATE_PIECES, gate_ref)):
        dst = 0
        for name, off, width in pieces:
            if name is None:
                o_ref[dst:dst + width, :] = jnp.zeros((width, o_ref.shape[1]), o_ref.dtype)
            else:
                src = _OFF[name] + off
                o_ref[dst:dst + width, :] = w_ref[src:src + width, :].astype(o_ref.dtype)
            dst += width


def _regroup_w_in(w_in):
    depth, d, n_in = w_in.shape
    wt = jnp.transpose(w_in, (0, 2, 1))
    tk = 256
    n_all = sum(p[2] for pieces in _GROUPS for p in pieces)
    w_all, w_gate = pl.pallas_call(
        _regroup_kernel,
        out_shape=(jax.ShapeDtypeStruct((depth, n_all, d), BF16), jax.ShapeDtypeStruct((depth, LANES, d), F32)),
        grid=(depth, d // tk),
        in_specs=[pl.BlockSpec((None, n_in, tk), lambda l, i: (l, 0, i))],
        out_specs=(pl.BlockSpec((None, n_all, tk), lambda l, i: (l, 0, i)),
                   pl.BlockSpec((None, LANES, tk), lambda l, i: (l, 0, i))),
        compiler_params=_cparams(("parallel", "parallel")),
        name="regroup_w_in",
    )(wt)
    return w_all, w_gate.astype(BF16)


def _rope_tables(seq):
    half = ROT_DIM // 2
    inv_freq = jnp.power(ROPE_THETA, -jnp.arange(half, dtype=F32) * (2.0 / ROT_DIM))
    ang = jnp.arange(seq, dtype=jnp.int32).astype(F32)[:, None] * inv_freq[None, :]
    cos, sin = jnp.cos(ang), jnp.sin(ang)
    ones = jnp.ones((seq, DH - ROT_DIM), F32)
    lane = jnp.arange(DH)
    partner = jnp.where(lane < half, lane + half, lane - half)
    swap = (lane[:, None] == partner[None, :]).astype(BF16)
    return (jnp.concatenate([cos, cos, ones], axis=1),
            jnp.concatenate([-sin, sin, 0.0 * ones], axis=1), swap)


def _cmp_overlap_t(seq):
    n_cmp = seq // CMP_STRIDE - 1
    n_sel = seq // SEL_BLOCK
    cs = jnp.arange(LANES)[None, :] * CMP_STRIDE
    ss = jnp.arange(LANES)[:, None] * SEL_BLOCK
    ov = (cs < ss + SEL_BLOCK) & (cs + CMP_BLOCK > ss)
    ov = ov & (jnp.arange(LANES)[None, :] < n_cmp) & (jnp.arange(LANES)[:, None] < n_sel)
    return ov.astype(BF16)


def _chunk_tri(tc):
    r = jnp.arange(tc)
    same = (r[:, None] // MLSTM_CHUNK) == (r[None, :] // MLSTM_CHUNK)
    tril = (same & (r[None, :] <= r[:, None])).astype(BF16)
    return tril, tril.T


def _layer(x, xb, batch, seq, consts, w_groups, cmp_w1, cmp_w2, cmp_pe, i_bias, f_bias, norm_g,
           conv_w, conv_b, gate_w, gate_b, lam, w_out, ln_g, ln_b):
    rope, ov_t, tril, triu = consts
    w_all, w_gate = w_groups
    ga, gbc, gd, gd_t = _project(xb, w_all, w_gate, rope, seq, min(2048, seq), GROUP)
    gb = gc = gbc

    nb = seq // CMP_STRIDE
    tk = ga[:, 4 * DH:5 * DH].reshape(batch, nb, CMP_STRIDE * DH)
    tv = gb[:, 0:DH].reshape(batch, nb, CMP_STRIDE * DH)
    pe = cmp_pe.reshape(2, 2, CMP_STRIDE * DH)
    kc, vc = _compress(tk, tv, cmp_w1.astype(BF16), cmp_w2.astype(BF16), pe)
    y_a = _nsa(ga, gb, gc, gd_t, kc, vc, ov_t, batch, seq)

    bias = jnp.concatenate([i_bias, f_bias])
    bias_col = jnp.zeros((1, LANES), F32).at[0, GD_I:GD_I + 2 * NH].set(bias)
    bias_row = bias[:, None]
    y_b = _mlstm(gb, gc, gd, gd_t, bias_col, bias_row, norm_g[None, :], tril, triu, batch, seq)

    y_c = _lru(gc, conv_w, conv_b[None, :], gate_w.astype(BF16), gate_b, lam[None, :], batch, seq)

    y_d = _moba(ga, gb, gc, batch, seq)

    return _out_proj((y_a, y_b, y_c, y_d), w_out.astype(BF16), x, ln_g[None, :], ln_b[None, :])


def kernel(x, w_in, nsa_cmp_w1, nsa_cmp_w2, nsa_cmp_pe, mlstm_i_bias, mlstm_f_bias, mlstm_norm_g,
           lru_conv_w, lru_conv_b, lru_gate_w, lru_gate_b, lru_lambda, w_out, ln_g, ln_b):
    batch, seq, d = x.shape
    tril, triu = _chunk_tri(MLSTM_CHUNK)
    consts = (_rope_tables(seq), _cmp_overlap_t(seq), tril, triu)
    xf = x.reshape(batch * seq, d)
    xb = xf.astype(BF16)
    w_groups = _regroup_w_in(w_in)
    for l in range(w_in.shape[0]):
        xf, xb = _layer(xf, xb, batch, seq, consts, [w[l] for w in w_groups], nsa_cmp_w1[l], nsa_cmp_w2[l],
                        nsa_cmp_pe[l], mlstm_i_bias[l], mlstm_f_bias[l], mlstm_norm_g[l], lru_conv_w[l],
                        lru_conv_b[l], lru_gate_w[l], lru_gate_b[l], lru_lambda[l], w_out[l], ln_g[l], ln_b[l])
    return xf.reshape(batch, seq, d)
```

```python
import functools
import math

import jax
import jax.numpy as jnp
from jax import lax
from jax.experimental import pallas as pl
from jax.experimental.pallas import tpu as pltpu

F32 = jnp.float32
BF16 = jnp.bfloat16

D_MODEL = 2048
DEPTH = 2
GROUP = 512
DH = 128
NH = 4
ROT_DIM = 32
ROPE_THETA = 500000.0

CMP_BLOCK = 32
CMP_STRIDE = 16
SEL_BLOCK = 64
SEL_TOPK = 8
WIN = 256
FORCE_SCORE = 1e9

MLSTM_CHUNK = 512
LRU_C = 8.0
MOBA_BLOCK = 256
MOBA_TOPK = 3

DEEPNORM_ALPHA = (2 * DEPTH) ** 0.25
NEG = -1e30
LN_EPS = 1e-5
SCALE = DH ** -0.5
EXP2_SCALE = SCALE * math.log2(math.e)

LANES = 128
VMEM_LIMIT = 56 * 1024 * 1024

_OFF = {}
_o = 0
for _name, _w in (('nsa_q', 512), ('nsa_kv', 768), ('nsa_gate', 12), ('nsa_z', 512), ('mlstm_qkv', 1536),
                  ('mlstm_if', 8), ('mlstm_o', 512), ('mlstm_z', 512), ('lru_x', 512), ('lru_z', 512),
                  ('moba_qkv', 1536), ('moba_z', 512)):
    _OFF[_name] = _o
    _o += _w

GD_I = 16
GD_F = 20
GC0 = 5

NT = (((1,), (1,)), ((), ()))
TN = (((0,), (0,)), ((), ()))


def _cparams(sem):
    return pltpu.CompilerParams(dimension_semantics=sem, vmem_limit_bytes=VMEM_LIMIT)


def _dot(a, b):
    return jnp.dot(a, b, preferred_element_type=F32)


def _dot_nt(a, b):
    return lax.dot_general(a, b, NT, preferred_element_type=F32)


def _split_hi_lo(a):
    hi = a.astype(BF16)
    lo = (a - hi.astype(F32)).astype(BF16)
    return hi, lo


def _silu(x):
    return x * jax.nn.sigmoid(x)


def _log_sigmoid(x):
    return jnp.minimum(x, 0.0) - jnp.log(1.0 + jnp.exp(-jnp.abs(x)))


def _proj_kernel(x_ref, w_ref, wg_ref, cos_ref, sin_ref, swap_ref, ga_ref, gbc_ref, gd_ref, gdt_ref, *, n_rot,
                 n_tiles):
    j = pl.program_id(1)
    x = x_ref[...]
    acc = _dot_nt(x, w_ref[...])
    gbc_ref[...] = acc.astype(gbc_ref.dtype)

    @pl.when(j < n_rot)
    def _():
        c = cos_ref[...]
        s = sin_ref[...]
        for h in range(acc.shape[1] // DH):
            t = acc[:, h * DH:(h + 1) * DH]
            swapped = _dot(t.astype(BF16), swap_ref[...])
            ga_ref[:, h * DH:(h + 1) * DH] = (t * c + swapped * s).astype(ga_ref.dtype)

    @pl.when(j == n_tiles - 1)
    def _():
        gd = _dot_nt(x, wg_ref[...])
        gd_ref[...] = gd
        gdt_ref[...] = jnp.transpose(gd)


def _project(xb, w_all, w_gate, layer, rope, seq, tm, tn):
    m, k = xb.shape
    widths = [sum(p[2] for p in pieces) for pieces in _GROUPS]
    assert tn == GROUP and widths[1] == GC0 * GROUP
    n_rot = widths[0] // tn
    n_tiles = sum(widths) // tn
    nrep = seq // tm
    rope_spec = pl.BlockSpec((tm, DH), lambda i, j: (i % nrep, 0))
    cos, sin, swap = rope
    return pl.pallas_call(
        functools.partial(_proj_kernel, n_rot=n_rot, n_tiles=n_tiles),
        out_shape=(jax.ShapeDtypeStruct((m, widths[0]), BF16), jax.ShapeDtypeStruct((m, widths[1] + widths[2]), BF16),
                   jax.ShapeDtypeStruct((m, LANES), F32), jax.ShapeDtypeStruct((LANES, m), F32)),
        grid=(m // tm, n_tiles),
        in_specs=[pl.BlockSpec((tm, k), lambda i, j: (i, 0)), pl.BlockSpec((None, tn, k), lambda i, j: (layer, j, 0)),
                  pl.BlockSpec((None, LANES, k), lambda i, j: (layer, 0, 0)), rope_spec, rope_spec,
                  pl.BlockSpec((DH, DH), lambda i, j: (0, 0))],
        out_specs=(pl.BlockSpec((tm, tn), lambda i, j: (i, jnp.minimum(j, n_rot - 1))),
                   pl.BlockSpec((tm, tn), lambda i, j: (i, jnp.maximum(j - n_rot, 0))),
                   pl.BlockSpec((tm, LANES), lambda i, j: (i, 0)), pl.BlockSpec((LANES, tm), lambda i, j: (0, i))),
        compiler_params=_cparams(("parallel", "arbitrary")),
        name="in_proj",
    )(xb, w_all, w_gate, cos, sin, swap)


def _compress_kernel(tk_ref, tv_ref, w1_ref, w2_ref, pe_ref, kc_ref, vc_ref, *, nb):
    half = (CMP_BLOCK // 2) * DH
    for idx, (t_ref, o_ref) in enumerate(((tk_ref, kc_ref), (tv_ref, vc_ref))):
        t = t_ref[...].astype(F32)
        lo = (t + pe_ref[idx, 0:1, :]).astype(BF16)
        hi = (t + pe_ref[idx, 1:2, :]).astype(BF16)
        a = _dot(lo, w1_ref[idx, 0:half, :])
        b = _dot(hi, w1_ref[idx, half:2 * half, :])
        hid = _silu(a + pltpu.roll(b, nb - 1, 0))
        out = _dot(hid.astype(BF16), w2_ref[idx])
        o_ref[...] = jnp.zeros(o_ref.shape, o_ref.dtype)
        o_ref[0:nb, :] = out.astype(o_ref.dtype)


def _compress(tk, tv, w1, w2, pe):
    b, nb, width = tk.shape
    blk = pl.BlockSpec((None, nb, width), lambda i: (i, 0, 0))
    full = lambda a: pl.BlockSpec(a.shape, lambda i: (0,) * a.ndim)
    out = jax.ShapeDtypeStruct((b, LANES, DH), BF16)
    return pl.pallas_call(
        functools.partial(_compress_kernel, nb=nb),
        out_shape=(out, out),
        grid=(b,),
        in_specs=[blk, blk, full(w1), full(w2), full(pe)],
        out_specs=(pl.BlockSpec((None, LANES, DH), lambda i: (i, 0, 0)),) * 2,
        compiler_params=_cparams(("parallel",)),
        name="nsa_compress",
    )(tk, tv, w1, w2, pe)


def _topk_rows(score, k_top):
    rowi = lax.broadcasted_iota(jnp.int32, score.shape, 0)
    rank = jnp.zeros(score.shape, F32)
    for k in range(score.shape[0]):
        sk = score[k:k + 1, :]
        beats = jnp.logical_or(sk > score, jnp.logical_and(sk == score, rowi > k))
        rank = rank + jnp.where(beats, 1.0, 0.0)
    return rank < k_top


def _softmax_rows(s):
    p = jnp.exp2((s - jnp.max(s, axis=0, keepdims=True)) * EXP2_SCALE)
    return p.astype(BF16), 1.0 / jnp.sum(p, axis=0, keepdims=True)


def _nsa_tile(i, q_ref, kc_ref, vc_ref, ks_ref, kw_ref, gt_ref, z_ref, ovt_ref, diagb_ref, winb_ref,
              o_ref, vst_sc, vwt_sc, *, seq, tq):
    width = NH * tq
    n_cmp = seq // CMP_STRIDE - 1
    n_sel = seq // SEL_BLOCK
    n_rows = 32
    q = q_ref[...]
    qs = jnp.concatenate([q[:, h * DH:(h + 1) * DH] for h in range(NH)], axis=0)

    rown = lax.broadcasted_iota(jnp.int32, (LANES, width), 0)
    tok = i * tq + (lax.broadcasted_iota(jnp.int32, (LANES, width), 1) & (tq - 1))
    mask_c = jnp.logical_and(rown * CMP_STRIDE + (CMP_BLOCK - 1) <= tok, rown < n_cmp)
    s = jnp.where(mask_c, _dot_nt(kc_ref[...], qs), NEG)
    e = jnp.exp2((s - jnp.max(s, axis=0, keepdims=True)) * EXP2_SCALE)
    p_c = jnp.where(mask_c, e, 0.0) * (1.0 / jnp.sum(e, axis=0, keepdims=True))
    o_c = lax.dot_general(vc_ref[...], p_c.astype(BF16), TN, preferred_element_type=F32)

    p_sum = p_c[:, 0:tq]
    for h in range(1, NH):
        p_sum = p_sum + p_c[:, h * tq:(h + 1) * tq]
    p_hi, p_lo = _split_hi_lo(p_sum)
    imp = (_dot(ovt_ref[...], p_hi) + _dot(ovt_ref[...], p_lo))[0:n_rows]
    rowj = lax.broadcasted_iota(jnp.int32, (n_rows, tq), 0)
    t_q = i * tq + lax.broadcasted_iota(jnp.int32, (n_rows, tq), 1)
    cur = t_q >> 6
    forced = jnp.logical_or(rowj == 0, jnp.logical_or(rowj == cur, rowj == cur - 1))
    valid = rowj * SEL_BLOCK <= t_q
    score = jnp.where(forced, FORCE_SCORE, jnp.where(valid, imp, NEG))
    score = jnp.where(rowj < n_sel, score, -jnp.inf)
    bias = jnp.where(_topk_rows(score, min(SEL_TOPK, n_sel)), 0.0, NEG)
    bias = jnp.concatenate([bias] * NH, axis=1)

    nk = (i + 1) * tq
    s = _dot_nt(ks_ref[0:nk, :], qs)
    parts = [s[j * SEL_BLOCK:(j + 1) * SEL_BLOCK] + bias[j:j + 1] for j in range(nk // SEL_BLOCK)]
    own = jnp.concatenate(parts[i * tq // SEL_BLOCK:], axis=0) + diagb_ref[...]
    p, inv = _softmax_rows(jnp.concatenate(parts[:i * tq // SEL_BLOCK] + [own], axis=0))
    o_s = _dot(vst_sc[:, 0:nk], p) * inv

    n_wblk = winb_ref.shape[0]
    blocks = [b for b in range(i - n_wblk + 1, i + 1) if b >= 0]
    parts = [_dot_nt(kw_ref[b * tq:(b + 1) * tq, :], qs) + winb_ref[b - i + n_wblk - 1] for b in blocks]
    p, inv = _softmax_rows(jnp.concatenate(parts, axis=0))
    o_w = _dot(vwt_sc[:, blocks[0] * tq:nk], p) * inv

    g = jax.nn.sigmoid(gt_ref[0:16, :])
    for h in range(NH):
        ls = slice(h * tq, (h + 1) * tq)
        cs = slice(h * DH, (h + 1) * DH)
        mix = (g[3 * h:3 * h + 1] * o_c[:, ls] + g[3 * h + 1:3 * h + 2] * o_s[:, ls]
               + g[3 * h + 2:3 * h + 3] * o_w[:, ls])
        o_ref[:, cs] = (jnp.transpose(mix) * _silu(z_ref[:, cs].astype(F32))).astype(o_ref.dtype)


def _nsa_kernel(q_ref, kc_ref, vc_ref, ks_ref, vs_ref, kw_ref, vw_ref, gt_ref, z_ref, ovt_ref, diagb_ref,
                winb_ref, o_ref, vst_sc, vwt_sc, *, seq, tq):
    i = pl.program_id(1)

    @pl.when(i == 0)
    def _():
        for c in range(seq // tq):
            cols = slice(c * tq, (c + 1) * tq)
            vst_sc[:, cols] = jnp.transpose(vs_ref[cols, :].astype(F32)).astype(BF16)
            vwt_sc[:, cols] = jnp.transpose(vw_ref[cols, :].astype(F32)).astype(BF16)

    for c in range(seq // tq):
        pl.when(i == c)(functools.partial(
            _nsa_tile, c, q_ref, kc_ref, vc_ref, ks_ref, kw_ref, gt_ref, z_ref, ovt_ref, diagb_ref, winb_ref,
            o_ref, vst_sc, vwt_sc, seq=seq, tq=tq))


def _nsa_bias_tables(tq):
    t = jnp.arange(NH * tq)[None, :] % tq
    r = jnp.arange(tq)[:, None]
    diag = jnp.where(r <= t, 0.0, NEG)
    n_wblk = WIN // tq + 1
    win = []
    for b in range(n_wblk):
        diff = t - (r + (b - (n_wblk - 1)) * tq)
        win.append(jnp.where((diff >= 0) & (diff < WIN), 0.0, NEG))
    return diag.astype(F32), jnp.stack(win).astype(F32)


def _nsa(ga, gb, gc, gd_t, kc, vc, ov_t, batch, seq):
    tq = 256
    nq = seq // tq
    assert seq // SEL_BLOCK <= 32 and tq % SEL_BLOCK == 0
    row = lambda b, i: (b * nq + i, 0)
    kv = lambda col: pl.BlockSpec((seq, DH), lambda b, i: (b, col))
    cmp_spec = pl.BlockSpec((None, LANES, DH), lambda b, i: (b, 0, 0))
    full = lambda a: pl.BlockSpec(a.shape, lambda b, i: (0,) * a.ndim)
    diag_b, win_b = _nsa_bias_tables(tq)
    return pl.pallas_call(
        functools.partial(_nsa_kernel, seq=seq, tq=tq),
        out_shape=jax.ShapeDtypeStruct((batch * seq, GROUP), BF16),
        grid=(batch, nq),
        in_specs=[pl.BlockSpec((tq, GROUP), row), cmp_spec, cmp_spec,
                  kv(5), kv(1), kv(6), kv(2),
                  pl.BlockSpec((LANES, tq), lambda b, i: (0, b * nq + i)),
                  pl.BlockSpec((tq, GROUP), lambda b, i: (b * nq + i, GC0)), full(ov_t), full(diag_b), full(win_b)],
        out_specs=pl.BlockSpec((tq, GROUP), row),
        scratch_shapes=[pltpu.VMEM((DH, seq), BF16), pltpu.VMEM((DH, seq), BF16)],
        compiler_params=_cparams(("parallel", "arbitrary")),
        name="nsa_attention",
    )(ga, kc, vc, ga, gb, ga, gb, gd_t, gc, ov_t, diag_b, win_b)


def _mlstm_kernel(q_ref, k_ref, v_ref, gcol_ref, grow_ref, bcol_ref, brow_ref, og_ref, z_ref, ng_ref,
                  tril_ref, triu_ref, out_ref, cn_sc, m_sc, *, tc):
    L = MLSTM_CHUNK

    @pl.when(pl.program_id(1) == 0)
    def _():
        cn_sc[...] = jnp.zeros(cn_sc.shape, F32)
        m_sc[...] = jnp.full(m_sc.shape, NEG, F32)

    gcol = gcol_ref[...] + bcol_ref[...]
    grow = grow_ref[...] + brow_ref[...]
    hi, lo = _split_hi_lo(_log_sigmoid(gcol))
    bcol_all = _dot(tril_ref[...], hi) + _dot(tril_ref[...], lo)
    hi, lo = _split_hi_lo(_log_sigmoid(grow))
    brow_all = _dot(hi, triu_ref[...]) + _dot(lo, triu_ref[...])

    causal = (lax.broadcasted_iota(jnp.int32, (L, L), 1) <= lax.broadcasted_iota(jnp.int32, (L, L), 0))
    ones = jnp.ones((L, DH), BF16)
    mean_mat = jnp.full((DH, DH), 1.0 / DH, BF16)

    def lane_mean(a):
        a_hi, a_lo = _split_hi_lo(a)
        return _dot(a_hi, mean_mat) + _dot(a_lo, mean_mat)

    for c in range(tc // L):
        rs = slice(c * L, (c + 1) * L)
        for h in range(NH):
            cs = slice(h * DH, (h + 1) * DH)
            qh = q_ref[rs, cs]
            kh = k_ref[rs, cs]
            v_aug = jnp.concatenate([v_ref[rs, cs], ones], axis=1)
            ig_col = gcol[rs, GD_I + h:GD_I + h + 1]
            b_col = bcol_all[rs, GD_F + h:GD_F + h + 1]
            ig_row = grow[h:h + 1, rs]
            b_row = brow_all[NH + h:NH + h + 1, rs]
            cn_prev = cn_sc[h]
            m_prev = m_sc[h][:, 0:1]

            d_log = jnp.where(causal, b_col + (ig_row - b_row), NEG)
            m_intra = jnp.max(d_log, axis=-1, keepdims=True)
            m_inter = b_col + m_prev
            m_t = jnp.maximum(m_inter, m_intra)
            w_inter = jnp.exp(m_inter - m_t)
            qk = _dot_nt(qh, kh) * jnp.exp(d_log - (m_t - math.log(SCALE)))
            intra = _dot(qk.astype(BF16), v_aug)
            inter = _dot(qh, cn_prev.astype(BF16))
            num = intra[:, 0:DH] + w_inter * inter[:, 0:DH]
            den = intra[:, DH:2 * DH] + w_inter * inter[:, DH:2 * DH]
            hh = num / jnp.maximum(jnp.abs(den), jnp.exp(-m_t))
            hh = jax.nn.sigmoid(og_ref[rs, cs].astype(F32)) * hh
            dlt = hh - lane_mean(hh)
            hn = dlt * lax.rsqrt(lane_mean(jnp.square(dlt)) + LN_EPS) * ng_ref[:, cs]
            out_ref[rs, cs] = (hn * _silu(z_ref[rs, cs].astype(F32))).astype(out_ref.dtype)

            b_last = b_row[:, L - 1:L]
            m_loc = jnp.max(b_last - b_row + ig_row, axis=-1, keepdims=True)
            e_col = jnp.exp(b_last - b_col + ig_col - m_loc)
            ek = ((e_col * SCALE) * kh.astype(F32)).astype(BF16)
            g_cn = lax.dot_general(ek, v_aug, TN, preferred_element_type=F32)
            m_new = jnp.maximum(b_last + m_prev, m_loc)
            cn_sc[h] = jnp.exp(b_last + m_prev - m_new) * cn_prev + jnp.exp(m_loc - m_new) * g_cn
            m_sc[h] = jnp.broadcast_to(m_new, (1, LANES))


def _mlstm(gb, gc, gd, gd_t, bias_col, bias_row, norm_g, tril, triu, batch, seq):
    tc = MLSTM_CHUNK
    nt = seq // tc
    row = lambda col: (lambda b, j: (b * nt + j, col))
    full = lambda a: pl.BlockSpec(a.shape, lambda b, j: (0,) * a.ndim)
    return pl.pallas_call(
        functools.partial(_mlstm_kernel, tc=tc),
        out_shape=jax.ShapeDtypeStruct((batch * seq, GROUP), BF16),
        grid=(batch, nt),
        in_specs=[pl.BlockSpec((tc, GROUP), row(1)), pl.BlockSpec((tc, GROUP), row(2)),
                  pl.BlockSpec((tc, GROUP), row(3)),
                  pl.BlockSpec((tc, LANES), row(0)),
                  pl.BlockSpec((2 * NH, tc), lambda b, j: (GD_I // (2 * NH), b * nt + j)),
                  full(bias_col), full(bias_row),
                  pl.BlockSpec((tc, GROUP), row(GC0 + 1)), pl.BlockSpec((tc, GROUP), row(GC0 + 2)),
                  full(norm_g), full(tril), full(triu)],
        out_specs=pl.BlockSpec((tc, GROUP), row(0)),
        scratch_shapes=[pltpu.VMEM((NH, DH, 2 * DH), F32), pltpu.VMEM((NH, 1, LANES), F32)],
        compiler_params=_cparams(("parallel", "arbitrary")),
        name="mlstm",
    )(gb, gb, gb, gd, gd_t, bias_col, bias_row, gc, gc, norm_g, tril, triu)


def _lru_kernel(x_ref, z_ref, cw_ref, cb_ref, gw_ref, gb_ref, lam_ref, out_ref, xbuf, h_sc, *, tl):
    pad = 8

    @pl.when(pl.program_id(1) == 0)
    def _():
        xbuf[0:pad, :] = jnp.zeros((pad, GROUP), F32)
        h_sc[...] = jnp.zeros(h_sc.shape, F32)

    x = x_ref[...].astype(F32)
    xbuf[pad:pad + tl, :] = x
    u = cw_ref[3:4, :] * x + cb_ref[...]
    for w in range(3):
        u = u + cw_ref[w:w + 1, :] * xbuf[pad - 3 + w:pad - 3 + w + tl, :]
    xbuf[0:pad, :] = x[tl - pad:tl, :]

    ub = u.astype(BF16)
    pre = []
    for gi in range(2):
        pre.append(jnp.concatenate(
            [_dot(ub[:, n * DH:(n + 1) * DH], gw_ref[gi, n]) for n in range(NH)], axis=1) + gb_ref[gi:gi + 1, :])
    r = jax.nn.sigmoid(pre[0])
    ig = jax.nn.sigmoid(pre[1])
    neg_lam = -lam_ref[...]
    softplus = jnp.maximum(neg_lam, 0.0) + jnp.log(1.0 + jnp.exp(-jnp.abs(neg_lam)))
    a = jnp.exp(-LRU_C * r * softplus)
    bx = jnp.sqrt(1.0 - a * a) * (ig * u)

    rowi = lax.broadcasted_iota(jnp.int32, (tl, GROUP), 0)
    d = 1
    while d < tl:
        keep = rowi >= d
        a_sh = jnp.where(keep, pltpu.roll(a, d, 0), 1.0)
        b_sh = jnp.where(keep, pltpu.roll(bx, d, 0), 0.0)
        bx = a * b_sh + bx
        a = a * a_sh
        d *= 2
    hseq = a * h_sc[...] + bx
    h_sc[...] = hseq[tl - 1:tl, :]
    out_ref[...] = (hseq * _silu(z_ref[...].astype(F32))).astype(out_ref.dtype)


def _lru(gc, conv_w, conv_b, gate_w, gate_b, lam, batch, seq):
    tl = 256
    nt = seq // tl
    row = lambda col: (lambda b, j: (b * nt + j, col))
    full = lambda a: pl.BlockSpec(a.shape, lambda b, j: (0,) * a.ndim)
    return pl.pallas_call(
        functools.partial(_lru_kernel, tl=tl),
        out_shape=jax.ShapeDtypeStruct((batch * seq, GROUP), BF16),
        grid=(batch, nt),
        in_specs=[pl.BlockSpec((tl, GROUP), row(GC0 + 3)), pl.BlockSpec((tl, GROUP), row(GC0 + 4)),
                  full(conv_w), full(conv_b), full(gate_w), full(gate_b), full(lam)],
        out_specs=pl.BlockSpec((tl, GROUP), row(0)),
        scratch_shapes=[pltpu.VMEM((tl + 8, GROUP), F32), pltpu.VMEM((1, GROUP), F32)],
        compiler_params=_cparams(("parallel", "arbitrary")),
        name="rglru",
    )(gc, gc, conv_w, conv_b, gate_w, gate_b, lam)


def _moba_tile(i, q_ref, k_ref, z_ref, o_ref, kmean_sc, vt_sc, *, seq):
    tq = MOBA_BLOCK
    nb = seq // MOBA_BLOCK
    ncand = 8
    nk = (i + 1) * tq
    past = lax.broadcasted_iota(jnp.int32, (ncand, tq), 0) < i
    causal = (lax.broadcasted_iota(jnp.int32, (tq, tq), 0) <= lax.broadcasted_iota(jnp.int32, (tq, tq), 1))
    for h in range(NH):
        cs = slice(h * DH, (h + 1) * DH)
        qh = q_ref[:, cs]
        s = _dot_nt(k_ref[0:nk, cs], qh)
        parts = [jnp.where(causal, s[i * tq:nk], NEG)]
        if i > 0:
            gate = _dot_nt(kmean_sc[h].astype(BF16), qh)[0:ncand]
            score = jnp.where(past, gate, NEG)
            sel = jnp.logical_and(_topk_rows(score, min(MOBA_TOPK, nb)), past)
            bias = jnp.where(sel, 0.0, NEG)
            parts = [s[j * tq:(j + 1) * tq] + bias[j:j + 1] for j in range(i)] + parts
        p, inv = _softmax_rows(jnp.concatenate(parts, axis=0))
        o = jnp.transpose(_dot(vt_sc[h, :, 0:nk], p) * inv)
        o_ref[:, cs] = (o * _silu(z_ref[:, cs].astype(F32))).astype(o_ref.dtype)


def _moba_kernel(q_ref, k_ref, v_ref, z_ref, o_ref, kmean_sc, vt_sc, *, seq):
    i = pl.program_id(1)
    tq = MOBA_BLOCK
    nb = seq // MOBA_BLOCK

    @pl.when(i == 0)
    def _():
        kmean_sc[...] = jnp.zeros(kmean_sc.shape, F32)
        for h in range(NH):
            cs = slice(h * DH, (h + 1) * DH)
            kmean_sc[h, 0:nb, :] = jnp.mean(k_ref[:, cs].astype(F32).reshape(nb, MOBA_BLOCK, DH), axis=1)
            for c in range(nb):
                rows = slice(c * tq, (c + 1) * tq)
                vt_sc[h, :, rows] = jnp.transpose(v_ref[rows, cs].astype(F32)).astype(BF16)

    for c in range(nb):
        pl.when(i == c)(functools.partial(_moba_tile, c, q_ref, k_ref, z_ref, o_ref, kmean_sc, vt_sc, seq=seq))


def _moba(ga, gb, gc, batch, seq):
    tq = MOBA_BLOCK
    nq = seq // tq
    assert nq <= 8
    row = lambda col: (lambda b, i: (b * nq + i, col))
    return pl.pallas_call(
        functools.partial(_moba_kernel, seq=seq),
        out_shape=jax.ShapeDtypeStruct((batch * seq, GROUP), BF16),
        grid=(batch, nq),
        in_specs=[pl.BlockSpec((tq, GROUP), row(2)),
                  pl.BlockSpec((seq, GROUP), lambda b, i: (b, 3)),
                  pl.BlockSpec((seq, GROUP), lambda b, i: (b, 4)),
                  pl.BlockSpec((tq, GROUP), row(GC0 + 5))],
        out_specs=pl.BlockSpec((tq, GROUP), row(0)),
        scratch_shapes=[pltpu.VMEM((NH, LANES, DH), F32), pltpu.VMEM((NH, DH, seq), BF16)],
        compiler_params=_cparams(("parallel", "arbitrary")),
        name="moba",
    )(ga, ga, gb, gc)


def _out_kernel(ya_ref, yb_ref, yc_ref, yd_ref, w_ref, x_ref, g_ref, b_ref, o_ref, ob_ref):
    acc = DEEPNORM_ALPHA * x_ref[...]
    for p, y_ref in enumerate((ya_ref, yb_ref, yc_ref, yd_ref)):
        acc = acc + _dot(y_ref[...], w_ref[p * GROUP:(p + 1) * GROUP, :])
    mu = jnp.mean(acc, axis=-1, keepdims=True)
    var = jnp.mean(jnp.square(acc - mu), axis=-1, keepdims=True)
    y = (acc - mu) * lax.rsqrt(var + LN_EPS) * g_ref[...] + b_ref[...]
    o_ref[...] = y
    ob_ref[...] = y.astype(BF16)


def _out_proj(ys, w_out, x, ln_g, ln_b):
    m, d = x.shape
    tm = 512
    yspec = pl.BlockSpec((tm, GROUP), lambda i: (i, 0))
    full = lambda a: pl.BlockSpec(a.shape, lambda i: (0,) * a.ndim)
    xspec = pl.BlockSpec((tm, d), lambda i: (i, 0))
    return pl.pallas_call(
        _out_kernel,
        out_shape=(jax.ShapeDtypeStruct((m, d), F32), jax.ShapeDtypeStruct((m, d), BF16)),
        grid=(m // tm,),
        in_specs=[yspec] * 4 + [full(w_out), xspec, full(ln_g), full(ln_b)],
        out_specs=(xspec, xspec),
        compiler_params=_cparams(("parallel",)),
        name="out_proj_ln",
    )(*ys, w_out, x, ln_g, ln_b)


_GROUPS = (
    (('nsa_q', 0, 512), ('nsa_kv', 0, 384), (None, 0, 128), ('moba_qkv', 0, 1024)),
    (('nsa_kv', 384, 384), (None, 0, 128), ('mlstm_qkv', 0, 1536), ('moba_qkv', 1024, 512)),
    (('nsa_z', 0, 512), ('mlstm_o', 0, 512), ('mlstm_z', 0, 512), ('lru_x', 0, 512), ('lru_z', 0, 512),
     ('moba_z', 0, 512)),
)
_GATE_PIECES = (('nsa_gate', 0, 12), (None, 0, 4), ('mlstm_if', 0, 8), (None, 0, LANES - 24))


def _regroup_kernel(w_ref, all_ref, gate_ref):
    for pieces, o_ref in ((sum(_GROUPS, ()), all_ref), (_GATE_PIECES, gate_ref)):
        dst = 0
        for name, off, width in pieces:
            if name is None:
                o_ref[dst:dst + width, :] = jnp.zeros((width, o_ref.shape[1]), o_ref.dtype)
            else:
                src = _OFF[name] + off
                o_ref[dst:dst + width, :] = w_ref[src:src + width, :].astype(o_ref.dtype)
            dst += width


def _regroup_w_in(w_in):
    depth, d, n_in = w_in.shape
    wt = jnp.transpose(w_in, (0, 2, 1))
    tk = 256
    n_all = sum(p[2] for pieces in _GROUPS for p in pieces)
    w_all, w_gate = pl.pallas_call(
        _regroup_kernel,
        out_shape=(jax.ShapeDtypeStruct((depth, n_all, d), BF16), jax.ShapeDtypeStruct((depth, LANES, d), F32)),
        grid=(depth, d // tk),
        in_specs=[pl.BlockSpec((None, n_in, tk), lambda l, i: (l, 0, i))],
        out_specs=(pl.BlockSpec((None, n_all, tk), lambda l, i: (l, 0, i)),
                   pl.BlockSpec((None, LANES, tk), lambda l, i: (l, 0, i))),
        compiler_params=_cparams(("parallel", "parallel")),
        name="regroup_w_in",
    )(wt)
    return w_all, w_gate.astype(BF16)


def _rope_tables(seq):
    half = ROT_DIM // 2
    inv_freq = jnp.power(ROPE_THETA, -jnp.arange(half, dtype=F32) * (2.0 / ROT_DIM))
    ang = jnp.arange(seq, dtype=jnp.int32).astype(F32)[:, None] * inv_freq[None, :]
    cos, sin = jnp.cos(ang), jnp.sin(ang)
    ones = jnp.ones((seq, DH - ROT_DIM), F32)
    lane = jnp.arange(DH)
    partner = jnp.where(lane < half, lane + half, lane - half)
    swap = (lane[:, None] == partner[None, :]).astype(BF16)
    return (jnp.concatenate([cos, cos, ones], axis=1),
            jnp.concatenate([-sin, sin, 0.0 * ones], axis=1), swap)


def _cmp_overlap_t(seq):
    n_cmp = seq // CMP_STRIDE - 1
    n_sel = seq // SEL_BLOCK
    cs = jnp.arange(LANES)[None, :] * CMP_STRIDE
    ss = jnp.arange(LANES)[:, None] * SEL_BLOCK
    ov = (cs < ss + SEL_BLOCK) & (cs + CMP_BLOCK > ss)
    ov = ov & (jnp.arange(LANES)[None, :] < n_cmp) & (jnp.arange(LANES)[:, None] < n_sel)
    return ov.astype(BF16)


def _chunk_tri(tc):
    r = jnp.arange(tc)
    same = (r[:, None] // MLSTM_CHUNK) == (r[None, :] // MLSTM_CHUNK)
    tril = (same & (r[None, :] <= r[:, None])).astype(BF16)
    return tril, tril.T


def _layer(x, xb, batch, seq, consts, w_groups, layer, cmp_w1, cmp_w2, cmp_pe, i_bias, f_bias, norm_g,
           conv_w, conv_b, gate_w, gate_b, lam, w_out, ln_g, ln_b):
    rope, ov_t, tril, triu = consts
    w_all, w_gate = w_groups
    ga, gbc, gd, gd_t = _project(xb, w_all, w_gate, layer, rope, seq, min(2048, seq), GROUP)
    gb = gc = gbc

    nb = seq // CMP_STRIDE
    tk = ga[:, 4 * DH:5 * DH].reshape(batch, nb, CMP_STRIDE * DH)
    tv = gb[:, 0:DH].reshape(batch, nb, CMP_STRIDE * DH)
    pe = cmp_pe.reshape(2, 2, CMP_STRIDE * DH)
    kc, vc = _compress(tk, tv, cmp_w1.astype(BF16), cmp_w2.astype(BF16), pe)
    y_a = _nsa(ga, gb, gc, gd_t, kc, vc, ov_t, batch, seq)

    bias = jnp.concatenate([i_bias, f_bias])
    bias_col = jnp.zeros((1, LANES), F32).at[0, GD_I:GD_I + 2 * NH].set(bias)
    bias_row = bias[:, None]
    y_b = _mlstm(gb, gc, gd, gd_t, bias_col, bias_row, norm_g[None, :], tril, triu, batch, seq)

    y_c = _lru(gc, conv_w, conv_b[None, :], gate_w.astype(BF16), gate_b, lam[None, :], batch, seq)

    y_d = _moba(ga, gb, gc, batch, seq)

    return _out_proj((y_a, y_b, y_c, y_d), w_out.astype(BF16), x, ln_g[None, :], ln_b[None, :])


def kernel(x, w_in, nsa_cmp_w1, nsa_cmp_w2, nsa_cmp_pe, mlstm_i_bias, mlstm_f_bias, mlstm_norm_g,
           lru_conv_w, lru_conv_b, lru_gate_w, lru_gate_b, lru_lambda, w_out, ln_g, ln_b):
    batch, seq, d = x.shape
    tril, triu = _chunk_tri(MLSTM_CHUNK)
    consts = (_rope_tables(seq), _cmp_overlap_t(seq), tril, triu)
    xf = x.reshape(batch * seq, d)
    xb = xf.astype(BF16)
    w_groups = _regroup_w_in(w_in)
    for l in range(w_in.shape[0]):
        xf, xb = _layer(xf, xb, batch, seq, consts, w_groups, l, nsa_cmp_w1[l], nsa_cmp_w2[l],
                        nsa_cmp_pe[l], mlstm_i_bias[l], mlstm_f_bias[l], mlstm_norm_g[l], lru_conv_w[l],
                        lru_conv_b[l], lru_gate_w[l], lru_gate_b[l], lru_lambda[l], w_out[l], ln_g[l], ln_b[l])
    return xf.reshape(batch, seq, d)
```

```python
import functools
import math

import jax
import jax.numpy as jnp
from jax import lax
from jax.experimental import pallas as pl
from jax.experimental.pallas import tpu as pltpu

F32 = jnp.float32
BF16 = jnp.bfloat16

D_MODEL = 2048
DEPTH = 2
GROUP = 512
DH = 128
NH = 4
ROT_DIM = 32
ROPE_THETA = 500000.0

CMP_BLOCK = 32
CMP_STRIDE = 16
SEL_BLOCK = 64
SEL_TOPK = 8
WIN = 256
FORCE_SCORE = 1e9

MLSTM_CHUNK = 512
LRU_C = 8.0
MOBA_BLOCK = 256
MOBA_TOPK = 3

DEEPNORM_ALPHA = (2 * DEPTH) ** 0.25
NEG = -1e30
LN_EPS = 1e-5
SCALE = DH ** -0.5
EXP2_SCALE = SCALE * math.log2(math.e)

LANES = 128
SUB = 8
VMEM_LIMIT = 56 * 1024 * 1024

_OFF = {}
_o = 0
for _name, _w in (('nsa_q', 512), ('nsa_kv', 768), ('nsa_gate', 12), ('nsa_z', 512), ('mlstm_qkv', 1536),
                  ('mlstm_if', 8), ('mlstm_o', 512), ('mlstm_z', 512), ('lru_x', 512), ('lru_z', 512),
                  ('moba_qkv', 1536), ('moba_z', 512)):
    _OFF[_name] = _o
    _o += _w

GD_I = 16
GD_F = 20
GC0 = 5

NT = (((1,), (1,)), ((), ()))
TN = (((0,), (0,)), ((), ()))


def _cparams(sem):
    return pltpu.CompilerParams(dimension_semantics=sem, vmem_limit_bytes=VMEM_LIMIT)


def _dot(a, b):
    return jnp.dot(a, b, preferred_element_type=F32)


def _dot_nt(a, b):
    return lax.dot_general(a, b, NT, preferred_element_type=F32)


def _split_hi_lo(a):
    hi = a.astype(BF16)
    lo = (a - hi.astype(F32)).astype(BF16)
    return hi, lo


def _sigmoid(x):
    return 0.5 * jnp.tanh(0.5 * x) + 0.5


def _silu(x):
    return x * _sigmoid(x)


def _log_sigmoid(x):
    return jnp.minimum(x, 0.0) - jnp.log(1.0 + jnp.exp(-jnp.abs(x)))


def _proj_kernel(x_ref, w_ref, wg_ref, cos_ref, sin_ref, swap_ref, ga_ref, gbc_ref, gd_ref, gdt_ref, *, n_rot,
                 n_tiles):
    j = pl.program_id(1)
    x = x_ref[...]
    acc = _dot_nt(x, w_ref[...])
    gbc_ref[...] = acc.astype(gbc_ref.dtype)

    @pl.when(j < n_rot)
    def _():
        c = cos_ref[...]
        s = sin_ref[...]
        for h in range(acc.shape[1] // DH):
            t = acc[:, h * DH:(h + 1) * DH]
            swapped = _dot(t.astype(BF16), swap_ref[...])
            ga_ref[:, h * DH:(h + 1) * DH] = (t * c + swapped * s).astype(ga_ref.dtype)

    @pl.when(j == n_tiles - 1)
    def _():
        gd = _dot_nt(x, wg_ref[...])
        gd_ref[...] = gd
        gdt_ref[...] = jnp.transpose(gd)


def _project(xb, w_all, w_gate, layer, rope, seq, tm, tn):
    m, k = xb.shape
    widths = [sum(p[2] for p in pieces) for pieces in _GROUPS]
    assert tn == GROUP and widths[1] == GC0 * GROUP
    n_rot = widths[0] // tn
    n_tiles = sum(widths) // tn
    nrep = seq // tm
    rope_spec = pl.BlockSpec((tm, DH), lambda i, j: (i % nrep, 0))
    cos, sin, swap = rope
    return pl.pallas_call(
        functools.partial(_proj_kernel, n_rot=n_rot, n_tiles=n_tiles),
        out_shape=(jax.ShapeDtypeStruct((m, widths[0]), BF16), jax.ShapeDtypeStruct((m, widths[1] + widths[2]), BF16),
                   jax.ShapeDtypeStruct((m, LANES), F32), jax.ShapeDtypeStruct((LANES, m), F32)),
        grid=(m // tm, n_tiles),
        in_specs=[pl.BlockSpec((tm, k), lambda i, j: (i, 0)), pl.BlockSpec((None, tn, k), lambda i, j: (layer, j, 0)),
                  pl.BlockSpec((None, LANES, k), lambda i, j: (layer, 0, 0)), rope_spec, rope_spec,
                  pl.BlockSpec((DH, DH), lambda i, j: (0, 0))],
        out_specs=(pl.BlockSpec((tm, tn), lambda i, j: (i, jnp.minimum(j, n_rot - 1))),
                   pl.BlockSpec((tm, tn), lambda i, j: (i, jnp.maximum(j - n_rot, 0))),
                   pl.BlockSpec((tm, LANES), lambda i, j: (i, 0)), pl.BlockSpec((LANES, tm), lambda i, j: (0, i))),
        compiler_params=_cparams(("parallel", "arbitrary")),
        name="in_proj",
    )(xb, w_all, w_gate, cos, sin, swap)


def _compress_kernel(tk_ref, tv_ref, w1_ref, w2_ref, pe_ref, kc_ref, vc_ref, *, nb):
    half = (CMP_BLOCK // 2) * DH
    for idx, (t_ref, o_ref) in enumerate(((tk_ref, kc_ref), (tv_ref, vc_ref))):
        t = t_ref[...].astype(F32)
        lo = (t + pe_ref[idx, 0:1, :]).astype(BF16)
        hi = (t + pe_ref[idx, 1:2, :]).astype(BF16)
        a = _dot(lo, w1_ref[idx, 0:half, :])
        b = _dot(hi, w1_ref[idx, half:2 * half, :])
        hid = _silu(a + pltpu.roll(b, nb - 1, 0))
        out = _dot(hid.astype(BF16), w2_ref[idx])
        o_ref[...] = jnp.zeros(o_ref.shape, o_ref.dtype)
        o_ref[0:nb, :] = out.astype(o_ref.dtype)


def _compress(tk, tv, w1, w2, pe):
    b, nb, width = tk.shape
    blk = pl.BlockSpec((None, nb, width), lambda i: (i, 0, 0))
    full = lambda a: pl.BlockSpec(a.shape, lambda i: (0,) * a.ndim)
    out = jax.ShapeDtypeStruct((b, LANES, DH), BF16)
    return pl.pallas_call(
        functools.partial(_compress_kernel, nb=nb),
        out_shape=(out, out),
        grid=(b,),
        in_specs=[blk, blk, full(w1), full(w2), full(pe)],
        out_specs=(pl.BlockSpec((None, LANES, DH), lambda i: (i, 0, 0)),) * 2,
        compiler_params=_cparams(("parallel",)),
        name="nsa_compress",
    )(tk, tv, w1, w2, pe)


def _topk_rows(score, k_top):
    rowi = lax.broadcasted_iota(jnp.int32, score.shape, 0)
    rank = jnp.zeros(score.shape, F32)
    for k in range(score.shape[0]):
        sk = score[k:k + 1, :]
        beats = jnp.logical_or(sk > score, jnp.logical_and(sk == score, rowi > k))
        rank = rank + jnp.where(beats, 1.0, 0.0)
    return rank < k_top


def _prescale(q):
    return (q.astype(F32) * EXP2_SCALE).astype(q.dtype)


def _softmax_rows(s):
    p = jnp.exp2(s - jnp.max(s, axis=0, keepdims=True))
    return p.astype(BF16), 1.0 / jnp.sum(p, axis=0, keepdims=True)


def _nsa_tile(i, q_ref, kc_ref, vc_ref, ks_ref, kw_ref, gt_ref, z_ref, ovt_ref, diagb_ref, winb_ref,
              o_ref, vst_sc, vwt_sc, *, seq, tq):
    width = NH * tq
    n_cmp = seq // CMP_STRIDE - 1
    n_sel = seq // SEL_BLOCK
    n_rows = 32
    q = q_ref[...]
    qs = jnp.concatenate([q[:, h * DH:(h + 1) * DH] for h in range(NH)], axis=0)
    qs = _prescale(qs)

    rown = lax.broadcasted_iota(jnp.int32, (LANES, width), 0)
    tok = i * tq + (lax.broadcasted_iota(jnp.int32, (LANES, width), 1) & (tq - 1))
    mask_c = jnp.logical_and(rown * CMP_STRIDE + (CMP_BLOCK - 1) <= tok, rown < n_cmp)
    s = jnp.where(mask_c, _dot_nt(kc_ref[...], qs), NEG)
    e = jnp.exp2(s - jnp.max(s, axis=0, keepdims=True))
    p_c = jnp.where(mask_c, e, 0.0) * (1.0 / jnp.sum(e, axis=0, keepdims=True))
    o_c = lax.dot_general(vc_ref[...], p_c.astype(BF16), TN, preferred_element_type=F32)

    p_sum = p_c[:, 0:tq]
    for h in range(1, NH):
        p_sum = p_sum + p_c[:, h * tq:(h + 1) * tq]
    p_hi, p_lo = _split_hi_lo(p_sum)
    imp = (_dot(ovt_ref[...], p_hi) + _dot(ovt_ref[...], p_lo))[0:n_rows]
    rowj = lax.broadcasted_iota(jnp.int32, (n_rows, tq), 0)
    t_q = i * tq + lax.broadcasted_iota(jnp.int32, (n_rows, tq), 1)
    cur = t_q >> 6
    forced = jnp.logical_or(rowj == 0, jnp.logical_or(rowj == cur, rowj == cur - 1))
    valid = rowj * SEL_BLOCK <= t_q
    score = jnp.where(forced, FORCE_SCORE, jnp.where(valid, imp, NEG))
    score = jnp.where(rowj < n_sel, score, -jnp.inf)
    bias = jnp.where(_topk_rows(score, min(SEL_TOPK, n_sel)), 0.0, NEG)
    bias = jnp.concatenate([bias] * NH, axis=1)

    nk = (i + 1) * tq
    s = _dot_nt(ks_ref[0:nk, :], qs)
    parts = [s[j * SEL_BLOCK:(j + 1) * SEL_BLOCK] + bias[j:j + 1] for j in range(nk // SEL_BLOCK)]
    own = jnp.concatenate(parts[i * tq // SEL_BLOCK:], axis=0) + diagb_ref[...]
    p, inv = _softmax_rows(jnp.concatenate(parts[:i * tq // SEL_BLOCK] + [own], axis=0))
    o_s = _dot(vst_sc[:, 0:nk], p) * inv

    n_wblk = winb_ref.shape[0]
    blocks = [b for b in range(i - n_wblk + 1, i + 1) if b >= 0]
    parts = [_dot_nt(kw_ref[b * tq:(b + 1) * tq, :], qs) + winb_ref[b - i + n_wblk - 1] for b in blocks]
    p, inv = _softmax_rows(jnp.concatenate(parts, axis=0))
    o_w = _dot(vwt_sc[:, blocks[0] * tq:nk], p) * inv

    g = _sigmoid(gt_ref[0:16, :])
    for h in range(NH):
        ls = slice(h * tq, (h + 1) * tq)
        cs = slice(h * DH, (h + 1) * DH)
        mix = (g[3 * h:3 * h + 1] * o_c[:, ls] + g[3 * h + 1:3 * h + 2] * o_s[:, ls]
               + g[3 * h + 2:3 * h + 3] * o_w[:, ls])
        o_ref[:, cs] = (jnp.transpose(mix) * _silu(z_ref[:, cs].astype(F32))).astype(o_ref.dtype)


def _nsa_kernel(q_ref, kc_ref, vc_ref, ks_ref, vs_ref, kw_ref, vw_ref, gt_ref, z_ref, ovt_ref, diagb_ref,
                winb_ref, o_ref, vst_sc, vwt_sc, *, seq, tq):
    i = pl.program_id(1)

    @pl.when(i == 0)
    def _():
        for c in range(seq // tq):
            cols = slice(c * tq, (c + 1) * tq)
            vst_sc[:, cols] = jnp.transpose(vs_ref[cols, :].astype(F32)).astype(BF16)
            vwt_sc[:, cols] = jnp.transpose(vw_ref[cols, :].astype(F32)).astype(BF16)

    for c in range(seq // tq):
        pl.when(i == c)(functools.partial(
            _nsa_tile, c, q_ref, kc_ref, vc_ref, ks_ref, kw_ref, gt_ref, z_ref, ovt_ref, diagb_ref, winb_ref,
            o_ref, vst_sc, vwt_sc, seq=seq, tq=tq))


def _nsa_bias_tables(tq):
    t = jnp.arange(NH * tq)[None, :] % tq
    r = jnp.arange(tq)[:, None]
    diag = jnp.where(r <= t, 0.0, NEG)
    n_wblk = WIN // tq + 1
    win = []
    for b in range(n_wblk):
        diff = t - (r + (b - (n_wblk - 1)) * tq)
        win.append(jnp.where((diff >= 0) & (diff < WIN), 0.0, NEG))
    return diag.astype(F32), jnp.stack(win).astype(F32)


def _nsa(ga, gb, gc, gd_t, kc, vc, ov_t, batch, seq):
    tq = 256
    nq = seq // tq
    assert seq // SEL_BLOCK <= 32 and tq % SEL_BLOCK == 0
    row = lambda b, i: (b * nq + i, 0)
    kv = lambda col: pl.BlockSpec((seq, DH), lambda b, i: (b, col))
    cmp_spec = pl.BlockSpec((None, LANES, DH), lambda b, i: (b, 0, 0))
    full = lambda a: pl.BlockSpec(a.shape, lambda b, i: (0,) * a.ndim)
    diag_b, win_b = _nsa_bias_tables(tq)
    return pl.pallas_call(
        functools.partial(_nsa_kernel, seq=seq, tq=tq),
        out_shape=jax.ShapeDtypeStruct((batch * seq, GROUP), BF16),
        grid=(batch, nq),
        in_specs=[pl.BlockSpec((tq, GROUP), row), cmp_spec, cmp_spec,
                  kv(5), kv(1), kv(6), kv(2),
                  pl.BlockSpec((LANES, tq), lambda b, i: (0, b * nq + i)),
                  pl.BlockSpec((tq, GROUP), lambda b, i: (b * nq + i, GC0)), full(ov_t), full(diag_b), full(win_b)],
        out_specs=pl.BlockSpec((tq, GROUP), row),
        scratch_shapes=[pltpu.VMEM((DH, seq), BF16), pltpu.VMEM((DH, seq), BF16)],
        compiler_params=_cparams(("parallel", "arbitrary")),
        name="nsa_attention",
    )(ga, kc, vc, ga, gb, ga, gb, gd_t, gc, ov_t, diag_b, win_b)


def _mlstm_kernel(q_ref, k_ref, v_ref, gcol_ref, grow_ref, bcol_ref, brow_ref, og_ref, z_ref, ng_ref,
                  tril_ref, triu_ref, out_ref, cn_sc, m_sc, *, tc):
    L = MLSTM_CHUNK

    @pl.when(pl.program_id(1) == 0)
    def _():
        cn_sc[...] = jnp.zeros(cn_sc.shape, F32)
        m_sc[...] = jnp.full(m_sc.shape, NEG, F32)

    gcol = gcol_ref[...] + bcol_ref[...]
    grow = grow_ref[...] + brow_ref[...]
    hi, lo = _split_hi_lo(_log_sigmoid(gcol))
    bcol_all = _dot(tril_ref[...], hi) + _dot(tril_ref[...], lo)
    hi, lo = _split_hi_lo(_log_sigmoid(grow))
    brow_all = _dot(hi, triu_ref[...]) + _dot(lo, triu_ref[...])

    causal = (lax.broadcasted_iota(jnp.int32, (L, L), 1) <= lax.broadcasted_iota(jnp.int32, (L, L), 0))
    ones = jnp.ones((L, DH), BF16)
    mean_mat = jnp.full((DH, DH), 1.0 / DH, BF16)

    def lane_mean(a):
        a_hi, a_lo = _split_hi_lo(a)
        return _dot(a_hi, mean_mat) + _dot(a_lo, mean_mat)

    for c in range(tc // L):
        rs = slice(c * L, (c + 1) * L)
        for h in range(NH):
            cs = slice(h * DH, (h + 1) * DH)
            qh = q_ref[rs, cs]
            kh = k_ref[rs, cs]
            v_aug = jnp.concatenate([v_ref[rs, cs], ones], axis=1)
            ig_col = gcol[rs, GD_I + h:GD_I + h + 1]
            b_col = bcol_all[rs, GD_F + h:GD_F + h + 1]
            ig_row = grow[h:h + 1, rs]
            b_row = brow_all[NH + h:NH + h + 1, rs]
            cn_prev = cn_sc[h]
            m_prev = m_sc[h][:, 0:1]

            d_log = jnp.where(causal, b_col + (ig_row - b_row), NEG)
            m_intra = jnp.max(d_log, axis=-1, keepdims=True)
            m_inter = b_col + m_prev
            m_t = jnp.maximum(m_inter, m_intra)
            w_inter = jnp.exp(m_inter - m_t)
            qk = _dot_nt(qh, kh) * jnp.exp(d_log - (m_t - math.log(SCALE)))
            intra = _dot(qk.astype(BF16), v_aug)
            inter = _dot(qh, cn_prev.astype(BF16))
            num = intra[:, 0:DH] + w_inter * inter[:, 0:DH]
            den = intra[:, DH:2 * DH] + w_inter * inter[:, DH:2 * DH]
            hh = num / jnp.maximum(jnp.abs(den), jnp.exp(-m_t))
            hh = _sigmoid(og_ref[rs, cs].astype(F32)) * hh
            dlt = hh - lane_mean(hh)
            hn = dlt * lax.rsqrt(lane_mean(jnp.square(dlt)) + LN_EPS) * ng_ref[:, cs]
            out_ref[rs, cs] = (hn * _silu(z_ref[rs, cs].astype(F32))).astype(out_ref.dtype)

            b_last = b_row[:, L - 1:L]
            m_loc = jnp.max(b_last - b_row + ig_row, axis=-1, keepdims=True)
            e_col = jnp.exp(b_last - b_col + ig_col - m_loc)
            ek = ((e_col * SCALE) * kh.astype(F32)).astype(BF16)
            g_cn = lax.dot_general(ek, v_aug, TN, preferred_element_type=F32)
            m_new = jnp.maximum(b_last + m_prev, m_loc)
            cn_sc[h] = jnp.exp(b_last + m_prev - m_new) * cn_prev + jnp.exp(m_loc - m_new) * g_cn
            m_sc[h] = jnp.broadcast_to(m_new, (1, LANES))


def _mlstm(gb, gc, gd, gd_t, bias_col, bias_row, norm_g, tril, triu, batch, seq):
    tc = MLSTM_CHUNK
    nt = seq // tc
    row = lambda col: (lambda b, j: (b * nt + j, col))
    full = lambda a: pl.BlockSpec(a.shape, lambda b, j: (0,) * a.ndim)
    return pl.pallas_call(
        functools.partial(_mlstm_kernel, tc=tc),
        out_shape=jax.ShapeDtypeStruct((batch * seq, GROUP), BF16),
        grid=(batch, nt),
        in_specs=[pl.BlockSpec((tc, GROUP), row(1)), pl.BlockSpec((tc, GROUP), row(2)),
                  pl.BlockSpec((tc, GROUP), row(3)),
                  pl.BlockSpec((tc, LANES), row(0)),
                  pl.BlockSpec((2 * NH, tc), lambda b, j: (GD_I // (2 * NH), b * nt + j)),
                  full(bias_col), full(bias_row),
                  pl.BlockSpec((tc, GROUP), row(GC0 + 1)), pl.BlockSpec((tc, GROUP), row(GC0 + 2)),
                  full(norm_g), full(tril), full(triu)],
        out_specs=pl.BlockSpec((tc, GROUP), row(0)),
        scratch_shapes=[pltpu.VMEM((NH, DH, 2 * DH), F32), pltpu.VMEM((NH, 1, LANES), F32)],
        compiler_params=_cparams(("parallel", "arbitrary")),
        name="mlstm",
    )(gb, gb, gb, gd, gd_t, bias_col, bias_row, gc, gc, norm_g, tril, triu)


def _lru_kernel(x_ref, z_ref, cw_ref, cb_ref, gw_ref, gb_ref, lam_ref, out_ref, xbuf, h_sc, *, tl):
    pad = 8

    @pl.when(pl.program_id(1) == 0)
    def _():
        xbuf[0:pad, :] = jnp.zeros((pad, GROUP), F32)
        h_sc[...] = jnp.zeros(h_sc.shape, F32)

    x = x_ref[...].astype(F32)
    xbuf[pad:pad + tl, :] = x
    u = cw_ref[3:4, :] * x + cb_ref[...]
    for w in range(3):
        u = u + cw_ref[w:w + 1, :] * xbuf[pad - 3 + w:pad - 3 + w + tl, :]
    xbuf[0:pad, :] = x[tl - pad:tl, :]

    ub = u.astype(BF16)
    pre = []
    for gi in range(2):
        pre.append(jnp.concatenate(
            [_dot(ub[:, n * DH:(n + 1) * DH], gw_ref[gi, n]) for n in range(NH)], axis=1) + gb_ref[gi:gi + 1, :])
    r = _sigmoid(pre[0])
    ig = _sigmoid(pre[1])
    neg_lam = -lam_ref[...]
    softplus = jnp.maximum(neg_lam, 0.0) + jnp.log(1.0 + jnp.exp(-jnp.abs(neg_lam)))
    a = jnp.exp(-LRU_C * r * softplus)
    bx = jnp.sqrt(1.0 - a * a) * (ig * u)

    sub = lax.broadcasted_iota(jnp.int32, (tl, GROUP), 0) & (SUB - 1)
    d = 1
    while d < SUB:
        keep = sub >= d
        a_sh = jnp.where(keep, pltpu.roll(a, d, 0), 1.0)
        b_sh = jnp.where(keep, pltpu.roll(bx, d, 0), 0.0)
        bx = a * b_sh + bx
        a = a * a_sh
        d *= 2
    carry = jnp.broadcast_to(h_sc[...], (SUB, GROUP))
    groups = []
    for g in range(tl // SUB):
        hg = a[g * SUB:(g + 1) * SUB] * carry + bx[g * SUB:(g + 1) * SUB]
        groups.append(hg)
        carry = jnp.broadcast_to(hg[SUB - 1:SUB], (SUB, GROUP))
    hseq = jnp.concatenate(groups, axis=0)
    h_sc[...] = hseq[tl - 1:tl, :]
    out_ref[...] = (hseq * _silu(z_ref[...].astype(F32))).astype(out_ref.dtype)


def _lru(gc, conv_w, conv_b, gate_w, gate_b, lam, batch, seq):
    tl = 256
    nt = seq // tl
    row = lambda col: (lambda b, j: (b * nt + j, col))
    full = lambda a: pl.BlockSpec(a.shape, lambda b, j: (0,) * a.ndim)
    return pl.pallas_call(
        functools.partial(_lru_kernel, tl=tl),
        out_shape=jax.ShapeDtypeStruct((batch * seq, GROUP), BF16),
        grid=(batch, nt),
        in_specs=[pl.BlockSpec((tl, GROUP), row(GC0 + 3)), pl.BlockSpec((tl, GROUP), row(GC0 + 4)),
                  full(conv_w), full(conv_b), full(gate_w), full(gate_b), full(lam)],
        out_specs=pl.BlockSpec((tl, GROUP), row(0)),
        scratch_shapes=[pltpu.VMEM((tl + 8, GROUP), F32), pltpu.VMEM((1, GROUP), F32)],
        compiler_params=_cparams(("parallel", "arbitrary")),
        name="rglru",
    )(gc, gc, conv_w, conv_b, gate_w, gate_b, lam)


def _moba_tile(i, q_ref, k_ref, z_ref, o_ref, kmean_sc, vt_sc, *, seq):
    tq = MOBA_BLOCK
    nb = seq // MOBA_BLOCK
    ncand = 8
    nk = (i + 1) * tq
    past = lax.broadcasted_iota(jnp.int32, (ncand, tq), 0) < i
    causal = (lax.broadcasted_iota(jnp.int32, (tq, tq), 0) <= lax.broadcasted_iota(jnp.int32, (tq, tq), 1))
    for h in range(NH):
        cs = slice(h * DH, (h + 1) * DH)
        qh = q_ref[:, cs]
        s = _dot_nt(k_ref[0:nk, cs], _prescale(qh))
        parts = [jnp.where(causal, s[i * tq:nk], NEG)]
        if i > 0:
            gate = _dot_nt(kmean_sc[h].astype(BF16), qh)[0:ncand]
            score = jnp.where(past, gate, NEG)
            sel = jnp.logical_and(_topk_rows(score, min(MOBA_TOPK, nb)), past)
            bias = jnp.where(sel, 0.0, NEG)
            parts = [s[j * tq:(j + 1) * tq] + bias[j:j + 1] for j in range(i)] + parts
        p, inv = _softmax_rows(jnp.concatenate(parts, axis=0))
        o = jnp.transpose(_dot(vt_sc[h, :, 0:nk], p) * inv)
        o_ref[:, cs] = (o * _silu(z_ref[:, cs].astype(F32))).astype(o_ref.dtype)


def _moba_kernel(q_ref, k_ref, v_ref, z_ref, o_ref, kmean_sc, vt_sc, *, seq):
    i = pl.program_id(1)
    tq = MOBA_BLOCK
    nb = seq // MOBA_BLOCK

    @pl.when(i == 0)
    def _():
        kmean_sc[...] = jnp.zeros(kmean_sc.shape, F32)
        for h in range(NH):
            cs = slice(h * DH, (h + 1) * DH)
            kmean_sc[h, 0:nb, :] = jnp.mean(k_ref[:, cs].astype(F32).reshape(nb, MOBA_BLOCK, DH), axis=1)
            for c in range(nb):
                rows = slice(c * tq, (c + 1) * tq)
                vt_sc[h, :, rows] = jnp.transpose(v_ref[rows, cs].astype(F32)).astype(BF16)

    for c in range(nb):
        pl.when(i == c)(functools.partial(_moba_tile, c, q_ref, k_ref, z_ref, o_ref, kmean_sc, vt_sc, seq=seq))


def _moba(ga, gb, gc, batch, seq):
    tq = MOBA_BLOCK
    nq = seq // tq
    assert nq <= 8
    row = lambda col: (lambda b, i: (b * nq + i, col))
    return pl.pallas_call(
        functools.partial(_moba_kernel, seq=seq),
        out_shape=jax.ShapeDtypeStruct((batch * seq, GROUP), BF16),
        grid=(batch, nq),
        in_specs=[pl.BlockSpec((tq, GROUP), row(2)),
                  pl.BlockSpec((seq, GROUP), lambda b, i: (b, 3)),
                  pl.BlockSpec((seq, GROUP), lambda b, i: (b, 4)),
                  pl.BlockSpec((tq, GROUP), row(GC0 + 5))],
        out_specs=pl.BlockSpec((tq, GROUP), row(0)),
        scratch_shapes=[pltpu.VMEM((NH, LANES, DH), F32), pltpu.VMEM((NH, DH, seq), BF16)],
        compiler_params=_cparams(("parallel", "arbitrary")),
        name="moba",
    )(ga, ga, gb, gc)


def _out_kernel(ya_ref, yb_ref, yc_ref, yd_ref, w_ref, x_ref, g_ref, b_ref, o_ref, ob_ref):
    acc = DEEPNORM_ALPHA * x_ref[...]
    for p, y_ref in enumerate((ya_ref, yb_ref, yc_ref, yd_ref)):
        acc = acc + _dot(y_ref[...], w_ref[p * GROUP:(p + 1) * GROUP, :])
    mu = jnp.mean(acc, axis=-1, keepdims=True)
    var = jnp.mean(jnp.square(acc - mu), axis=-1, keepdims=True)
    y = (acc - mu) * lax.rsqrt(var + LN_EPS) * g_ref[...] + b_ref[...]
    o_ref[...] = y
    ob_ref[...] = y.astype(BF16)


def _out_proj(ys, w_out, x, ln_g, ln_b):
    m, d = x.shape
    tm = 512
    yspec = pl.BlockSpec((tm, GROUP), lambda i: (i, 0))
    full = lambda a: pl.BlockSpec(a.shape, lambda i: (0,) * a.ndim)
    xspec = pl.BlockSpec((tm, d), lambda i: (i, 0))
    return pl.pallas_call(
        _out_kernel,
        out_shape=(jax.ShapeDtypeStruct((m, d), F32), jax.ShapeDtypeStruct((m, d), BF16)),
        grid=(m // tm,),
        in_specs=[yspec] * 4 + [full(w_out), xspec, full(ln_g), full(ln_b)],
        out_specs=(xspec, xspec),
        compiler_params=_cparams(("parallel",)),
        name="out_proj_ln",
    )(*ys, w_out, x, ln_g, ln_b)


_GROUPS = (
    (('nsa_q', 0, 512), ('nsa_kv', 0, 384), (None, 0, 128), ('moba_qkv', 0, 1024)),
    (('nsa_kv', 384, 384), (None, 0, 128), ('mlstm_qkv', 0, 1536), ('moba_qkv', 1024, 512)),
    (('nsa_z', 0, 512), ('mlstm_o', 0, 512), ('mlstm_z', 0, 512), ('lru_x', 0, 512), ('lru_z', 0, 512),
     ('moba_z', 0, 512)),
)
_GATE_PIECES = (('nsa_gate', 0, 12), (None, 0, 4), ('mlstm_if', 0, 8), (None, 0, LANES - 24))


def _regroup_kernel(w_ref, all_ref, gate_ref):
    for pieces, o_ref in ((sum(_GROUPS, ()), all_ref), (_GATE_PIECES, gate_ref)):
        dst = 0
        for name, off, width in pieces:
            if name is None:
                o_ref[dst:dst + width, :] = jnp.zeros((width, o_ref.shape[1]), o_ref.dtype)
            else:
                src = _OFF[name] + off
                o_ref[dst:dst + width, :] = w_ref[src:src + width, :].astype(o_ref.dtype)
            dst += width


def _regroup_w_in(w_in):
    depth, d, n_in = w_in.shape
    wt = jnp.transpose(w_in, (0, 2, 1))
    tk = 256
    n_all = sum(p[2] for pieces in _GROUPS for p in pieces)
    w_all, w_gate = pl.pallas_call(
        _regroup_kernel,
        out_shape=(jax.ShapeDtypeStruct((depth, n_all, d), BF16), jax.ShapeDtypeStruct((depth, LANES, d), F32)),
        grid=(depth, d // tk),
        in_specs=[pl.BlockSpec((None, n_in, tk), lambda l, i: (l, 0, i))],
        out_specs=(pl.BlockSpec((None, n_all, tk), lambda l, i: (l, 0, i)),
                   pl.BlockSpec((None, LANES, tk), lambda l, i: (l, 0, i))),
        compiler_params=_cparams(("parallel", "parallel")),
        name="regroup_w_in",
    )(wt)
    return w_all, w_gate.astype(BF16)


def _rope_tables(seq):
    half = ROT_DIM // 2
    inv_freq = jnp.power(ROPE_THETA, -jnp.arange(half, dtype=F32) * (2.0 / ROT_DIM))
    ang = jnp.arange(seq, dtype=jnp.int32).astype(F32)[:, None] * inv_freq[None, :]
    cos, sin = jnp.cos(ang), jnp.sin(ang)
    ones = jnp.ones((seq, DH - ROT_DIM), F32)
    lane = jnp.arange(DH)
    partner = jnp.where(lane < half, lane + half, lane - half)
    swap = (lane[:, None] == partner[None, :]).astype(BF16)
    return (jnp.concatenate([cos, cos, ones], axis=1),
            jnp.concatenate([-sin, sin, 0.0 * ones], axis=1), swap)


def _cmp_overlap_t(seq):
    n_cmp = seq // CMP_STRIDE - 1
    n_sel = seq // SEL_BLOCK
    cs = jnp.arange(LANES)[None, :] * CMP_STRIDE
    ss = jnp.arange(LANES)[:, None] * SEL_BLOCK
    ov = (cs < ss + SEL_BLOCK) & (cs + CMP_BLOCK > ss)
    ov = ov & (jnp.arange(LANES)[None, :] < n_cmp) & (jnp.arange(LANES)[:, None] < n_sel)
    return ov.astype(BF16)


def _chunk_tri(tc):
    r = jnp.arange(tc)
    same = (r[:, None] // MLSTM_CHUNK) == (r[None, :] // MLSTM_CHUNK)
    tril = (same & (r[None, :] <= r[:, None])).astype(BF16)
    return tril, tril.T


def _layer(x, xb, batch, seq, consts, w_groups, layer, cmp_w1, cmp_w2, cmp_pe, i_bias, f_bias, norm_g,
           conv_w, conv_b, gate_w, gate_b, lam, w_out, ln_g, ln_b):
    rope, ov_t, tril, triu = consts
    w_all, w_gate = w_groups
    ga, gbc, gd, gd_t = _project(xb, w_all, w_gate, layer, rope, seq, min(2048, seq), GROUP)
    gb = gc = gbc

    nb = seq // CMP_STRIDE
    tk = ga[:, 4 * DH:5 * DH].reshape(batch, nb, CMP_STRIDE * DH)
    tv = gb[:, 0:DH].reshape(batch, nb, CMP_STRIDE * DH)
    pe = cmp_pe.reshape(2, 2, CMP_STRIDE * DH)
    kc, vc = _compress(tk, tv, cmp_w1.astype(BF16), cmp_w2.astype(BF16), pe)
    y_a = _nsa(ga, gb, gc, gd_t, kc, vc, ov_t, batch, seq)

    bias = jnp.concatenate([i_bias, f_bias])
    bias_col = jnp.zeros((1, LANES), F32).at[0, GD_I:GD_I + 2 * NH].set(bias)
    bias_row = bias[:, None]
    y_b = _mlstm(gb, gc, gd, gd_t, bias_col, bias_row, norm_g[None, :], tril, triu, batch, seq)

    y_c = _lru(gc, conv_w, conv_b[None, :], gate_w.astype(BF16), gate_b, lam[None, :], batch, seq)

    y_d = _moba(ga, gb, gc, batch, seq)

    return _out_proj((y_a, y_b, y_c, y_d), w_out.astype(BF16), x, ln_g[None, :], ln_b[None, :])


def kernel(x, w_in, nsa_cmp_w1, nsa_cmp_w2, nsa_cmp_pe, mlstm_i_bias, mlstm_f_bias, mlstm_norm_g,
           lru_conv_w, lru_conv_b, lru_gate_w, lru_gate_b, lru_lambda, w_out, ln_g, ln_b):
    batch, seq, d = x.shape
    tril, triu = _chunk_tri(MLSTM_CHUNK)
    consts = (_rope_tables(seq), _cmp_overlap_t(seq), tril, triu)
    xf = x.reshape(batch * seq, d)
    xb = xf.astype(BF16)
    w_groups = _regroup_w_in(w_in)
    for l in range(w_in.shape[0]):
        xf, xb = _layer(xf, xb, batch, seq, consts, w_groups, l, nsa_cmp_w1[l], nsa_cmp_w2[l],
                        nsa_cmp_pe[l], mlstm_i_bias[l], mlstm_f_bias[l], mlstm_norm_g[l], lru_conv_w[l],
                        lru_conv_b[l], lru_gate_w[l], lru_gate_b[l], lru_lambda[l], w_out[l], ln_g[l], ln_b[l])
    return xf.reshape(batch, seq, d)
```

```python
import functools
import math

import jax
import jax.numpy as jnp
from jax import lax
from jax.experimental import pallas as pl
from jax.experimental.pallas import tpu as pltpu

F32 = jnp.float32
BF16 = jnp.bfloat16

D_MODEL = 2048
DEPTH = 2
GROUP = 512
DH = 128
NH = 4
ROT_DIM = 32
ROPE_THETA = 500000.0

CMP_BLOCK = 32
CMP_STRIDE = 16
SEL_BLOCK = 64
SEL_TOPK = 8
WIN = 256
FORCE_SCORE = 1e9

MLSTM_CHUNK = 512
LRU_C = 8.0
MOBA_BLOCK = 256
MOBA_TOPK = 3

DEEPNORM_ALPHA = (2 * DEPTH) ** 0.25
NEG = -1e30
LN_EPS = 1e-5
SCALE = DH ** -0.5
EXP2_SCALE = SCALE * math.log2(math.e)

LANES = 128
SUB = 8
VMEM_LIMIT = 56 * 1024 * 1024

_OFF = {}
_o = 0
for _name, _w in (('nsa_q', 512), ('nsa_kv', 768), ('nsa_gate', 12), ('nsa_z', 512), ('mlstm_qkv', 1536),
                  ('mlstm_if', 8), ('mlstm_o', 512), ('mlstm_z', 512), ('lru_x', 512), ('lru_z', 512),
                  ('moba_qkv', 1536), ('moba_z', 512)):
    _OFF[_name] = _o
    _o += _w

GD_I = 16
GD_F = 20
GC0 = 5
GD_COL = 7 * 128

NT = (((1,), (1,)), ((), ()))
TN = (((0,), (0,)), ((), ()))


def _cparams(sem):
    return pltpu.CompilerParams(dimension_semantics=sem, vmem_limit_bytes=VMEM_LIMIT)


def _dot(a, b):
    return jnp.dot(a, b, preferred_element_type=F32)


def _dot_nt(a, b):
    return lax.dot_general(a, b, NT, preferred_element_type=F32)


def _split_hi_lo(a):
    hi = a.astype(BF16)
    lo = (a - hi.astype(F32)).astype(BF16)
    return hi, lo


def _sigmoid(x):
    return 0.5 * jnp.tanh(0.5 * x) + 0.5


def _silu(x):
    return x * _sigmoid(x)


def _log_sigmoid(x):
    return jnp.minimum(x, 0.0) - jnp.log(1.0 + jnp.exp(-jnp.abs(x)))


def _proj_kernel(x_ref, w_ref, cos_ref, sin_ref, swap_ref, ga_ref, gbc_ref, gd_ref, gdt_ref, *, n_rot):
    j = pl.program_id(1)
    acc = _dot_nt(x_ref[...], w_ref[...])
    gbc_ref[...] = acc.astype(gbc_ref.dtype)

    @pl.when(j < n_rot)
    def _():
        c = cos_ref[...]
        s = sin_ref[...]
        for h in range(acc.shape[1] // DH):
            t = acc[:, h * DH:(h + 1) * DH]
            swapped = _dot(t.astype(BF16), swap_ref[...])
            ga_ref[:, h * DH:(h + 1) * DH] = (t * c + swapped * s).astype(ga_ref.dtype)

    @pl.when(j == GD_COL // acc.shape[1])
    def _():
        gd = acc[:, GD_COL % acc.shape[1]:GD_COL % acc.shape[1] + LANES]
        gd_ref[...] = gd
        gdt_ref[...] = jnp.transpose(gd)


def _project(xb, w_all, layer, rope, seq, tm, tn):
    m, k = xb.shape
    widths = [sum(p[2] for p in pieces) for pieces in _GROUPS]
    assert tn == GROUP and widths[1] == GC0 * GROUP
    n_rot = widths[0] // tn
    nrep = seq // tm
    rope_spec = pl.BlockSpec((tm, DH), lambda i, j: (i % nrep, 0))
    cos, sin, swap = rope
    return pl.pallas_call(
        functools.partial(_proj_kernel, n_rot=n_rot),
        out_shape=(jax.ShapeDtypeStruct((m, widths[0]), BF16), jax.ShapeDtypeStruct((m, widths[1] + widths[2]), BF16),
                   jax.ShapeDtypeStruct((m, LANES), F32), jax.ShapeDtypeStruct((LANES, m), F32)),
        grid=(m // tm, sum(widths) // tn),
        in_specs=[pl.BlockSpec((tm, k), lambda i, j: (i, 0)), pl.BlockSpec((None, tn, k), lambda i, j: (layer, j, 0)),
                  rope_spec, rope_spec, pl.BlockSpec((DH, DH), lambda i, j: (0, 0))],
        out_specs=(pl.BlockSpec((tm, tn), lambda i, j: (i, jnp.minimum(j, n_rot - 1))),
                   pl.BlockSpec((tm, tn), lambda i, j: (i, jnp.maximum(j - n_rot, 0))),
                   pl.BlockSpec((tm, LANES), lambda i, j: (i, 0)), pl.BlockSpec((LANES, tm), lambda i, j: (0, i))),
        compiler_params=_cparams(("parallel", "arbitrary")),
        name="in_proj",
    )(xb, w_all, cos, sin, swap)


def _compress_kernel(tk_ref, tv_ref, w1_ref, w2_ref, pe_ref, kc_ref, vc_ref, *, nb):
    half = (CMP_BLOCK // 2) * DH
    for idx, (t_ref, o_ref) in enumerate(((tk_ref, kc_ref), (tv_ref, vc_ref))):
        t = t_ref[...].astype(F32)
        lo = (t + pe_ref[idx, 0:1, :]).astype(BF16)
        hi = (t + pe_ref[idx, 1:2, :]).astype(BF16)
        a = _dot(lo, w1_ref[idx, 0:half, :])
        b = _dot(hi, w1_ref[idx, half:2 * half, :])
        hid = _silu(a + pltpu.roll(b, nb - 1, 0))
        out = _dot(hid.astype(BF16), w2_ref[idx])
        o_ref[...] = jnp.zeros(o_ref.shape, o_ref.dtype)
        o_ref[0:nb, :] = out.astype(o_ref.dtype)


def _compress(tk, tv, w1, w2, pe):
    b, nb, width = tk.shape
    blk = pl.BlockSpec((None, nb, width), lambda i: (i, 0, 0))
    full = lambda a: pl.BlockSpec(a.shape, lambda i: (0,) * a.ndim)
    out = jax.ShapeDtypeStruct((b, LANES, DH), BF16)
    return pl.pallas_call(
        functools.partial(_compress_kernel, nb=nb),
        out_shape=(out, out),
        grid=(b,),
        in_specs=[blk, blk, full(w1), full(w2), full(pe)],
        out_specs=(pl.BlockSpec((None, LANES, DH), lambda i: (i, 0, 0)),) * 2,
        compiler_params=_cparams(("parallel",)),
        name="nsa_compress",
    )(tk, tv, w1, w2, pe)


def _topk_rows(score, k_top):
    rowi = lax.broadcasted_iota(jnp.int32, score.shape, 0)
    rank = jnp.zeros(score.shape, F32)
    for k in range(score.shape[0]):
        sk = score[k:k + 1, :]
        beats = jnp.logical_or(sk > score, jnp.logical_and(sk == score, rowi > k))
        rank = rank + jnp.where(beats, 1.0, 0.0)
    return rank < k_top


def _prescale(q):
    return (q.astype(F32) * EXP2_SCALE).astype(q.dtype)


def _softmax_rows(s):
    p = jnp.exp2(s - jnp.max(s, axis=0, keepdims=True))
    return p.astype(BF16), 1.0 / jnp.sum(p, axis=0, keepdims=True)


def _nsa_tile(i, q_ref, kc_ref, vc_ref, ks_ref, kw_ref, gt_ref, z_ref, ovt_ref, diagb_ref, winb_ref,
              o_ref, vst_sc, vwt_sc, *, seq, tq):
    width = NH * tq
    n_cmp = seq // CMP_STRIDE - 1
    n_sel = seq // SEL_BLOCK
    n_rows = 32
    q = q_ref[...]
    qs = jnp.concatenate([q[:, h * DH:(h + 1) * DH] for h in range(NH)], axis=0)
    qs = _prescale(qs)

    rown = lax.broadcasted_iota(jnp.int32, (LANES, width), 0)
    tok = i * tq + (lax.broadcasted_iota(jnp.int32, (LANES, width), 1) & (tq - 1))
    mask_c = jnp.logical_and(rown * CMP_STRIDE + (CMP_BLOCK - 1) <= tok, rown < n_cmp)
    s = jnp.where(mask_c, _dot_nt(kc_ref[...], qs), NEG)
    e = jnp.exp2(s - jnp.max(s, axis=0, keepdims=True))
    p_c = jnp.where(mask_c, e, 0.0) * (1.0 / jnp.sum(e, axis=0, keepdims=True))
    o_c = lax.dot_general(vc_ref[...], p_c.astype(BF16), TN, preferred_element_type=F32)

    p_sum = p_c[:, 0:tq]
    for h in range(1, NH):
        p_sum = p_sum + p_c[:, h * tq:(h + 1) * tq]
    p_hi, p_lo = _split_hi_lo(p_sum)
    imp = (_dot(ovt_ref[...], p_hi) + _dot(ovt_ref[...], p_lo))[0:n_rows]
    rowj = lax.broadcasted_iota(jnp.int32, (n_rows, tq), 0)
    t_q = i * tq + lax.broadcasted_iota(jnp.int32, (n_rows, tq), 1)
    cur = t_q >> 6
    forced = jnp.logical_or(rowj == 0, jnp.logical_or(rowj == cur, rowj == cur - 1))
    valid = rowj * SEL_BLOCK <= t_q
    score = jnp.where(forced, FORCE_SCORE, jnp.where(valid, imp, NEG))
    score = jnp.where(rowj < n_sel, score, -jnp.inf)
    bias = jnp.where(_topk_rows(score, min(SEL_TOPK, n_sel)), 0.0, NEG)
    bias = jnp.concatenate([bias] * NH, axis=1)

    nk = (i + 1) * tq
    s = _dot_nt(ks_ref[0:nk, :], qs)
    parts = [s[j * SEL_BLOCK:(j + 1) * SEL_BLOCK] + bias[j:j + 1] for j in range(nk // SEL_BLOCK)]
    own = jnp.concatenate(parts[i * tq // SEL_BLOCK:], axis=0) + diagb_ref[...]
    p, inv = _softmax_rows(jnp.concatenate(parts[:i * tq // SEL_BLOCK] + [own], axis=0))
    o_s = _dot(vst_sc[:, 0:nk], p) * inv

    n_wblk = winb_ref.shape[0]
    blocks = [b for b in range(i - n_wblk + 1, i + 1) if b >= 0]
    parts = [_dot_nt(kw_ref[b * tq:(b + 1) * tq, :], qs) + winb_ref[b - i + n_wblk - 1] for b in blocks]
    p, inv = _softmax_rows(jnp.concatenate(parts, axis=0))
    o_w = _dot(vwt_sc[:, blocks[0] * tq:nk], p) * inv

    g = _sigmoid(gt_ref[0:16, :])
    for h in range(NH):
        ls = slice(h * tq, (h + 1) * tq)
        cs = slice(h * DH, (h + 1) * DH)
        mix = (g[3 * h:3 * h + 1] * o_c[:, ls] + g[3 * h + 1:3 * h + 2] * o_s[:, ls]
               + g[3 * h + 2:3 * h + 3] * o_w[:, ls])
        o_ref[:, cs] = (jnp.transpose(mix) * _silu(z_ref[:, cs].astype(F32))).astype(o_ref.dtype)


def _nsa_kernel(q_ref, kc_ref, vc_ref, ks_ref, vs_ref, kw_ref, vw_ref, gt_ref, z_ref, ovt_ref, diagb_ref,
                winb_ref, o_ref, vst_sc, vwt_sc, *, seq, tq):
    i = pl.program_id(1)

    @pl.when(i == 0)
    def _():
        for c in range(seq // tq):
            cols = slice(c * tq, (c + 1) * tq)
            vst_sc[:, cols] = jnp.transpose(vs_ref[cols, :].astype(F32)).astype(BF16)
            vwt_sc[:, cols] = jnp.transpose(vw_ref[cols, :].astype(F32)).astype(BF16)

    for c in range(seq // tq):
        pl.when(i == c)(functools.partial(
            _nsa_tile, c, q_ref, kc_ref, vc_ref, ks_ref, kw_ref, gt_ref, z_ref, ovt_ref, diagb_ref, winb_ref,
            o_ref, vst_sc, vwt_sc, seq=seq, tq=tq))


def _nsa_bias_tables(tq):
    t = jnp.arange(NH * tq)[None, :] % tq
    r = jnp.arange(tq)[:, None]
    diag = jnp.where(r <= t, 0.0, NEG)
    n_wblk = WIN // tq + 1
    win = []
    for b in range(n_wblk):
        diff = t - (r + (b - (n_wblk - 1)) * tq)
        win.append(jnp.where((diff >= 0) & (diff < WIN), 0.0, NEG))
    return diag.astype(F32), jnp.stack(win).astype(F32)


def _nsa(ga, gb, gc, gd_t, kc, vc, ov_t, batch, seq):
    tq = 256
    nq = seq // tq
    assert seq // SEL_BLOCK <= 32 and tq % SEL_BLOCK == 0
    row = lambda b, i: (b * nq + i, 0)
    kv = lambda col: pl.BlockSpec((seq, DH), lambda b, i: (b, col))
    cmp_spec = pl.BlockSpec((None, LANES, DH), lambda b, i: (b, 0, 0))
    full = lambda a: pl.BlockSpec(a.shape, lambda b, i: (0,) * a.ndim)
    diag_b, win_b = _nsa_bias_tables(tq)
    return pl.pallas_call(
        functools.partial(_nsa_kernel, seq=seq, tq=tq),
        out_shape=jax.ShapeDtypeStruct((batch * seq, GROUP), BF16),
        grid=(batch, nq),
        in_specs=[pl.BlockSpec((tq, GROUP), row), cmp_spec, cmp_spec,
                  kv(5), kv(1), kv(6), kv(2),
                  pl.BlockSpec((LANES, tq), lambda b, i: (0, b * nq + i)),
                  pl.BlockSpec((tq, GROUP), lambda b, i: (b * nq + i, GC0)), full(ov_t), full(diag_b), full(win_b)],
        out_specs=pl.BlockSpec((tq, GROUP), row),
        scratch_shapes=[pltpu.VMEM((DH, seq), BF16), pltpu.VMEM((DH, seq), BF16)],
        compiler_params=_cparams(("parallel", "arbitrary")),
        name="nsa_attention",
    )(ga, kc, vc, ga, gb, ga, gb, gd_t, gc, ov_t, diag_b, win_b)


def _mlstm_kernel(q_ref, k_ref, v_ref, gcol_ref, grow_ref, bcol_ref, brow_ref, og_ref, z_ref, ng_ref,
                  tril_ref, triu_ref, out_ref, cn_sc, m_sc, *, tc):
    L = MLSTM_CHUNK

    @pl.when(pl.program_id(1) == 0)
    def _():
        cn_sc[...] = jnp.zeros(cn_sc.shape, F32)
        m_sc[...] = jnp.full(m_sc.shape, NEG, F32)

    gcol = gcol_ref[...] + bcol_ref[...]
    grow = grow_ref[...] + brow_ref[...]
    hi, lo = _split_hi_lo(_log_sigmoid(gcol))
    bcol_all = _dot(tril_ref[...], hi) + _dot(tril_ref[...], lo)
    hi, lo = _split_hi_lo(_log_sigmoid(grow))
    brow_all = _dot(hi, triu_ref[...]) + _dot(lo, triu_ref[...])

    causal = (lax.broadcasted_iota(jnp.int32, (L, L), 1) <= lax.broadcasted_iota(jnp.int32, (L, L), 0))
    ones = jnp.ones((L, DH), BF16)
    mean_mat = jnp.full((DH, DH), 1.0 / DH, BF16)

    def lane_mean(a):
        a_hi, a_lo = _split_hi_lo(a)
        return _dot(a_hi, mean_mat) + _dot(a_lo, mean_mat)

    for c in range(tc // L):
        rs = slice(c * L, (c + 1) * L)
        for h in range(NH):
            cs = slice(h * DH, (h + 1) * DH)
            qh = q_ref[rs, cs]
            kh = k_ref[rs, cs]
            v_aug = jnp.concatenate([v_ref[rs, cs], ones], axis=1)
            ig_col = gcol[rs, GD_I + h:GD_I + h + 1]
            b_col = bcol_all[rs, GD_F + h:GD_F + h + 1]
            ig_row = grow[h:h + 1, rs]
            b_row = brow_all[NH + h:NH + h + 1, rs]
            cn_prev = cn_sc[h]
            m_prev = m_sc[h][:, 0:1]

            d_log = jnp.where(causal, b_col + (ig_row - b_row), NEG)
            m_intra = jnp.max(d_log, axis=-1, keepdims=True)
            m_inter = b_col + m_prev
            m_t = jnp.maximum(m_inter, m_intra)
            w_inter = jnp.exp(m_inter - m_t)
            qk = _dot_nt(qh, kh) * jnp.exp(d_log - (m_t - math.log(SCALE)))
            intra = _dot(qk.astype(BF16), v_aug)
            inter = _dot(qh, cn_prev.astype(BF16))
            num = intra[:, 0:DH] + w_inter * inter[:, 0:DH]
            den = intra[:, DH:2 * DH] + w_inter * inter[:, DH:2 * DH]
            hh = num / jnp.maximum(jnp.abs(den), jnp.exp(-m_t))
            hh = _sigmoid(og_ref[rs, cs].astype(F32)) * hh
            dlt = hh - lane_mean(hh)
            hn = dlt * lax.rsqrt(lane_mean(jnp.square(dlt)) + LN_EPS) * ng_ref[:, cs]
            out_ref[rs, cs] = (hn * _silu(z_ref[rs, cs].astype(F32))).astype(out_ref.dtype)

            b_last = b_row[:, L - 1:L]
            m_loc = jnp.max(b_last - b_row + ig_row, axis=-1, keepdims=True)
            e_col = jnp.exp(b_last - b_col + ig_col - m_loc)
            ek = ((e_col * SCALE) * kh.astype(F32)).astype(BF16)
            g_cn = lax.dot_general(ek, v_aug, TN, preferred_element_type=F32)
            m_new = jnp.maximum(b_last + m_prev, m_loc)
            cn_sc[h] = jnp.exp(b_last + m_prev - m_new) * cn_prev + jnp.exp(m_loc - m_new) * g_cn
            m_sc[h] = jnp.broadcast_to(m_new, (1, LANES))


def _mlstm(gb, gc, gd, gd_t, bias_col, bias_row, norm_g, tril, triu, batch, seq):
    tc = MLSTM_CHUNK
    nt = seq // tc
    row = lambda col: (lambda b, j: (b * nt + j, col))
    full = lambda a: pl.BlockSpec(a.shape, lambda b, j: (0,) * a.ndim)
    return pl.pallas_call(
        functools.partial(_mlstm_kernel, tc=tc),
        out_shape=jax.ShapeDtypeStruct((batch * seq, GROUP), BF16),
        grid=(batch, nt),
        in_specs=[pl.BlockSpec((tc, GROUP), row(1)), pl.BlockSpec((tc, GROUP), row(2)),
                  pl.BlockSpec((tc, GROUP), row(3)),
                  pl.BlockSpec((tc, LANES), row(0)),
                  pl.BlockSpec((2 * NH, tc), lambda b, j: (GD_I // (2 * NH), b * nt + j)),
                  full(bias_col), full(bias_row),
                  pl.BlockSpec((tc, GROUP), row(GC0 + 1)), pl.BlockSpec((tc, GROUP), row(GC0 + 2)),
                  full(norm_g), full(tril), full(triu)],
        out_specs=pl.BlockSpec((tc, GROUP), row(0)),
        scratch_shapes=[pltpu.VMEM((NH, DH, 2 * DH), F32), pltpu.VMEM((NH, 1, LANES), F32)],
        compiler_params=_cparams(("parallel", "arbitrary")),
        name="mlstm",
    )(gb, gb, gb, gd, gd_t, bias_col, bias_row, gc, gc, norm_g, tril, triu)


def _lru_kernel(x_ref, z_ref, cw_ref, cb_ref, gw_ref, gb_ref, lam_ref, out_ref, xbuf, h_sc, *, tl):
    pad = 8

    @pl.when(pl.program_id(1) == 0)
    def _():
        xbuf[0:pad, :] = jnp.zeros((pad, GROUP), F32)
        h_sc[...] = jnp.zeros(h_sc.shape, F32)

    x = x_ref[...].astype(F32)
    xbuf[pad:pad + tl, :] = x
    u = cw_ref[3:4, :] * x + cb_ref[...]
    for w in range(3):
        u = u + cw_ref[w:w + 1, :] * xbuf[pad - 3 + w:pad - 3 + w + tl, :]
    xbuf[0:pad, :] = x[tl - pad:tl, :]

    ub = u.astype(BF16)
    pre = []
    for gi in range(2):
        pre.append(jnp.concatenate(
            [_dot(ub[:, n * DH:(n + 1) * DH], gw_ref[gi, n]) for n in range(NH)], axis=1) + gb_ref[gi:gi + 1, :])
    r = _sigmoid(pre[0])
    ig = _sigmoid(pre[1])
    neg_lam = -lam_ref[...]
    softplus = jnp.maximum(neg_lam, 0.0) + jnp.log(1.0 + jnp.exp(-jnp.abs(neg_lam)))
    a = jnp.exp(-LRU_C * r * softplus)
    bx = jnp.sqrt(1.0 - a * a) * (ig * u)

    sub = lax.broadcasted_iota(jnp.int32, (tl, GROUP), 0) & (SUB - 1)
    d = 1
    while d < SUB:
        keep = sub >= d
        a_sh = jnp.where(keep, pltpu.roll(a, d, 0), 1.0)
        b_sh = jnp.where(keep, pltpu.roll(bx, d, 0), 0.0)
        bx = a * b_sh + bx
        a = a * a_sh
        d *= 2
    carry = jnp.broadcast_to(h_sc[...], (SUB, GROUP))
    groups = []
    for g in range(tl // SUB):
        hg = a[g * SUB:(g + 1) * SUB] * carry + bx[g * SUB:(g + 1) * SUB]
        groups.append(hg)
        carry = jnp.broadcast_to(hg[SUB - 1:SUB], (SUB, GROUP))
    hseq = jnp.concatenate(groups, axis=0)
    h_sc[...] = hseq[tl - 1:tl, :]
    out_ref[...] = (hseq * _silu(z_ref[...].astype(F32))).astype(out_ref.dtype)


def _lru(gc, conv_w, conv_b, gate_w, gate_b, lam, batch, seq):
    tl = 512
    nt = seq // tl
    row = lambda col: (lambda b, j: (b * nt + j, col))
    full = lambda a: pl.BlockSpec(a.shape, lambda b, j: (0,) * a.ndim)
    return pl.pallas_call(
        functools.partial(_lru_kernel, tl=tl),
        out_shape=jax.ShapeDtypeStruct((batch * seq, GROUP), BF16),
        grid=(batch, nt),
        in_specs=[pl.BlockSpec((tl, GROUP), row(GC0 + 3)), pl.BlockSpec((tl, GROUP), row(GC0 + 4)),
                  full(conv_w), full(conv_b), full(gate_w), full(gate_b), full(lam)],
        out_specs=pl.BlockSpec((tl, GROUP), row(0)),
        scratch_shapes=[pltpu.VMEM((tl + 8, GROUP), F32), pltpu.VMEM((1, GROUP), F32)],
        compiler_params=_cparams(("parallel", "arbitrary")),
        name="rglru",
    )(gc, gc, conv_w, conv_b, gate_w, gate_b, lam)


def _moba_tile(i, q_ref, k_ref, z_ref, o_ref, kmean_sc, vt_sc, *, seq):
    tq = MOBA_BLOCK
    nb = seq // MOBA_BLOCK
    ncand = 8
    nk = (i + 1) * tq
    past = lax.broadcasted_iota(jnp.int32, (ncand, tq), 0) < i
    causal = (lax.broadcasted_iota(jnp.int32, (tq, tq), 0) <= lax.broadcasted_iota(jnp.int32, (tq, tq), 1))
    for h in range(NH):
        cs = slice(h * DH, (h + 1) * DH)
        qh = q_ref[:, cs]
        s = _dot_nt(k_ref[0:nk, cs], _prescale(qh))
        parts = [jnp.where(causal, s[i * tq:nk], NEG)]
        if i > 0:
            gate = _dot_nt(kmean_sc[h].astype(BF16), qh)[0:ncand]
            score = jnp.where(past, gate, NEG)
            sel = jnp.logical_and(_topk_rows(score, min(MOBA_TOPK, nb)), past)
            bias = jnp.where(sel, 0.0, NEG)
            parts = [s[j * tq:(j + 1) * tq] + bias[j:j + 1] for j in range(i)] + parts
        p, inv = _softmax_rows(jnp.concatenate(parts, axis=0))
        o = jnp.transpose(_dot(vt_sc[h, :, 0:nk], p) * inv)
        o_ref[:, cs] = (o * _silu(z_ref[:, cs].astype(F32))).astype(o_ref.dtype)


def _moba_kernel(q_ref, k_ref, v_ref, z_ref, o_ref, kmean_sc, vt_sc, *, seq):
    i = pl.program_id(1)
    tq = MOBA_BLOCK
    nb = seq // MOBA_BLOCK

    @pl.when(i == 0)
    def _():
        kmean_sc[...] = jnp.zeros(kmean_sc.shape, F32)
        for h in range(NH):
            cs = slice(h * DH, (h + 1) * DH)
            kmean_sc[h, 0:nb, :] = jnp.mean(k_ref[:, cs].astype(F32).reshape(nb, MOBA_BLOCK, DH), axis=1)
            for c in range(nb):
                rows = slice(c * tq, (c + 1) * tq)
                vt_sc[h, :, rows] = jnp.transpose(v_ref[rows, cs].astype(F32)).astype(BF16)

    for c in range(nb):
        pl.when(i == c)(functools.partial(_moba_tile, c, q_ref, k_ref, z_ref, o_ref, kmean_sc, vt_sc, seq=seq))


def _moba(ga, gb, gc, batch, seq):
    tq = MOBA_BLOCK
    nq = seq // tq
    assert nq <= 8
    row = lambda col: (lambda b, i: (b * nq + i, col))
    return pl.pallas_call(
        functools.partial(_moba_kernel, seq=seq),
        out_shape=jax.ShapeDtypeStruct((batch * seq, GROUP), BF16),
        grid=(batch, nq),
        in_specs=[pl.BlockSpec((tq, GROUP), row(2)),
                  pl.BlockSpec((seq, GROUP), lambda b, i: (b, 3)),
                  pl.BlockSpec((seq, GROUP), lambda b, i: (b, 4)),
                  pl.BlockSpec((tq, GROUP), row(GC0 + 5))],
        out_specs=pl.BlockSpec((tq, GROUP), row(0)),
        scratch_shapes=[pltpu.VMEM((NH, LANES, DH), F32), pltpu.VMEM((NH, DH, seq), BF16)],
        compiler_params=_cparams(("parallel", "arbitrary")),
        name="moba",
    )(ga, ga, gb, gc)


def _out_kernel(ya_ref, yb_ref, yc_ref, yd_ref, w_ref, x_ref, g_ref, b_ref, o_ref, ob_ref):
    acc = DEEPNORM_ALPHA * x_ref[...]
    for p, y_ref in enumerate((ya_ref, yb_ref, yc_ref, yd_ref)):
        acc = acc + _dot(y_ref[...], w_ref[p * GROUP:(p + 1) * GROUP, :])
    mu = jnp.mean(acc, axis=-1, keepdims=True)
    var = jnp.mean(jnp.square(acc - mu), axis=-1, keepdims=True)
    y = (acc - mu) * lax.rsqrt(var + LN_EPS) * g_ref[...] + b_ref[...]
    o_ref[...] = y
    ob_ref[...] = y.astype(BF16)


def _out_proj(ys, w_out, x, ln_g, ln_b):
    m, d = x.shape
    tm = 512
    yspec = pl.BlockSpec((tm, GROUP), lambda i: (i, 0))
    full = lambda a: pl.BlockSpec(a.shape, lambda i: (0,) * a.ndim)
    xspec = pl.BlockSpec((tm, d), lambda i: (i, 0))
    return pl.pallas_call(
        _out_kernel,
        out_shape=(jax.ShapeDtypeStruct((m, d), F32), jax.ShapeDtypeStruct((m, d), BF16)),
        grid=(m // tm,),
        in_specs=[yspec] * 4 + [full(w_out), xspec, full(ln_g), full(ln_b)],
        out_specs=(xspec, xspec),
        compiler_params=_cparams(("parallel",)),
        name="out_proj_ln",
    )(*ys, w_out, x, ln_g, ln_b)


_GROUPS = (
    (('nsa_q', 0, 512), ('nsa_kv', 0, 384), ('gates', 0, LANES), ('moba_qkv', 0, 1024)),
    (('nsa_kv', 384, 384), (None, 0, 128), ('mlstm_qkv', 0, 1536), ('moba_qkv', 1024, 512)),
    (('nsa_z', 0, 512), ('mlstm_o', 0, 512), ('mlstm_z', 0, 512), ('lru_x', 0, 512), ('lru_z', 0, 512),
     ('moba_z', 0, 512)),
)
_GATE_PIECES = (('nsa_gate', 0, 12), (None, 0, 4), ('mlstm_if', 0, 8), (None, 0, LANES - 24))


def _copy_rows(w_ref, pieces, o_ref, dst):
    for name, off, width in pieces:
        if name is None:
            o_ref[dst:dst + width, :] = jnp.zeros((width, o_ref.shape[1]), o_ref.dtype)
        else:
            src = _OFF[name] + off
            o_ref[dst:dst + width, :] = w_ref[src:src + width, :].astype(o_ref.dtype)
        dst += width


def _regroup_kernel(w_ref, o_ref, gate_sc):
    dst = 0
    for piece in sum(_GROUPS, ()):
        if piece[0] == 'gates':
            _copy_rows(w_ref, _GATE_PIECES, gate_sc, 0)
            o_ref[dst:dst + LANES, :] = gate_sc[...].astype(o_ref.dtype)
        else:
            _copy_rows(w_ref, (piece,), o_ref, dst)
        dst += piece[2]


def _regroup_w_in(w_in):
    depth, d, n_in = w_in.shape
    wt = jnp.transpose(w_in, (0, 2, 1))
    tk = 256
    n_all = sum(p[2] for pieces in _GROUPS for p in pieces)
    return pl.pallas_call(
        _regroup_kernel,
        out_shape=jax.ShapeDtypeStruct((depth, n_all, d), BF16),
        grid=(depth, d // tk),
        in_specs=[pl.BlockSpec((None, n_in, tk), lambda l, i: (l, 0, i))],
        out_specs=pl.BlockSpec((None, n_all, tk), lambda l, i: (l, 0, i)),
        scratch_shapes=[pltpu.VMEM((LANES, tk), F32)],
        compiler_params=_cparams(("parallel", "parallel")),
        name="regroup_w_in",
    )(wt)


def _rope_tables(seq):
    half = ROT_DIM // 2
    inv_freq = jnp.power(ROPE_THETA, -jnp.arange(half, dtype=F32) * (2.0 / ROT_DIM))
    ang = jnp.arange(seq, dtype=jnp.int32).astype(F32)[:, None] * inv_freq[None, :]
    cos, sin = jnp.cos(ang), jnp.sin(ang)
    ones = jnp.ones((seq, DH - ROT_DIM), F32)
    lane = jnp.arange(DH)
    partner = jnp.where(lane < half, lane + half, lane - half)
    swap = (lane[:, None] == partner[None, :]).astype(BF16)
    return (jnp.concatenate([cos, cos, ones], axis=1),
            jnp.concatenate([-sin, sin, 0.0 * ones], axis=1), swap)


def _cmp_overlap_t(seq):
    n_cmp = seq // CMP_STRIDE - 1
    n_sel = seq // SEL_BLOCK
    cs = jnp.arange(LANES)[None, :] * CMP_STRIDE
    ss = jnp.arange(LANES)[:, None] * SEL_BLOCK
    ov = (cs < ss + SEL_BLOCK) & (cs + CMP_BLOCK > ss)
    ov = ov & (jnp.arange(LANES)[None, :] < n_cmp) & (jnp.arange(LANES)[:, None] < n_sel)
    return ov.astype(BF16)


def _chunk_tri(tc):
    r = jnp.arange(tc)
    same = (r[:, None] // MLSTM_CHUNK) == (r[None, :] // MLSTM_CHUNK)
    tril = (same & (r[None, :] <= r[:, None])).astype(BF16)
    return tril, tril.T


def _layer(x, xb, batch, seq, consts, w_all, layer, cmp_w1, cmp_w2, cmp_pe, i_bias, f_bias, norm_g,
           conv_w, conv_b, gate_w, gate_b, lam, w_out, ln_g, ln_b):
    rope, ov_t, tril, triu = consts
    ga, gbc, gd, gd_t = _project(xb, w_all, layer, rope, seq, min(2048, seq), GROUP)
    gb = gc = gbc

    nb = seq // CMP_STRIDE
    tk = ga[:, 4 * DH:5 * DH].reshape(batch, nb, CMP_STRIDE * DH)
    tv = gb[:, 0:DH].reshape(batch, nb, CMP_STRIDE * DH)
    pe = cmp_pe.reshape(2, 2, CMP_STRIDE * DH)
    kc, vc = _compress(tk, tv, cmp_w1.astype(BF16), cmp_w2.astype(BF16), pe)
    y_a = _nsa(ga, gb, gc, gd_t, kc, vc, ov_t, batch, seq)

    bias = jnp.concatenate([i_bias, f_bias])
    bias_col = jnp.zeros((1, LANES), F32).at[0, GD_I:GD_I + 2 * NH].set(bias)
    bias_row = bias[:, None]
    y_b = _mlstm(gb, gc, gd, gd_t, bias_col, bias_row, norm_g[None, :], tril, triu, batch, seq)

    y_c = _lru(gc, conv_w, conv_b[None, :], gate_w.astype(BF16), gate_b, lam[None, :], batch, seq)

    y_d = _moba(ga, gb, gc, batch, seq)

    return _out_proj((y_a, y_b, y_c, y_d), w_out.astype(BF16), x, ln_g[None, :], ln_b[None, :])


def kernel(x, w_in, nsa_cmp_w1, nsa_cmp_w2, nsa_cmp_pe, mlstm_i_bias, mlstm_f_bias, mlstm_norm_g,
           lru_conv_w, lru_conv_b, lru_gate_w, lru_gate_b, lru_lambda, w_out, ln_g, ln_b):
    batch, seq, d = x.shape
    tril, triu = _chunk_tri(MLSTM_CHUNK)
    consts = (_rope_tables(seq), _cmp_overlap_t(seq), tril, triu)
    xf = x.reshape(batch * seq, d)
    xb = xf.astype(BF16)
    w_all = _regroup_w_in(w_in)
    for l in range(w_in.shape[0]):
        xf, xb = _layer(xf, xb, batch, seq, consts, w_all, l, nsa_cmp_w1[l], nsa_cmp_w2[l],
                        nsa_cmp_pe[l], mlstm_i_bias[l], mlstm_f_bias[l], mlstm_norm_g[l], lru_conv_w[l],
                        lru_conv_b[l], lru_gate_w[l], lru_gate_b[l], lru_lambda[l], w_out[l], ln_g[l], ln_b[l])
    return xf.reshape(batch, seq, d)
```

```python
import functools
import math

import jax
import jax.numpy as jnp
from jax import lax
from jax.experimental import pallas as pl
from jax.experimental.pallas import tpu as pltpu

F32 = jnp.float32
BF16 = jnp.bfloat16

DEPTH = 2
GROUP = 512
DH = 128
NH = 4
ROT_DIM = 32
ROPE_THETA = 500000.0

CMP_BLOCK = 32
CMP_STRIDE = 16
SEL_BLOCK = 64
SEL_TOPK = 8
WIN = 256
FORCE_SCORE = 1e9

MLSTM_CHUNK = 512
LRU_C = 8.0
MOBA_BLOCK = 256
MOBA_TOPK = 3

DEEPNORM_ALPHA = (2 * DEPTH) ** 0.25
NEG = -1e30
LN_EPS = 1e-5
SCALE = DH ** -0.5
EXP2_SCALE = SCALE * math.log2(math.e)

LANES = 128
SUB = 8
VMEM_LIMIT = 56 * 1024 * 1024

PROJ_ROWS = 2048
PROJ_COLS = 512
REGROUP_COLS = 256
NSA_ROWS = 256
NSA_SEL_ROWS = 32
MOBA_CAND_ROWS = SUB
LRU_ROWS = 512
OUT_ROWS = 512

_OFF = {}
_o = 0
for _name, _w in (('nsa_q', 512), ('nsa_kv', 768), ('nsa_gate', 12), ('nsa_z', 512), ('mlstm_qkv', 1536),
                  ('mlstm_if', 8), ('mlstm_o', 512), ('mlstm_z', 512), ('lru_x', 512), ('lru_z', 512),
                  ('moba_qkv', 1536), ('moba_z', 512)):
    _OFF[_name] = _o
    _o += _w

GD_I = 16
GD_F = 20
GC0 = 5
GD_COL = 7 * 128

NT = (((1,), (1,)), ((), ()))
TN = (((0,), (0,)), ((), ()))


def _cparams(sem):
    return pltpu.CompilerParams(dimension_semantics=sem, vmem_limit_bytes=VMEM_LIMIT)


def _dot(a, b):
    return jnp.dot(a, b, preferred_element_type=F32)


def _dot_nt(a, b):
    return lax.dot_general(a, b, NT, preferred_element_type=F32)


def _split_hi_lo(a):
    hi = a.astype(BF16)
    lo = (a - hi.astype(F32)).astype(BF16)
    return hi, lo


def _sigmoid(x):
    return 0.5 * jnp.tanh(0.5 * x) + 0.5


def _silu(x):
    return x * _sigmoid(x)


def _log_sigmoid(x):
    return jnp.minimum(x, 0.0) - jnp.log(1.0 + jnp.exp(-jnp.abs(x)))


def _proj_kernel(x_ref, w_ref, cos_ref, sin_ref, swap_ref, ga_ref, gbc_ref, gd_ref, gdt_ref, *, n_rot):
    j = pl.program_id(1)
    acc = _dot_nt(x_ref[...], w_ref[...])
    gbc_ref[...] = acc.astype(gbc_ref.dtype)

    @pl.when(j < n_rot)
    def _():
        c = cos_ref[...]
        s = sin_ref[...]
        for h in range(acc.shape[1] // DH):
            t = acc[:, h * DH:(h + 1) * DH]
            swapped = _dot(t.astype(BF16), swap_ref[...])
            ga_ref[:, h * DH:(h + 1) * DH] = (t * c + swapped * s).astype(ga_ref.dtype)

    @pl.when(j == GD_COL // acc.shape[1])
    def _():
        gd = acc[:, GD_COL % acc.shape[1]:GD_COL % acc.shape[1] + LANES]
        gd_ref[...] = gd
        gdt_ref[...] = jnp.transpose(gd)


def _project(xb, w_all, layer, rope, seq, tm, tn):
    m, k = xb.shape
    widths = [sum(p[2] for p in pieces) for pieces in _GROUPS]
    assert tn == GROUP and widths[1] == GC0 * GROUP
    n_rot = widths[0] // tn
    nrep = seq // tm
    rope_spec = pl.BlockSpec((tm, DH), lambda i, j: (i % nrep, 0))
    cos, sin, swap = rope
    return pl.pallas_call(
        functools.partial(_proj_kernel, n_rot=n_rot),
        out_shape=(jax.ShapeDtypeStruct((m, widths[0]), BF16), jax.ShapeDtypeStruct((m, widths[1] + widths[2]), BF16),
                   jax.ShapeDtypeStruct((m, LANES), F32), jax.ShapeDtypeStruct((LANES, m), F32)),
        grid=(m // tm, sum(widths) // tn),
        in_specs=[pl.BlockSpec((tm, k), lambda i, j: (i, 0)), pl.BlockSpec((None, tn, k), lambda i, j: (layer, j, 0)),
                  rope_spec, rope_spec, pl.BlockSpec((DH, DH), lambda i, j: (0, 0))],
        out_specs=(pl.BlockSpec((tm, tn), lambda i, j: (i, jnp.minimum(j, n_rot - 1))),
                   pl.BlockSpec((tm, tn), lambda i, j: (i, jnp.maximum(j - n_rot, 0))),
                   pl.BlockSpec((tm, LANES), lambda i, j: (i, 0)), pl.BlockSpec((LANES, tm), lambda i, j: (0, i))),
        compiler_params=_cparams(("parallel", "arbitrary")),
        name="in_proj",
    )(xb, w_all, cos, sin, swap)


def _compress_kernel(tk_ref, tv_ref, w1_ref, w2_ref, pe_ref, kc_ref, vc_ref, *, nb):
    half = (CMP_BLOCK // 2) * DH
    for idx, (t_ref, o_ref) in enumerate(((tk_ref, kc_ref), (tv_ref, vc_ref))):
        t = t_ref[...].astype(F32)
        lo = (t + pe_ref[idx, 0:1, :]).astype(BF16)
        hi = (t + pe_ref[idx, 1:2, :]).astype(BF16)
        a = _dot(lo, w1_ref[idx, 0:half, :])
        b = _dot(hi, w1_ref[idx, half:2 * half, :])
        hid = _silu(a + pltpu.roll(b, nb - 1, 0))
        out = _dot(hid.astype(BF16), w2_ref[idx])
        o_ref[...] = jnp.zeros(o_ref.shape, o_ref.dtype)
        o_ref[0:nb, :] = out.astype(o_ref.dtype)


def _compress(tk, tv, w1, w2, pe):
    b, nb, width = tk.shape
    blk = pl.BlockSpec((None, nb, width), lambda i: (i, 0, 0))
    full = lambda a: pl.BlockSpec(a.shape, lambda i: (0,) * a.ndim)
    out = jax.ShapeDtypeStruct((b, LANES, DH), BF16)
    return pl.pallas_call(
        functools.partial(_compress_kernel, nb=nb),
        out_shape=(out, out),
        grid=(b,),
        in_specs=[blk, blk, full(w1), full(w2), full(pe)],
        out_specs=(pl.BlockSpec((None, LANES, DH), lambda i: (i, 0, 0)),) * 2,
        compiler_params=_cparams(("parallel",)),
        name="nsa_compress",
    )(tk, tv, w1, w2, pe)


def _topk_rows(score, k_top):
    rowi = lax.broadcasted_iota(jnp.int32, score.shape, 0)
    rank = jnp.zeros(score.shape, F32)
    for k in range(score.shape[0]):
        sk = score[k:k + 1, :]
        beats = jnp.logical_or(sk > score, jnp.logical_and(sk == score, rowi > k))
        rank = rank + jnp.where(beats, 1.0, 0.0)
    return rank < k_top


def _prescale(q):
    return (q.astype(F32) * EXP2_SCALE).astype(q.dtype)


def _softmax_rows(s):
    p = jnp.exp2(s - jnp.max(s, axis=0, keepdims=True))
    return p.astype(BF16), 1.0 / jnp.sum(p, axis=0, keepdims=True)


def _nsa_tile(i, q_ref, kc_ref, vc_ref, ks_ref, kw_ref, gt_ref, z_ref, ovt_ref, diagb_ref, winb_ref,
              o_ref, vst_sc, vwt_sc, *, seq, tq):
    width = NH * tq
    n_cmp = seq // CMP_STRIDE - 1
    n_sel = seq // SEL_BLOCK
    n_rows = NSA_SEL_ROWS
    q = q_ref[...]
    qs = jnp.concatenate([q[:, h * DH:(h + 1) * DH] for h in range(NH)], axis=0)
    qs = _prescale(qs)

    rown = lax.broadcasted_iota(jnp.int32, (LANES, width), 0)
    tok = i * tq + (lax.broadcasted_iota(jnp.int32, (LANES, width), 1) & (tq - 1))
    mask_c = jnp.logical_and(rown * CMP_STRIDE + (CMP_BLOCK - 1) <= tok, rown < n_cmp)
    s = jnp.where(mask_c, _dot_nt(kc_ref[...], qs), NEG)
    e = jnp.exp2(s - jnp.max(s, axis=0, keepdims=True))
    p_c = jnp.where(mask_c, e, 0.0) * (1.0 / jnp.sum(e, axis=0, keepdims=True))
    o_c = lax.dot_general(vc_ref[...], p_c.astype(BF16), TN, preferred_element_type=F32)

    p_sum = p_c[:, 0:tq]
    for h in range(1, NH):
        p_sum = p_sum + p_c[:, h * tq:(h + 1) * tq]
    p_hi, p_lo = _split_hi_lo(p_sum)
    imp = (_dot(ovt_ref[...], p_hi) + _dot(ovt_ref[...], p_lo))[0:n_rows]
    rowj = lax.broadcasted_iota(jnp.int32, (n_rows, tq), 0)
    t_q = i * tq + lax.broadcasted_iota(jnp.int32, (n_rows, tq), 1)
    cur = t_q >> (SEL_BLOCK.bit_length() - 1)
    forced = jnp.logical_or(rowj == 0, jnp.logical_or(rowj == cur, rowj == cur - 1))
    valid = rowj * SEL_BLOCK <= t_q
    score = jnp.where(forced, FORCE_SCORE, jnp.where(valid, imp, NEG))
    score = jnp.where(rowj < n_sel, score, -jnp.inf)
    bias = jnp.where(_topk_rows(score, min(SEL_TOPK, n_sel)), 0.0, NEG)
    bias = jnp.concatenate([bias] * NH, axis=1)

    nk = (i + 1) * tq
    s = _dot_nt(ks_ref[0:nk, :], qs)
    parts = [s[j * SEL_BLOCK:(j + 1) * SEL_BLOCK] + bias[j:j + 1] for j in range(nk // SEL_BLOCK)]
    own = jnp.concatenate(parts[i * tq // SEL_BLOCK:], axis=0) + diagb_ref[...]
    p, inv = _softmax_rows(jnp.concatenate(parts[:i * tq // SEL_BLOCK] + [own], axis=0))
    o_s = _dot(vst_sc[:, 0:nk], p) * inv

    n_wblk = winb_ref.shape[0]
    blocks = [b for b in range(i - n_wblk + 1, i + 1) if b >= 0]
    parts = [_dot_nt(kw_ref[b * tq:(b + 1) * tq, :], qs) + winb_ref[b - i + n_wblk - 1] for b in blocks]
    p, inv = _softmax_rows(jnp.concatenate(parts, axis=0))
    o_w = _dot(vwt_sc[:, blocks[0] * tq:nk], p) * inv

    g = _sigmoid(gt_ref[0:GD_I, :])
    for h in range(NH):
        ls = slice(h * tq, (h + 1) * tq)
        cs = slice(h * DH, (h + 1) * DH)
        mix = (g[3 * h:3 * h + 1] * o_c[:, ls] + g[3 * h + 1:3 * h + 2] * o_s[:, ls]
               + g[3 * h + 2:3 * h + 3] * o_w[:, ls])
        o_ref[:, cs] = (jnp.transpose(mix) * _silu(z_ref[:, cs].astype(F32))).astype(o_ref.dtype)


def _nsa_kernel(q_ref, kc_ref, vc_ref, ks_ref, vs_ref, kw_ref, vw_ref, gt_ref, z_ref, ovt_ref, diagb_ref,
                winb_ref, o_ref, vst_sc, vwt_sc, *, seq, tq):
    i = pl.program_id(1)

    @pl.when(i == 0)
    def _():
        for c in range(seq // tq):
            cols = slice(c * tq, (c + 1) * tq)
            vst_sc[:, cols] = jnp.transpose(vs_ref[cols, :].astype(F32)).astype(BF16)
            vwt_sc[:, cols] = jnp.transpose(vw_ref[cols, :].astype(F32)).astype(BF16)

    for c in range(seq // tq):
        pl.when(i == c)(functools.partial(
            _nsa_tile, c, q_ref, kc_ref, vc_ref, ks_ref, kw_ref, gt_ref, z_ref, ovt_ref, diagb_ref, winb_ref,
            o_ref, vst_sc, vwt_sc, seq=seq, tq=tq))


def _nsa_bias_tables(tq):
    t = jnp.arange(NH * tq)[None, :] % tq
    r = jnp.arange(tq)[:, None]
    diag = jnp.where(r <= t, 0.0, NEG)
    n_wblk = WIN // tq + 1
    win = []
    for b in range(n_wblk):
        diff = t - (r + (b - (n_wblk - 1)) * tq)
        win.append(jnp.where((diff >= 0) & (diff < WIN), 0.0, NEG))
    return diag.astype(F32), jnp.stack(win).astype(F32)


def _nsa(ga, gb, gc, gd_t, kc, vc, ov_t, batch, seq):
    tq = NSA_ROWS
    nq = seq // tq
    assert seq // SEL_BLOCK <= NSA_SEL_ROWS and tq % SEL_BLOCK == 0
    row = lambda b, i: (b * nq + i, 0)
    kv = lambda col: pl.BlockSpec((seq, DH), lambda b, i: (b, col))
    cmp_spec = pl.BlockSpec((None, LANES, DH), lambda b, i: (b, 0, 0))
    full = lambda a: pl.BlockSpec(a.shape, lambda b, i: (0,) * a.ndim)
    diag_b, win_b = _nsa_bias_tables(tq)
    return pl.pallas_call(
        functools.partial(_nsa_kernel, seq=seq, tq=tq),
        out_shape=jax.ShapeDtypeStruct((batch * seq, GROUP), BF16),
        grid=(batch, nq),
        in_specs=[pl.BlockSpec((tq, GROUP), row), cmp_spec, cmp_spec,
                  kv(5), kv(1), kv(6), kv(2),
                  pl.BlockSpec((LANES, tq), lambda b, i: (0, b * nq + i)),
                  pl.BlockSpec((tq, GROUP), lambda b, i: (b * nq + i, GC0)), full(ov_t), full(diag_b), full(win_b)],
        out_specs=pl.BlockSpec((tq, GROUP), row),
        scratch_shapes=[pltpu.VMEM((DH, seq), BF16), pltpu.VMEM((DH, seq), BF16)],
        compiler_params=_cparams(("parallel", "arbitrary")),
        name="nsa_attention",
    )(ga, kc, vc, ga, gb, ga, gb, gd_t, gc, ov_t, diag_b, win_b)


def _mlstm_kernel(q_ref, k_ref, v_ref, gcol_ref, grow_ref, bcol_ref, brow_ref, og_ref, z_ref, ng_ref,
                  tril_ref, triu_ref, out_ref, cn_sc, m_sc, *, tc):
    L = MLSTM_CHUNK

    @pl.when(pl.program_id(1) == 0)
    def _():
        cn_sc[...] = jnp.zeros(cn_sc.shape, F32)
        m_sc[...] = jnp.full(m_sc.shape, NEG, F32)

    gcol = gcol_ref[...] + bcol_ref[...]
    grow = grow_ref[...] + brow_ref[...]
    hi, lo = _split_hi_lo(_log_sigmoid(gcol))
    bcol_all = _dot(tril_ref[...], hi) + _dot(tril_ref[...], lo)
    hi, lo = _split_hi_lo(_log_sigmoid(grow))
    brow_all = _dot(hi, triu_ref[...]) + _dot(lo, triu_ref[...])

    causal = (lax.broadcasted_iota(jnp.int32, (L, L), 1) <= lax.broadcasted_iota(jnp.int32, (L, L), 0))
    ones = jnp.ones((L, DH), BF16)
    mean_mat = jnp.full((DH, DH), 1.0 / DH, BF16)

    def lane_mean(a):
        a_hi, a_lo = _split_hi_lo(a)
        return _dot(a_hi, mean_mat) + _dot(a_lo, mean_mat)

    for c in range(tc // L):
        rs = slice(c * L, (c + 1) * L)
        for h in range(NH):
            cs = slice(h * DH, (h + 1) * DH)
            qh = q_ref[rs, cs]
            kh = k_ref[rs, cs]
            v_aug = jnp.concatenate([v_ref[rs, cs], ones], axis=1)
            ig_col = gcol[rs, GD_I + h:GD_I + h + 1]
            b_col = bcol_all[rs, GD_F + h:GD_F + h + 1]
            ig_row = grow[h:h + 1, rs]
            b_row = brow_all[NH + h:NH + h + 1, rs]
            cn_prev = cn_sc[h]
            m_prev = m_sc[h][:, 0:1]

            d_log = jnp.where(causal, b_col + (ig_row - b_row), NEG)
            m_intra = jnp.max(d_log, axis=-1, keepdims=True)
            m_inter = b_col + m_prev
            m_t = jnp.maximum(m_inter, m_intra)
            w_inter = jnp.exp(m_inter - m_t)
            qk = _dot_nt(qh, kh) * jnp.exp(d_log - (m_t - math.log(SCALE)))
            intra = _dot(qk.astype(BF16), v_aug)
            inter = _dot(qh, cn_prev.astype(BF16))
            num = intra[:, 0:DH] + w_inter * inter[:, 0:DH]
            den = intra[:, DH:2 * DH] + w_inter * inter[:, DH:2 * DH]
            hh = num / jnp.maximum(jnp.abs(den), jnp.exp(-m_t))
            hh = _sigmoid(og_ref[rs, cs].astype(F32)) * hh
            dlt = hh - lane_mean(hh)
            hn = dlt * lax.rsqrt(lane_mean(jnp.square(dlt)) + LN_EPS) * ng_ref[:, cs]
            out_ref[rs, cs] = (hn * _silu(z_ref[rs, cs].astype(F32))).astype(out_ref.dtype)

            b_last = b_row[:, L - 1:L]
            m_loc = jnp.max(b_last - b_row + ig_row, axis=-1, keepdims=True)
            e_col = jnp.exp(b_last - b_col + ig_col - m_loc)
            ek = ((e_col * SCALE) * kh.astype(F32)).astype(BF16)
            g_cn = lax.dot_general(ek, v_aug, TN, preferred_element_type=F32)
            m_new = jnp.maximum(b_last + m_prev, m_loc)
            cn_sc[h] = jnp.exp(b_last + m_prev - m_new) * cn_prev + jnp.exp(m_loc - m_new) * g_cn
            m_sc[h] = jnp.broadcast_to(m_new, (1, LANES))


def _mlstm(gb, gc, gd, gd_t, bias_col, bias_row, norm_g, tril, triu, batch, seq):
    tc = MLSTM_CHUNK
    nt = seq // tc
    row = lambda col: (lambda b, j: (b * nt + j, col))
    full = lambda a: pl.BlockSpec(a.shape, lambda b, j: (0,) * a.ndim)
    return pl.pallas_call(
        functools.partial(_mlstm_kernel, tc=tc),
        out_shape=jax.ShapeDtypeStruct((batch * seq, GROUP), BF16),
        grid=(batch, nt),
        in_specs=[pl.BlockSpec((tc, GROUP), row(1)), pl.BlockSpec((tc, GROUP), row(2)),
                  pl.BlockSpec((tc, GROUP), row(3)),
                  pl.BlockSpec((tc, LANES), row(0)),
                  pl.BlockSpec((2 * NH, tc), lambda b, j: (GD_I // (2 * NH), b * nt + j)),
                  full(bias_col), full(bias_row),
                  pl.BlockSpec((tc, GROUP), row(GC0 + 1)), pl.BlockSpec((tc, GROUP), row(GC0 + 2)),
                  full(norm_g), full(tril), full(triu)],
        out_specs=pl.BlockSpec((tc, GROUP), row(0)),
        scratch_shapes=[pltpu.VMEM((NH, DH, 2 * DH), F32), pltpu.VMEM((NH, 1, LANES), F32)],
        compiler_params=_cparams(("parallel", "arbitrary")),
        name="mlstm",
    )(gb, gb, gb, gd, gd_t, bias_col, bias_row, gc, gc, norm_g, tril, triu)


def _lru_kernel(x_ref, z_ref, cw_ref, cb_ref, gw_ref, gb_ref, lam_ref, out_ref, xbuf, h_sc, *, tl):
    pad = 8

    @pl.when(pl.program_id(1) == 0)
    def _():
        xbuf[0:pad, :] = jnp.zeros((pad, GROUP), F32)
        h_sc[...] = jnp.zeros(h_sc.shape, F32)

    x = x_ref[...].astype(F32)
    xbuf[pad:pad + tl, :] = x
    u = cw_ref[3:4, :] * x + cb_ref[...]
    for w in range(3):
        u = u + cw_ref[w:w + 1, :] * xbuf[pad - 3 + w:pad - 3 + w + tl, :]
    xbuf[0:pad, :] = x[tl - pad:tl, :]

    ub = u.astype(BF16)
    pre = []
    for gi in range(2):
        pre.append(jnp.concatenate(
            [_dot(ub[:, n * DH:(n + 1) * DH], gw_ref[gi, n]) for n in range(NH)], axis=1) + gb_ref[gi:gi + 1, :])
    r = _sigmoid(pre[0])
    ig = _sigmoid(pre[1])
    neg_lam = -lam_ref[...]
    softplus = jnp.maximum(neg_lam, 0.0) + jnp.log(1.0 + jnp.exp(-jnp.abs(neg_lam)))
    a = jnp.exp(-LRU_C * r * softplus)
    bx = jnp.sqrt(1.0 - a * a) * (ig * u)

    sub = lax.broadcasted_iota(jnp.int32, (tl, GROUP), 0) & (SUB - 1)
    d = 1
    while d < SUB:
        keep = sub >= d
        a_sh = jnp.where(keep, pltpu.roll(a, d, 0), 1.0)
        b_sh = jnp.where(keep, pltpu.roll(bx, d, 0), 0.0)
        bx = a * b_sh + bx
        a = a * a_sh
        d *= 2
    carry = jnp.broadcast_to(h_sc[...], (SUB, GROUP))
    groups = []
    for g in range(tl // SUB):
        hg = a[g * SUB:(g + 1) * SUB] * carry + bx[g * SUB:(g + 1) * SUB]
        groups.append(hg)
        carry = jnp.broadcast_to(hg[SUB - 1:SUB], (SUB, GROUP))
    hseq = jnp.concatenate(groups, axis=0)
    h_sc[...] = hseq[tl - 1:tl, :]
    out_ref[...] = (hseq * _silu(z_ref[...].astype(F32))).astype(out_ref.dtype)


def _lru(gc, conv_w, conv_b, gate_w, gate_b, lam, batch, seq):
    tl = LRU_ROWS
    nt = seq // tl
    row = lambda col: (lambda b, j: (b * nt + j, col))
    full = lambda a: pl.BlockSpec(a.shape, lambda b, j: (0,) * a.ndim)
    return pl.pallas_call(
        functools.partial(_lru_kernel, tl=tl),
        out_shape=jax.ShapeDtypeStruct((batch * seq, GROUP), BF16),
        grid=(batch, nt),
        in_specs=[pl.BlockSpec((tl, GROUP), row(GC0 + 3)), pl.BlockSpec((tl, GROUP), row(GC0 + 4)),
                  full(conv_w), full(conv_b), full(gate_w), full(gate_b), full(lam)],
        out_specs=pl.BlockSpec((tl, GROUP), row(0)),
        scratch_shapes=[pltpu.VMEM((tl + 8, GROUP), F32), pltpu.VMEM((1, GROUP), F32)],
        compiler_params=_cparams(("parallel", "arbitrary")),
        name="rglru",
    )(gc, gc, conv_w, conv_b, gate_w, gate_b, lam)


def _moba_tile(i, q_ref, k_ref, z_ref, o_ref, kmean_sc, vt_sc, *, seq):
    tq = MOBA_BLOCK
    nb = seq // MOBA_BLOCK
    ncand = MOBA_CAND_ROWS
    nk = (i + 1) * tq
    past = lax.broadcasted_iota(jnp.int32, (ncand, tq), 0) < i
    causal = (lax.broadcasted_iota(jnp.int32, (tq, tq), 0) <= lax.broadcasted_iota(jnp.int32, (tq, tq), 1))
    for h in range(NH):
        cs = slice(h * DH, (h + 1) * DH)
        qh = q_ref[:, cs]
        s = _dot_nt(k_ref[0:nk, cs], _prescale(qh))
        parts = [jnp.where(causal, s[i * tq:nk], NEG)]
        if i > 0:
            gate = _dot_nt(kmean_sc[h].astype(BF16), qh)[0:ncand]
            score = jnp.where(past, gate, NEG)
            sel = jnp.logical_and(_topk_rows(score, min(MOBA_TOPK, nb)), past)
            bias = jnp.where(sel, 0.0, NEG)
            parts = [s[j * tq:(j + 1) * tq] + bias[j:j + 1] for j in range(i)] + parts
        p, inv = _softmax_rows(jnp.concatenate(parts, axis=0))
        o = jnp.transpose(_dot(vt_sc[h, :, 0:nk], p) * inv)
        o_ref[:, cs] = (o * _silu(z_ref[:, cs].astype(F32))).astype(o_ref.dtype)


def _moba_kernel(q_ref, k_ref, v_ref, z_ref, o_ref, kmean_sc, vt_sc, *, seq):
    i = pl.program_id(1)
    tq = MOBA_BLOCK
    nb = seq // MOBA_BLOCK

    @pl.when(i == 0)
    def _():
        kmean_sc[...] = jnp.zeros(kmean_sc.shape, F32)
        for h in range(NH):
            cs = slice(h * DH, (h + 1) * DH)
            kmean_sc[h, 0:nb, :] = jnp.mean(k_ref[:, cs].astype(F32).reshape(nb, MOBA_BLOCK, DH), axis=1)
            for c in range(nb):
                rows = slice(c * tq, (c + 1) * tq)
                vt_sc[h, :, rows] = jnp.transpose(v_ref[rows, cs].astype(F32)).astype(BF16)

    for c in range(nb):
        pl.when(i == c)(functools.partial(_moba_tile, c, q_ref, k_ref, z_ref, o_ref, kmean_sc, vt_sc, seq=seq))


def _moba(ga, gb, gc, batch, seq):
    tq = MOBA_BLOCK
    nq = seq // tq
    assert nq <= MOBA_CAND_ROWS
    row = lambda col: (lambda b, i: (b * nq + i, col))
    return pl.pallas_call(
        functools.partial(_moba_kernel, seq=seq),
        out_shape=jax.ShapeDtypeStruct((batch * seq, GROUP), BF16),
        grid=(batch, nq),
        in_specs=[pl.BlockSpec((tq, GROUP), row(2)),
                  pl.BlockSpec((seq, GROUP), lambda b, i: (b, 3)),
                  pl.BlockSpec((seq, GROUP), lambda b, i: (b, 4)),
                  pl.BlockSpec((tq, GROUP), row(GC0 + 5))],
        out_specs=pl.BlockSpec((tq, GROUP), row(0)),
        scratch_shapes=[pltpu.VMEM((NH, LANES, DH), F32), pltpu.VMEM((NH, DH, seq), BF16)],
        compiler_params=_cparams(("parallel", "arbitrary")),
        name="moba",
    )(ga, ga, gb, gc)


def _out_kernel(ya_ref, yb_ref, yc_ref, yd_ref, w_ref, x_ref, g_ref, b_ref, o_ref, ob_ref):
    acc = DEEPNORM_ALPHA * x_ref[...]
    for p, y_ref in enumerate((ya_ref, yb_ref, yc_ref, yd_ref)):
        acc = acc + _dot(y_ref[...], w_ref[p * GROUP:(p + 1) * GROUP, :])
    mu = jnp.mean(acc, axis=-1, keepdims=True)
    var = jnp.mean(jnp.square(acc - mu), axis=-1, keepdims=True)
    y = (acc - mu) * lax.rsqrt(var + LN_EPS) * g_ref[...] + b_ref[...]
    o_ref[...] = y
    ob_ref[...] = y.astype(BF16)


def _out_proj(ys, w_out, x, ln_g, ln_b):
    m, d = x.shape
    tm = OUT_ROWS
    yspec = pl.BlockSpec((tm, GROUP), lambda i: (i, 0))
    full = lambda a: pl.BlockSpec(a.shape, lambda i: (0,) * a.ndim)
    xspec = pl.BlockSpec((tm, d), lambda i: (i, 0))
    return pl.pallas_call(
        _out_kernel,
        out_shape=(jax.ShapeDtypeStruct((m, d), F32), jax.ShapeDtypeStruct((m, d), BF16)),
        grid=(m // tm,),
        in_specs=[yspec] * 4 + [full(w_out), xspec, full(ln_g), full(ln_b)],
        out_specs=(xspec, xspec),
        compiler_params=_cparams(("parallel",)),
        name="out_proj_ln",
    )(*ys, w_out, x, ln_g, ln_b)


_GROUPS = (
    (('nsa_q', 0, 512), ('nsa_kv', 0, 384), ('gates', 0, LANES), ('moba_qkv', 0, 1024)),
    (('nsa_kv', 384, 384), (None, 0, 128), ('mlstm_qkv', 0, 1536), ('moba_qkv', 1024, 512)),
    (('nsa_z', 0, 512), ('mlstm_o', 0, 512), ('mlstm_z', 0, 512), ('lru_x', 0, 512), ('lru_z', 0, 512),
     ('moba_z', 0, 512)),
)
_GATE_PIECES = (('nsa_gate', 0, 12), (None, 0, 4), ('mlstm_if', 0, 8), (None, 0, LANES - 24))


def _copy_rows(w_ref, pieces, o_ref, dst):
    for name, off, width in pieces:
        if name is None:
            o_ref[dst:dst + width, :] = jnp.zeros((width, o_ref.shape[1]), o_ref.dtype)
        else:
            src = _OFF[name] + off
            o_ref[dst:dst + width, :] = w_ref[src:src + width, :].astype(o_ref.dtype)
        dst += width


def _regroup_kernel(w_ref, o_ref, gate_sc):
    dst = 0
    for piece in sum(_GROUPS, ()):
        if piece[0] == 'gates':
            _copy_rows(w_ref, _GATE_PIECES, gate_sc, 0)
            o_ref[dst:dst + LANES, :] = gate_sc[...].astype(o_ref.dtype)
        else:
            _copy_rows(w_ref, (piece,), o_ref, dst)
        dst += piece[2]


def _regroup_w_in(w_in):
    depth, d, n_in = w_in.shape
    wt = jnp.transpose(w_in, (0, 2, 1))
    tk = REGROUP_COLS
    n_all = sum(p[2] for pieces in _GROUPS for p in pieces)
    return pl.pallas_call(
        _regroup_kernel,
        out_shape=jax.ShapeDtypeStruct((depth, n_all, d), BF16),
        grid=(depth, d // tk),
        in_specs=[pl.BlockSpec((None, n_in, tk), lambda l, i: (l, 0, i))],
        out_specs=pl.BlockSpec((None, n_all, tk), lambda l, i: (l, 0, i)),
        scratch_shapes=[pltpu.VMEM((LANES, tk), F32)],
        compiler_params=_cparams(("parallel", "parallel")),
        name="regroup_w_in",
    )(wt)


def _rope_tables(seq):
    half = ROT_DIM // 2
    inv_freq = jnp.power(ROPE_THETA, -jnp.arange(half, dtype=F32) * (2.0 / ROT_DIM))
    ang = jnp.arange(seq, dtype=jnp.int32).astype(F32)[:, None] * inv_freq[None, :]
    cos, sin = jnp.cos(ang), jnp.sin(ang)
    ones = jnp.ones((seq, DH - ROT_DIM), F32)
    lane = jnp.arange(DH)
    partner = jnp.where(lane < half, lane + half, lane - half)
    swap = (lane[:, None] == partner[None, :]).astype(BF16)
    return (jnp.concatenate([cos, cos, ones], axis=1),
            jnp.concatenate([-sin, sin, 0.0 * ones], axis=1), swap)


def _cmp_overlap_t(seq):
    n_cmp = seq // CMP_STRIDE - 1
    n_sel = seq // SEL_BLOCK
    cs = jnp.arange(LANES)[None, :] * CMP_STRIDE
    ss = jnp.arange(LANES)[:, None] * SEL_BLOCK
    ov = (cs < ss + SEL_BLOCK) & (cs + CMP_BLOCK > ss)
    ov = ov & (jnp.arange(LANES)[None, :] < n_cmp) & (jnp.arange(LANES)[:, None] < n_sel)
    return ov.astype(BF16)


def _chunk_tri(tc):
    r = jnp.arange(tc)
    same = (r[:, None] // MLSTM_CHUNK) == (r[None, :] // MLSTM_CHUNK)
    tril = (same & (r[None, :] <= r[:, None])).astype(BF16)
    return tril, tril.T


def _layer(x, xb, batch, seq, consts, w_all, layer, cmp_w1, cmp_w2, cmp_pe, i_bias, f_bias, norm_g,
           conv_w, conv_b, gate_w, gate_b, lam, w_out, ln_g, ln_b):
    rope, ov_t, tril, triu = consts
    ga, gbc, gd, gd_t = _project(xb, w_all, layer, rope, seq, min(PROJ_ROWS, seq), PROJ_COLS)
    gb = gc = gbc

    nb = seq // CMP_STRIDE
    tk = ga[:, 4 * DH:5 * DH].reshape(batch, nb, CMP_STRIDE * DH)
    tv = gb[:, 0:DH].reshape(batch, nb, CMP_STRIDE * DH)
    pe = cmp_pe.reshape(2, 2, CMP_STRIDE * DH)
    kc, vc = _compress(tk, tv, cmp_w1.astype(BF16), cmp_w2.astype(BF16), pe)
    y_a = _nsa(ga, gb, gc, gd_t, kc, vc, ov_t, batch, seq)

    bias = jnp.concatenate([i_bias, f_bias])
    bias_col = jnp.zeros((1, LANES), F32).at[0, GD_I:GD_I + 2 * NH].set(bias)
    bias_row = bias[:, None]
    y_b = _mlstm(gb, gc, gd, gd_t, bias_col, bias_row, norm_g[None, :], tril, triu, batch, seq)

    y_c = _lru(gc, conv_w, conv_b[None, :], gate_w.astype(BF16), gate_b, lam[None, :], batch, seq)

    y_d = _moba(ga, gb, gc, batch, seq)

    return _out_proj((y_a, y_b, y_c, y_d), w_out.astype(BF16), x, ln_g[None, :], ln_b[None, :])


def kernel(x, w_in, nsa_cmp_w1, nsa_cmp_w2, nsa_cmp_pe, mlstm_i_bias, mlstm_f_bias, mlstm_norm_g,
           lru_conv_w, lru_conv_b, lru_gate_w, lru_gate_b, lru_lambda, w_out, ln_g, ln_b):
    batch, seq, d = x.shape
    tril, triu = _chunk_tri(MLSTM_CHUNK)
    consts = (_rope_tables(seq), _cmp_overlap_t(seq), tril, triu)
    xf = x.reshape(batch * seq, d)
    xb = xf.astype(BF16)
    w_all = _regroup_w_in(w_in)
    for l in range(w_in.shape[0]):
        xf, xb = _layer(xf, xb, batch, seq, consts, w_all, l, nsa_cmp_w1[l], nsa_cmp_w2[l],
                        nsa_cmp_pe[l], mlstm_i_bias[l], mlstm_f_bias[l], mlstm_norm_g[l], lru_conv_w[l],
                        lru_conv_b[l], lru_gate_w[l], lru_gate_b[l], lru_lambda[l], w_out[l], ln_g[l], ln_b[l])
    return xf.reshape(batch, seq, d)
```

```python
import functools
import math

import jax
import jax.numpy as jnp
from jax import lax
from jax.experimental import pallas as pl
from jax.experimental.pallas import tpu as pltpu

F32 = jnp.float32
BF16 = jnp.bfloat16

DEPTH = 2
GROUP = 512
DH = 128
NH = 4
ROT_DIM = 32
ROPE_THETA = 500000.0

CMP_BLOCK = 32
CMP_STRIDE = 16
SEL_BLOCK = 64
SEL_TOPK = 8
WIN = 256
FORCE_SCORE = 1e9

MLSTM_CHUNK = 512
LRU_C = 8.0
MOBA_BLOCK = 256
MOBA_TOPK = 3

DEEPNORM_ALPHA = (2 * DEPTH) ** 0.25
NEG = -1e30
LN_EPS = 1e-5
SCALE = DH ** -0.5
EXP2_SCALE = SCALE * math.log2(math.e)

LANES = 128
SUB = 8
VMEM_LIMIT = 56 * 1024 * 1024

PROJ_ROWS = 2048
PROJ_COLS = 512
REGROUP_COLS = 256
NSA_ROWS = 256
NSA_SEL_ROWS = 32
MOBA_CAND_ROWS = SUB
LRU_ROWS = 512
MLSTM_ROWS = 2 * MLSTM_CHUNK
OUT_ROWS = 512

_OFF = {}
_o = 0
for _name, _w in (('nsa_q', 512), ('nsa_kv', 768), ('nsa_gate', 12), ('nsa_z', 512), ('mlstm_qkv', 1536),
                  ('mlstm_if', 8), ('mlstm_o', 512), ('mlstm_z', 512), ('lru_x', 512), ('lru_z', 512),
                  ('moba_qkv', 1536), ('moba_z', 512)):
    _OFF[_name] = _o
    _o += _w

GD_I = 16
GD_F = 20
GC0 = 5
GD_COL = 7 * 128

NT = (((1,), (1,)), ((), ()))
TN = (((0,), (0,)), ((), ()))


def _cparams(sem):
    return pltpu.CompilerParams(dimension_semantics=sem, vmem_limit_bytes=VMEM_LIMIT)


def _dot(a, b):
    return jnp.dot(a, b, preferred_element_type=F32)


def _dot_nt(a, b):
    return lax.dot_general(a, b, NT, preferred_element_type=F32)


def _split_hi_lo(a):
    hi = a.astype(BF16)
    lo = (a - hi.astype(F32)).astype(BF16)
    return hi, lo


def _sigmoid(x):
    return 0.5 * jnp.tanh(0.5 * x) + 0.5


def _silu(x):
    return x * _sigmoid(x)


def _log_sigmoid(x):
    return jnp.minimum(x, 0.0) - jnp.log(1.0 + jnp.exp(-jnp.abs(x)))


def _proj_kernel(x_ref, w_ref, cos_ref, sin_ref, swap_ref, ga_ref, gbc_ref, gd_ref, gdt_ref, *, n_rot):
    j = pl.program_id(1)
    acc = _dot_nt(x_ref[...], w_ref[...])
    gbc_ref[...] = acc.astype(gbc_ref.dtype)

    @pl.when(j < n_rot)
    def _():
        c = cos_ref[...]
        s = sin_ref[...]
        for h in range(acc.shape[1] // DH):
            t = acc[:, h * DH:(h + 1) * DH]
            swapped = _dot(t.astype(BF16), swap_ref[...])
            ga_ref[:, h * DH:(h + 1) * DH] = (t * c + swapped * s).astype(ga_ref.dtype)

    @pl.when(j == GD_COL // acc.shape[1])
    def _():
        gd = acc[:, GD_COL % acc.shape[1]:GD_COL % acc.shape[1] + LANES]
        gd_ref[...] = gd
        gdt_ref[...] = jnp.transpose(gd)


def _project(xb, w_all, layer, rope, seq, tm, tn):
    m, k = xb.shape
    widths = [sum(p[2] for p in pieces) for pieces in _GROUPS]
    assert tn == GROUP and widths[1] == GC0 * GROUP
    n_rot = widths[0] // tn
    nrep = seq // tm
    rope_spec = pl.BlockSpec((tm, DH), lambda i, j: (i % nrep, 0))
    cos, sin, swap = rope
    return pl.pallas_call(
        functools.partial(_proj_kernel, n_rot=n_rot),
        out_shape=(jax.ShapeDtypeStruct((m, widths[0]), BF16), jax.ShapeDtypeStruct((m, widths[1] + widths[2]), BF16),
                   jax.ShapeDtypeStruct((m, LANES), F32), jax.ShapeDtypeStruct((LANES, m), F32)),
        grid=(m // tm, sum(widths) // tn),
        in_specs=[pl.BlockSpec((tm, k), lambda i, j: (i, 0)), pl.BlockSpec((None, tn, k), lambda i, j: (layer, j, 0)),
                  rope_spec, rope_spec, pl.BlockSpec((DH, DH), lambda i, j: (0, 0))],
        out_specs=(pl.BlockSpec((tm, tn), lambda i, j: (i, jnp.minimum(j, n_rot - 1))),
                   pl.BlockSpec((tm, tn), lambda i, j: (i, jnp.maximum(j - n_rot, 0))),
                   pl.BlockSpec((tm, LANES), lambda i, j: (i, 0)), pl.BlockSpec((LANES, tm), lambda i, j: (0, i))),
        compiler_params=_cparams(("parallel", "arbitrary")),
        name="in_proj",
    )(xb, w_all, cos, sin, swap)


def _compress_kernel(tk_ref, tv_ref, w1_ref, w2_ref, pe_ref, kc_ref, vc_ref, *, nb):
    half = (CMP_BLOCK // 2) * DH
    for idx, (t_ref, o_ref) in enumerate(((tk_ref, kc_ref), (tv_ref, vc_ref))):
        t = t_ref[...].astype(F32)
        lo = (t + pe_ref[idx, 0:1, :]).astype(BF16)
        hi = (t + pe_ref[idx, 1:2, :]).astype(BF16)
        a = _dot(lo, w1_ref[idx, 0:half, :])
        b = _dot(hi, w1_ref[idx, half:2 * half, :])
        hid = _silu(a + pltpu.roll(b, nb - 1, 0))
        out = _dot(hid.astype(BF16), w2_ref[idx])
        o_ref[...] = jnp.zeros(o_ref.shape, o_ref.dtype)
        o_ref[0:nb, :] = out.astype(o_ref.dtype)


def _compress(tk, tv, w1, w2, pe):
    b, nb, width = tk.shape
    blk = pl.BlockSpec((None, nb, width), lambda i: (i, 0, 0))
    full = lambda a: pl.BlockSpec(a.shape, lambda i: (0,) * a.ndim)
    out = jax.ShapeDtypeStruct((b, LANES, DH), BF16)
    return pl.pallas_call(
        functools.partial(_compress_kernel, nb=nb),
        out_shape=(out, out),
        grid=(b,),
        in_specs=[blk, blk, full(w1), full(w2), full(pe)],
        out_specs=(pl.BlockSpec((None, LANES, DH), lambda i: (i, 0, 0)),) * 2,
        compiler_params=_cparams(("parallel",)),
        name="nsa_compress",
    )(tk, tv, w1, w2, pe)


def _topk_rows(score, k_top):
    rowi = lax.broadcasted_iota(jnp.int32, score.shape, 0)
    rank = jnp.zeros(score.shape, F32)
    for k in range(score.shape[0]):
        sk = score[k:k + 1, :]
        beats = jnp.logical_or(sk > score, jnp.logical_and(sk == score, rowi > k))
        rank = rank + jnp.where(beats, 1.0, 0.0)
    return rank < k_top


def _prescale(q):
    return (q.astype(F32) * EXP2_SCALE).astype(q.dtype)


def _softmax_rows(s):
    p = jnp.exp2(s - jnp.max(s, axis=0, keepdims=True))
    return p.astype(BF16), 1.0 / jnp.sum(p, axis=0, keepdims=True)


def _nsa_tile(i, q_ref, kc_ref, vc_ref, ks_ref, kw_ref, gt_ref, z_ref, ovt_ref, diagb_ref, winb_ref,
              o_ref, vst_sc, vwt_sc, *, seq, tq):
    width = NH * tq
    n_cmp = seq // CMP_STRIDE - 1
    n_sel = seq // SEL_BLOCK
    n_rows = NSA_SEL_ROWS
    q = q_ref[...]
    qs = jnp.concatenate([q[:, h * DH:(h + 1) * DH] for h in range(NH)], axis=0)
    qs = _prescale(qs)

    rown = lax.broadcasted_iota(jnp.int32, (LANES, width), 0)
    tok = i * tq + (lax.broadcasted_iota(jnp.int32, (LANES, width), 1) & (tq - 1))
    mask_c = jnp.logical_and(rown * CMP_STRIDE + (CMP_BLOCK - 1) <= tok, rown < n_cmp)
    s = jnp.where(mask_c, _dot_nt(kc_ref[...], qs), NEG)
    e = jnp.exp2(s - jnp.max(s, axis=0, keepdims=True))
    p_c = jnp.where(mask_c, e, 0.0) * (1.0 / jnp.sum(e, axis=0, keepdims=True))
    o_c = lax.dot_general(vc_ref[...], p_c.astype(BF16), TN, preferred_element_type=F32)

    p_sum = p_c[:, 0:tq]
    for h in range(1, NH):
        p_sum = p_sum + p_c[:, h * tq:(h + 1) * tq]
    p_hi, p_lo = _split_hi_lo(p_sum)
    imp = (_dot(ovt_ref[...], p_hi) + _dot(ovt_ref[...], p_lo))[0:n_rows]
    rowj = lax.broadcasted_iota(jnp.int32, (n_rows, tq), 0)
    t_q = i * tq + lax.broadcasted_iota(jnp.int32, (n_rows, tq), 1)
    cur = t_q >> (SEL_BLOCK.bit_length() - 1)
    forced = jnp.logical_or(rowj == 0, jnp.logical_or(rowj == cur, rowj == cur - 1))
    valid = rowj * SEL_BLOCK <= t_q
    score = jnp.where(forced, FORCE_SCORE, jnp.where(valid, imp, NEG))
    score = jnp.where(rowj < n_sel, score, -jnp.inf)
    bias = jnp.where(_topk_rows(score, min(SEL_TOPK, n_sel)), 0.0, NEG)
    bias = jnp.concatenate([bias] * NH, axis=1)

    nk = (i + 1) * tq
    s = _dot_nt(ks_ref[0:nk, :], qs)
    parts = [s[j * SEL_BLOCK:(j + 1) * SEL_BLOCK] + bias[j:j + 1] for j in range(nk // SEL_BLOCK)]
    own = jnp.concatenate(parts[i * tq // SEL_BLOCK:], axis=0) + diagb_ref[...]
    p, inv = _softmax_rows(jnp.concatenate(parts[:i * tq // SEL_BLOCK] + [own], axis=0))
    o_s = _dot(vst_sc[:, 0:nk], p) * inv

    n_wblk = winb_ref.shape[0]
    blocks = [b for b in range(i - n_wblk + 1, i + 1) if b >= 0]
    parts = [_dot_nt(kw_ref[b * tq:(b + 1) * tq, :], qs) + winb_ref[b - i + n_wblk - 1] for b in blocks]
    p, inv = _softmax_rows(jnp.concatenate(parts, axis=0))
    o_w = _dot(vwt_sc[:, blocks[0] * tq:nk], p) * inv

    g = _sigmoid(gt_ref[0:GD_I, :])
    for h in range(NH):
        ls = slice(h * tq, (h + 1) * tq)
        cs = slice(h * DH, (h + 1) * DH)
        mix = (g[3 * h:3 * h + 1] * o_c[:, ls] + g[3 * h + 1:3 * h + 2] * o_s[:, ls]
               + g[3 * h + 2:3 * h + 3] * o_w[:, ls])
        o_ref[:, cs] = (jnp.transpose(mix) * _silu(z_ref[:, cs].astype(F32))).astype(o_ref.dtype)


def _nsa_kernel(q_ref, kc_ref, vc_ref, ks_ref, vs_ref, kw_ref, vw_ref, gt_ref, z_ref, ovt_ref, diagb_ref,
                winb_ref, o_ref, vst_sc, vwt_sc, *, seq, tq):
    i = pl.program_id(1)

    @pl.when(i == 0)
    def _():
        for c in range(seq // tq):
            cols = slice(c * tq, (c + 1) * tq)
            vst_sc[:, cols] = jnp.transpose(vs_ref[cols, :].astype(F32)).astype(BF16)
            vwt_sc[:, cols] = jnp.transpose(vw_ref[cols, :].astype(F32)).astype(BF16)

    for c in range(seq // tq):
        pl.when(i == c)(functools.partial(
            _nsa_tile, c, q_ref, kc_ref, vc_ref, ks_ref, kw_ref, gt_ref, z_ref, ovt_ref, diagb_ref, winb_ref,
            o_ref, vst_sc, vwt_sc, seq=seq, tq=tq))


def _nsa_bias_tables(tq):
    t = jnp.arange(NH * tq)[None, :] % tq
    r = jnp.arange(tq)[:, None]
    diag = jnp.where(r <= t, 0.0, NEG)
    n_wblk = WIN // tq + 1
    win = []
    for b in range(n_wblk):
        diff = t - (r + (b - (n_wblk - 1)) * tq)
        win.append(jnp.where((diff >= 0) & (diff < WIN), 0.0, NEG))
    return diag.astype(F32), jnp.stack(win).astype(F32)


def _nsa(ga, gb, gc, gd_t, kc, vc, ov_t, batch, seq):
    tq = NSA_ROWS
    nq = seq // tq
    assert seq // SEL_BLOCK <= NSA_SEL_ROWS and tq % SEL_BLOCK == 0
    row = lambda b, i: (b * nq + i, 0)
    kv = lambda col: pl.BlockSpec((seq, DH), lambda b, i: (b, col))
    cmp_spec = pl.BlockSpec((None, LANES, DH), lambda b, i: (b, 0, 0))
    full = lambda a: pl.BlockSpec(a.shape, lambda b, i: (0,) * a.ndim)
    diag_b, win_b = _nsa_bias_tables(tq)
    return pl.pallas_call(
        functools.partial(_nsa_kernel, seq=seq, tq=tq),
        out_shape=jax.ShapeDtypeStruct((batch * seq, GROUP), BF16),
        grid=(batch, nq),
        in_specs=[pl.BlockSpec((tq, GROUP), row), cmp_spec, cmp_spec,
                  kv(5), kv(1), kv(6), kv(2),
                  pl.BlockSpec((LANES, tq), lambda b, i: (0, b * nq + i)),
                  pl.BlockSpec((tq, GROUP), lambda b, i: (b * nq + i, GC0)), full(ov_t), full(diag_b), full(win_b)],
        out_specs=pl.BlockSpec((tq, GROUP), row),
        scratch_shapes=[pltpu.VMEM((DH, seq), BF16), pltpu.VMEM((DH, seq), BF16)],
        compiler_params=_cparams(("parallel", "arbitrary")),
        name="nsa_attention",
    )(ga, kc, vc, ga, gb, ga, gb, gd_t, gc, ov_t, diag_b, win_b)


def _mlstm_kernel(q_ref, k_ref, v_ref, gcol_ref, grow_ref, bcol_ref, brow_ref, og_ref, z_ref, ng_ref,
                  tril_ref, triu_ref, out_ref, cn_sc, m_sc, *, tc):
    L = MLSTM_CHUNK

    @pl.when(pl.program_id(1) == 0)
    def _():
        cn_sc[...] = jnp.zeros(cn_sc.shape, F32)
        m_sc[...] = jnp.full(m_sc.shape, NEG, F32)

    gcol = gcol_ref[...] + bcol_ref[...]
    grow = grow_ref[...] + brow_ref[...]
    hi, lo = _split_hi_lo(_log_sigmoid(gcol))
    bcol_all = _dot(tril_ref[...], hi) + _dot(tril_ref[...], lo)
    hi, lo = _split_hi_lo(_log_sigmoid(grow))
    brow_all = _dot(hi, triu_ref[...]) + _dot(lo, triu_ref[...])

    causal = (lax.broadcasted_iota(jnp.int32, (L, L), 1) <= lax.broadcasted_iota(jnp.int32, (L, L), 0))
    ones = jnp.ones((L, DH), BF16)
    mean_mat = jnp.full((DH, DH), 1.0 / DH, BF16)

    def lane_mean(a):
        a_hi, a_lo = _split_hi_lo(a)
        return _dot(a_hi, mean_mat) + _dot(a_lo, mean_mat)

    for c in range(tc // L):
        rs = slice(c * L, (c + 1) * L)
        for h in range(NH):
            cs = slice(h * DH, (h + 1) * DH)
            qh = q_ref[rs, cs]
            kh = k_ref[rs, cs]
            v_aug = jnp.concatenate([v_ref[rs, cs], ones], axis=1)
            ig_col = gcol[rs, GD_I + h:GD_I + h + 1]
            b_col = bcol_all[rs, GD_F + h:GD_F + h + 1]
            ig_row = grow[h:h + 1, rs]
            b_row = brow_all[NH + h:NH + h + 1, rs]
            cn_prev = cn_sc[h]
            m_prev = m_sc[h][:, 0:1]

            d_log = jnp.where(causal, b_col + (ig_row - b_row), NEG)
            m_intra = jnp.max(d_log, axis=-1, keepdims=True)
            m_inter = b_col + m_prev
            m_t = jnp.maximum(m_inter, m_intra)
            w_inter = jnp.exp(m_inter - m_t)
            qk = _dot_nt(qh, kh) * jnp.exp(d_log - (m_t - math.log(SCALE)))
            intra = _dot(qk.astype(BF16), v_aug)
            inter = _dot(qh, cn_prev.astype(BF16))
            num = intra[:, 0:DH] + w_inter * inter[:, 0:DH]
            den = intra[:, DH:2 * DH] + w_inter * inter[:, DH:2 * DH]
            hh = num / jnp.maximum(jnp.abs(den), jnp.exp(-m_t))
            hh = _sigmoid(og_ref[rs, cs].astype(F32)) * hh
            dlt = hh - lane_mean(hh)
            hn = dlt * lax.rsqrt(lane_mean(jnp.square(dlt)) + LN_EPS) * ng_ref[:, cs]
            out_ref[rs, cs] = (hn * _silu(z_ref[rs, cs].astype(F32))).astype(out_ref.dtype)

            b_last = b_row[:, L - 1:L]
            m_loc = jnp.max(b_last - b_row + ig_row, axis=-1, keepdims=True)
            e_col = jnp.exp(b_last - b_col + ig_col - m_loc)
            ek = ((e_col * SCALE) * kh.astype(F32)).astype(BF16)
            g_cn = lax.dot_general(ek, v_aug, TN, preferred_element_type=F32)
            m_new = jnp.maximum(b_last + m_prev, m_loc)
            cn_sc[h] = jnp.exp(b_last + m_prev - m_new) * cn_prev + jnp.exp(m_loc - m_new) * g_cn
            m_sc[h] = jnp.broadcast_to(m_new, (1, LANES))


def _mlstm(gb, gc, gd, gd_t, bias_col, bias_row, norm_g, tril, triu, batch, seq):
    tc = MLSTM_ROWS
    nt = seq // tc
    row = lambda col: (lambda b, j: (b * nt + j, col))
    full = lambda a: pl.BlockSpec(a.shape, lambda b, j: (0,) * a.ndim)
    return pl.pallas_call(
        functools.partial(_mlstm_kernel, tc=tc),
        out_shape=jax.ShapeDtypeStruct((batch * seq, GROUP), BF16),
        grid=(batch, nt),
        in_specs=[pl.BlockSpec((tc, GROUP), row(1)), pl.BlockSpec((tc, GROUP), row(2)),
                  pl.BlockSpec((tc, GROUP), row(3)),
                  pl.BlockSpec((tc, LANES), row(0)),
                  pl.BlockSpec((2 * NH, tc), lambda b, j: (GD_I // (2 * NH), b * nt + j)),
                  full(bias_col), full(bias_row),
                  pl.BlockSpec((tc, GROUP), row(GC0 + 1)), pl.BlockSpec((tc, GROUP), row(GC0 + 2)),
                  full(norm_g), full(tril), full(triu)],
        out_specs=pl.BlockSpec((tc, GROUP), row(0)),
        scratch_shapes=[pltpu.VMEM((NH, DH, 2 * DH), F32), pltpu.VMEM((NH, 1, LANES), F32)],
        compiler_params=_cparams(("parallel", "arbitrary")),
        name="mlstm",
    )(gb, gb, gb, gd, gd_t, bias_col, bias_row, gc, gc, norm_g, tril, triu)


def _lru_kernel(x_ref, z_ref, cw_ref, cb_ref, gw_ref, gb_ref, lam_ref, out_ref, xbuf, h_sc, *, tl):
    pad = 8

    @pl.when(pl.program_id(1) == 0)
    def _():
        xbuf[0:pad, :] = jnp.zeros((pad, GROUP), F32)
        h_sc[...] = jnp.zeros(h_sc.shape, F32)

    x = x_ref[...].astype(F32)
    xbuf[pad:pad + tl, :] = x
    u = cw_ref[3:4, :] * x + cb_ref[...]
    for w in range(3):
        u = u + cw_ref[w:w + 1, :] * xbuf[pad - 3 + w:pad - 3 + w + tl, :]
    xbuf[0:pad, :] = x[tl - pad:tl, :]

    ub = u.astype(BF16)
    pre = []
    for gi in range(2):
        pre.append(jnp.concatenate(
            [_dot(ub[:, n * DH:(n + 1) * DH], gw_ref[gi, n]) for n in range(NH)], axis=1) + gb_ref[gi:gi + 1, :])
    r = _sigmoid(pre[0])
    ig = _sigmoid(pre[1])
    neg_lam = -lam_ref[...]
    softplus = jnp.maximum(neg_lam, 0.0) + jnp.log(1.0 + jnp.exp(-jnp.abs(neg_lam)))
    a = jnp.exp(-LRU_C * r * softplus)
    bx = jnp.sqrt(1.0 - a * a) * (ig * u)

    sub = lax.broadcasted_iota(jnp.int32, (tl, GROUP), 0) & (SUB - 1)
    d = 1
    while d < SUB:
        keep = sub >= d
        a_sh = jnp.where(keep, pltpu.roll(a, d, 0), 1.0)
        b_sh = jnp.where(keep, pltpu.roll(bx, d, 0), 0.0)
        bx = a * b_sh + bx
        a = a * a_sh
        d *= 2
    carry = jnp.broadcast_to(h_sc[...], (SUB, GROUP))
    groups = []
    for g in range(tl // SUB):
        hg = a[g * SUB:(g + 1) * SUB] * carry + bx[g * SUB:(g + 1) * SUB]
        groups.append(hg)
        carry = jnp.broadcast_to(hg[SUB - 1:SUB], (SUB, GROUP))
    hseq = jnp.concatenate(groups, axis=0)
    h_sc[...] = hseq[tl - 1:tl, :]
    out_ref[...] = (hseq * _silu(z_ref[...].astype(F32))).astype(out_ref.dtype)


def _lru(gc, conv_w, conv_b, gate_w, gate_b, lam, batch, seq):
    tl = LRU_ROWS
    nt = seq // tl
    row = lambda col: (lambda b, j: (b * nt + j, col))
    full = lambda a: pl.BlockSpec(a.shape, lambda b, j: (0,) * a.ndim)
    return pl.pallas_call(
        functools.partial(_lru_kernel, tl=tl),
        out_shape=jax.ShapeDtypeStruct((batch * seq, GROUP), BF16),
        grid=(batch, nt),
        in_specs=[pl.BlockSpec((tl, GROUP), row(GC0 + 3)), pl.BlockSpec((tl, GROUP), row(GC0 + 4)),
                  full(conv_w), full(conv_b), full(gate_w), full(gate_b), full(lam)],
        out_specs=pl.BlockSpec((tl, GROUP), row(0)),
        scratch_shapes=[pltpu.VMEM((tl + 8, GROUP), F32), pltpu.VMEM((1, GROUP), F32)],
        compiler_params=_cparams(("parallel", "arbitrary")),
        name="rglru",
    )(gc, gc, conv_w, conv_b, gate_w, gate_b, lam)


def _moba_tile(i, q_ref, k_ref, z_ref, o_ref, kmean_sc, vt_sc, *, seq):
    tq = MOBA_BLOCK
    nb = seq // MOBA_BLOCK
    ncand = MOBA_CAND_ROWS
    nk = (i + 1) * tq
    past = lax.broadcasted_iota(jnp.int32, (ncand, tq), 0) < i
    causal = (lax.broadcasted_iota(jnp.int32, (tq, tq), 0) <= lax.broadcasted_iota(jnp.int32, (tq, tq), 1))
    for h in range(NH):
        cs = slice(h * DH, (h + 1) * DH)
        qh = q_ref[:, cs]
        s = _dot_nt(k_ref[0:nk, cs], _prescale(qh))
        parts = [jnp.where(causal, s[i * tq:nk], NEG)]
        if i > 0:
            gate = _dot_nt(kmean_sc[h].astype(BF16), qh)[0:ncand]
            score = jnp.where(past, gate, NEG)
            sel = jnp.logical_and(_topk_rows(score, min(MOBA_TOPK, nb)), past)
            bias = jnp.where(sel, 0.0, NEG)
            parts = [s[j * tq:(j + 1) * tq] + bias[j:j + 1] for j in range(i)] + parts
        p, inv = _softmax_rows(jnp.concatenate(parts, axis=0))
        o = jnp.transpose(_dot(vt_sc[h, :, 0:nk], p) * inv)
        o_ref[:, cs] = (o * _silu(z_ref[:, cs].astype(F32))).astype(o_ref.dtype)


def _moba_kernel(q_ref, k_ref, v_ref, z_ref, o_ref, kmean_sc, vt_sc, *, seq):
    i = pl.program_id(1)
    tq = MOBA_BLOCK
    nb = seq // MOBA_BLOCK

    @pl.when(i == 0)
    def _():
        kmean_sc[...] = jnp.zeros(kmean_sc.shape, F32)
        for h in range(NH):
            cs = slice(h * DH, (h + 1) * DH)
            kmean_sc[h, 0:nb, :] = jnp.mean(k_ref[:, cs].astype(F32).reshape(nb, MOBA_BLOCK, DH), axis=1)
            for c in range(nb):
                rows = slice(c * tq, (c + 1) * tq)
                vt_sc[h, :, rows] = jnp.transpose(v_ref[rows, cs].astype(F32)).astype(BF16)

    for c in range(nb):
        pl.when(i == c)(functools.partial(_moba_tile, c, q_ref, k_ref, z_ref, o_ref, kmean_sc, vt_sc, seq=seq))


def _moba(ga, gb, gc, batch, seq):
    tq = MOBA_BLOCK
    nq = seq // tq
    assert nq <= MOBA_CAND_ROWS
    row = lambda col: (lambda b, i: (b * nq + i, col))
    return pl.pallas_call(
        functools.partial(_moba_kernel, seq=seq),
        out_shape=jax.ShapeDtypeStruct((batch * seq, GROUP), BF16),
        grid=(batch, nq),
        in_specs=[pl.BlockSpec((tq, GROUP), row(2)),
                  pl.BlockSpec((seq, GROUP), lambda b, i: (b, 3)),
                  pl.BlockSpec((seq, GROUP), lambda b, i: (b, 4)),
                  pl.BlockSpec((tq, GROUP), row(GC0 + 5))],
        out_specs=pl.BlockSpec((tq, GROUP), row(0)),
        scratch_shapes=[pltpu.VMEM((NH, LANES, DH), F32), pltpu.VMEM((NH, DH, seq), BF16)],
        compiler_params=_cparams(("parallel", "arbitrary")),
        name="moba",
    )(ga, ga, gb, gc)


def _out_kernel(ya_ref, yb_ref, yc_ref, yd_ref, w_ref, x_ref, g_ref, b_ref, o_ref, ob_ref):
    acc = DEEPNORM_ALPHA * x_ref[...]
    for p, y_ref in enumerate((ya_ref, yb_ref, yc_ref, yd_ref)):
        acc = acc + _dot(y_ref[...], w_ref[p * GROUP:(p + 1) * GROUP, :])
    mu = jnp.mean(acc, axis=-1, keepdims=True)
    var = jnp.mean(jnp.square(acc - mu), axis=-1, keepdims=True)
    y = (acc - mu) * lax.rsqrt(var + LN_EPS) * g_ref[...] + b_ref[...]
    o_ref[...] = y
    ob_ref[...] = y.astype(BF16)


def _out_proj(ys, w_out, x, ln_g, ln_b):
    m, d = x.shape
    tm = OUT_ROWS
    yspec = pl.BlockSpec((tm, GROUP), lambda i: (i, 0))
    full = lambda a: pl.BlockSpec(a.shape, lambda i: (0,) * a.ndim)
    xspec = pl.BlockSpec((tm, d), lambda i: (i, 0))
    return pl.pallas_call(
        _out_kernel,
        out_shape=(jax.ShapeDtypeStruct((m, d), F32), jax.ShapeDtypeStruct((m, d), BF16)),
        grid=(m // tm,),
        in_specs=[yspec] * 4 + [full(w_out), xspec, full(ln_g), full(ln_b)],
        out_specs=(xspec, xspec),
        compiler_params=_cparams(("parallel",)),
        name="out_proj_ln",
    )(*ys, w_out, x, ln_g, ln_b)


_GROUPS = (
    (('nsa_q', 0, 512), ('nsa_kv', 0, 384), ('gates', 0, LANES), ('moba_qkv', 0, 1024)),
    (('nsa_kv', 384, 384), (None, 0, 128), ('mlstm_qkv', 0, 1536), ('moba_qkv', 1024, 512)),
    (('nsa_z', 0, 512), ('mlstm_o', 0, 512), ('mlstm_z', 0, 512), ('lru_x', 0, 512), ('lru_z', 0, 512),
     ('moba_z', 0, 512)),
)
_GATE_PIECES = (('nsa_gate', 0, 12), (None, 0, 4), ('mlstm_if', 0, 8), (None, 0, LANES - 24))


def _copy_rows(w_ref, pieces, o_ref, dst):
    for name, off, width in pieces:
        if name is None:
            o_ref[dst:dst + width, :] = jnp.zeros((width, o_ref.shape[1]), o_ref.dtype)
        else:
            src = _OFF[name] + off
            o_ref[dst:dst + width, :] = w_ref[src:src + width, :].astype(o_ref.dtype)
        dst += width


def _regroup_kernel(w_ref, o_ref, gate_sc):
    dst = 0
    for piece in sum(_GROUPS, ()):
        if piece[0] == 'gates':
            _copy_rows(w_ref, _GATE_PIECES, gate_sc, 0)
            o_ref[dst:dst + LANES, :] = gate_sc[...].astype(o_ref.dtype)
        else:
            _copy_rows(w_ref, (piece,), o_ref, dst)
        dst += piece[2]


def _regroup_w_in(w_in):
    depth, d, n_in = w_in.shape
    wt = jnp.transpose(w_in, (0, 2, 1))
    tk = REGROUP_COLS
    n_all = sum(p[2] for pieces in _GROUPS for p in pieces)
    return pl.pallas_call(
        _regroup_kernel,
        out_shape=jax.ShapeDtypeStruct((depth, n_all, d), BF16),
        grid=(depth, d // tk),
        in_specs=[pl.BlockSpec((None, n_in, tk), lambda l, i: (l, 0, i))],
        out_specs=pl.BlockSpec((None, n_all, tk), lambda l, i: (l, 0, i)),
        scratch_shapes=[pltpu.VMEM((LANES, tk), F32)],
        compiler_params=_cparams(("parallel", "parallel")),
        name="regroup_w_in",
    )(wt)


def _rope_tables(seq):
    half = ROT_DIM // 2
    inv_freq = jnp.power(ROPE_THETA, -jnp.arange(half, dtype=F32) * (2.0 / ROT_DIM))
    ang = jnp.arange(seq, dtype=jnp.int32).astype(F32)[:, None] * inv_freq[None, :]
    cos, sin = jnp.cos(ang), jnp.sin(ang)
    ones = jnp.ones((seq, DH - ROT_DIM), F32)
    lane = jnp.arange(DH)
    partner = jnp.where(lane < half, lane + half, lane - half)
    swap = (lane[:, None] == partner[None, :]).astype(BF16)
    return (jnp.concatenate([cos, cos, ones], axis=1),
            jnp.concatenate([-sin, sin, 0.0 * ones], axis=1), swap)


def _cmp_overlap_t(seq):
    n_cmp = seq // CMP_STRIDE - 1
    n_sel = seq // SEL_BLOCK
    cs = jnp.arange(LANES)[None, :] * CMP_STRIDE
    ss = jnp.arange(LANES)[:, None] * SEL_BLOCK
    ov = (cs < ss + SEL_BLOCK) & (cs + CMP_BLOCK > ss)
    ov = ov & (jnp.arange(LANES)[None, :] < n_cmp) & (jnp.arange(LANES)[:, None] < n_sel)
    return ov.astype(BF16)


def _chunk_tri(tc):
    r = jnp.arange(tc)
    same = (r[:, None] // MLSTM_CHUNK) == (r[None, :] // MLSTM_CHUNK)
    tril = (same & (r[None, :] <= r[:, None])).astype(BF16)
    return tril, tril.T


def _layer(x, xb, batch, seq, consts, w_all, layer, cmp_w1, cmp_w2, cmp_pe, i_bias, f_bias, norm_g,
           conv_w, conv_b, gate_w, gate_b, lam, w_out, ln_g, ln_b):
    rope, ov_t, tril, triu = consts
    ga, gbc, gd, gd_t = _project(xb, w_all, layer, rope, seq, min(PROJ_ROWS, seq), PROJ_COLS)
    gb = gc = gbc

    nb = seq // CMP_STRIDE
    tk = ga[:, 4 * DH:5 * DH].reshape(batch, nb, CMP_STRIDE * DH)
    tv = gb[:, 0:DH].reshape(batch, nb, CMP_STRIDE * DH)
    pe = cmp_pe.reshape(2, 2, CMP_STRIDE * DH)
    kc, vc = _compress(tk, tv, cmp_w1.astype(BF16), cmp_w2.astype(BF16), pe)
    y_a = _nsa(ga, gb, gc, gd_t, kc, vc, ov_t, batch, seq)

    bias = jnp.concatenate([i_bias, f_bias])
    bias_col = jnp.zeros((1, LANES), F32).at[0, GD_I:GD_I + 2 * NH].set(bias)
    bias_row = bias[:, None]
    y_b = _mlstm(gb, gc, gd, gd_t, bias_col, bias_row, norm_g[None, :], tril, triu, batch, seq)

    y_c = _lru(gc, conv_w, conv_b[None, :], gate_w.astype(BF16), gate_b, lam[None, :], batch, seq)

    y_d = _moba(ga, gb, gc, batch, seq)

    return _out_proj((y_a, y_b, y_c, y_d), w_out.astype(BF16), x, ln_g[None, :], ln_b[None, :])


def kernel(x, w_in, nsa_cmp_w1, nsa_cmp_w2, nsa_cmp_pe, mlstm_i_bias, mlstm_f_bias, mlstm_norm_g,
           lru_conv_w, lru_conv_b, lru_gate_w, lru_gate_b, lru_lambda, w_out, ln_g, ln_b):
    batch, seq, d = x.shape
    tril, triu = _chunk_tri(MLSTM_ROWS)
    consts = (_rope_tables(seq), _cmp_overlap_t(seq), tril, triu)
    xf = x.reshape(batch * seq, d)
    xb = xf.astype(BF16)
    w_all = _regroup_w_in(w_in)
    for l in range(w_in.shape[0]):
        xf, xb = _layer(xf, xb, batch, seq, consts, w_all, l, nsa_cmp_w1[l], nsa_cmp_w2[l],
                        nsa_cmp_pe[l], mlstm_i_bias[l], mlstm_f_bias[l], mlstm_norm_g[l], lru_conv_w[l],
                        lru_conv_b[l], lru_gate_w[l], lru_gate_b[l], lru_lambda[l], w_out[l], ln_g[l], ln_b[l])
    return xf.reshape(batch, seq, d)
```

```python
import functools
import math

import jax
import jax.numpy as jnp
from jax import lax
from jax.experimental import pallas as pl
from jax.experimental.pallas import tpu as pltpu

F32 = jnp.float32
BF16 = jnp.bfloat16

DEPTH = 2
GROUP = 512
DH = 128
NH = 4
ROT_DIM = 32
ROPE_THETA = 500000.0

CMP_BLOCK = 32
CMP_STRIDE = 16
SEL_BLOCK = 64
SEL_TOPK = 8
WIN = 256
FORCE_SCORE = 1e9

MLSTM_CHUNK = 512
LRU_C = 8.0
MOBA_BLOCK = 256
MOBA_TOPK = 3

DEEPNORM_ALPHA = (2 * DEPTH) ** 0.25
NEG = -1e30
LN_EPS = 1e-5
SCALE = DH ** -0.5
EXP2_SCALE = SCALE * math.log2(math.e)

LANES = 128
SUB = 8
VMEM_LIMIT = 56 * 1024 * 1024

PROJ_ROWS = 2048
PROJ_COLS = 512
REGROUP_COLS = 256
NSA_ROWS = 256
NSA_SEL_ROWS = 32
MOBA_CAND_ROWS = SUB
LRU_ROWS = 512
MLSTM_ROWS = 2 * MLSTM_CHUNK
OUT_ROWS = 512

_OFF = {}
_o = 0
for _name, _w in (('nsa_q', 512), ('nsa_kv', 768), ('nsa_gate', 12), ('nsa_z', 512), ('mlstm_qkv', 1536),
                  ('mlstm_if', 8), ('mlstm_o', 512), ('mlstm_z', 512), ('lru_x', 512), ('lru_z', 512),
                  ('moba_qkv', 1536), ('moba_z', 512)):
    _OFF[_name] = _o
    _o += _w

GD_I = 16
GD_F = 20
GC0 = 5
GD_COL = 7 * 128

NT = (((1,), (1,)), ((), ()))
TN = (((0,), (0,)), ((), ()))


def _cparams(sem):
    return pltpu.CompilerParams(dimension_semantics=sem, vmem_limit_bytes=VMEM_LIMIT)


def _dot(a, b):
    return jnp.dot(a, b, preferred_element_type=F32)


def _dot_nt(a, b):
    return lax.dot_general(a, b, NT, preferred_element_type=F32)


def _split_hi_lo(a):
    hi = a.astype(BF16)
    lo = (a - hi.astype(F32)).astype(BF16)
    return hi, lo


def _sigmoid(x):
    return 0.5 * jnp.tanh(0.5 * x) + 0.5


def _silu(x):
    return x * _sigmoid(x)


def _log_sigmoid(x):
    return jnp.minimum(x, 0.0) - jnp.log(1.0 + jnp.exp(-jnp.abs(x)))


def _proj_kernel(x_ref, w_ref, cos_ref, sin_ref, swap_ref, ga_ref, gbc_ref, gd_ref, gdt_ref, *, n_rot):
    j = pl.program_id(1)
    acc = _dot_nt(x_ref[...], w_ref[...])
    gbc_ref[...] = acc.astype(gbc_ref.dtype)

    @pl.when(j < n_rot)
    def _():
        c = cos_ref[...]
        s = sin_ref[...]
        for h in range(acc.shape[1] // DH):
            t = acc[:, h * DH:(h + 1) * DH]
            swapped = _dot(t.astype(BF16), swap_ref[...])
            ga_ref[:, h * DH:(h + 1) * DH] = (t * c + swapped * s).astype(ga_ref.dtype)

    @pl.when(j == GD_COL // acc.shape[1])
    def _():
        gd = acc[:, GD_COL % acc.shape[1]:GD_COL % acc.shape[1] + LANES]
        gd_ref[...] = gd
        gdt_ref[...] = jnp.transpose(gd)


def _project(xb, w_all, layer, rope, seq, tm, tn):
    m, k = xb.shape
    widths = [sum(p[2] for p in pieces) for pieces in _GROUPS]
    assert tn == GROUP and widths[1] == GC0 * GROUP
    n_rot = widths[0] // tn
    nrep = seq // tm
    rope_spec = pl.BlockSpec((tm, DH), lambda i, j: (i % nrep, 0))
    cos, sin, swap = rope
    return pl.pallas_call(
        functools.partial(_proj_kernel, n_rot=n_rot),
        out_shape=(jax.ShapeDtypeStruct((m, widths[0]), BF16), jax.ShapeDtypeStruct((m, widths[1] + widths[2]), BF16),
                   jax.ShapeDtypeStruct((m, LANES), F32), jax.ShapeDtypeStruct((LANES, m), F32)),
        grid=(m // tm, sum(widths) // tn),
        in_specs=[pl.BlockSpec((tm, k), lambda i, j: (i, 0)), pl.BlockSpec((None, tn, k), lambda i, j: (layer, j, 0)),
                  rope_spec, rope_spec, pl.BlockSpec((DH, DH), lambda i, j: (0, 0))],
        out_specs=(pl.BlockSpec((tm, tn), lambda i, j: (i, jnp.minimum(j, n_rot - 1))),
                   pl.BlockSpec((tm, tn), lambda i, j: (i, jnp.maximum(j - n_rot, 0))),
                   pl.BlockSpec((tm, LANES), lambda i, j: (i, 0)), pl.BlockSpec((LANES, tm), lambda i, j: (0, i))),
        compiler_params=_cparams(("parallel", "arbitrary")),
        name="in_proj",
    )(xb, w_all, cos, sin, swap)


def _compress_kernel(tk_ref, tv_ref, w1_ref, w2_ref, pe_ref, kc_ref, vc_ref, *, nb):
    half = (CMP_BLOCK // 2) * DH
    for idx, (t_ref, o_ref) in enumerate(((tk_ref, kc_ref), (tv_ref, vc_ref))):
        t = t_ref[...].astype(F32)
        lo = (t + pe_ref[idx, 0:1, :]).astype(BF16)
        hi = (t + pe_ref[idx, 1:2, :]).astype(BF16)
        a = _dot(lo, w1_ref[idx, 0:half, :])
        b = _dot(hi, w1_ref[idx, half:2 * half, :])
        hid = _silu(a + pltpu.roll(b, nb - 1, 0))
        out = _dot(hid.astype(BF16), w2_ref[idx])
        o_ref[...] = jnp.zeros(o_ref.shape, o_ref.dtype)
        o_ref[0:nb, :] = out.astype(o_ref.dtype)


def _compress(tk, tv, w1, w2, pe):
    b, nb, width = tk.shape
    blk = pl.BlockSpec((None, nb, width), lambda i: (i, 0, 0))
    full = lambda a: pl.BlockSpec(a.shape, lambda i: (0,) * a.ndim)
    out = jax.ShapeDtypeStruct((b, LANES, DH), BF16)
    return pl.pallas_call(
        functools.partial(_compress_kernel, nb=nb),
        out_shape=(out, out),
        grid=(b,),
        in_specs=[blk, blk, full(w1), full(w2), full(pe)],
        out_specs=(pl.BlockSpec((None, LANES, DH), lambda i: (i, 0, 0)),) * 2,
        compiler_params=_cparams(("parallel",)),
        name="nsa_compress",
    )(tk, tv, w1, w2, pe)


def _topk_rows(score, k_top):
    rowi = lax.broadcasted_iota(jnp.int32, score.shape, 0)
    rank = jnp.zeros(score.shape, F32)
    for k in range(score.shape[0]):
        sk = score[k:k + 1, :]
        beats = jnp.logical_or(sk > score, jnp.logical_and(sk == score, rowi > k))
        rank = rank + jnp.where(beats, 1.0, 0.0)
    return rank < k_top


def _prescale(q):
    return (q.astype(F32) * EXP2_SCALE).astype(q.dtype)


def _softmax_rows(s):
    p = jnp.exp2(s - jnp.max(s, axis=0, keepdims=True))
    return p.astype(BF16), 1.0 / jnp.sum(p, axis=0, keepdims=True)


def _nsa_tile(i, q_ref, kc_ref, vc_ref, ks_ref, kw_ref, gt_ref, z_ref, ovt_ref, diagb_ref, winb_ref,
              o_ref, vst_sc, vwt_sc, *, seq, tq):
    width = NH * tq
    n_cmp = seq // CMP_STRIDE - 1
    n_sel = seq // SEL_BLOCK
    n_rows = NSA_SEL_ROWS
    q = q_ref[...]
    qs = jnp.concatenate([q[:, h * DH:(h + 1) * DH] for h in range(NH)], axis=0)
    qs = _prescale(qs)

    rown = lax.broadcasted_iota(jnp.int32, (LANES, width), 0)
    tok = i * tq + (lax.broadcasted_iota(jnp.int32, (LANES, width), 1) & (tq - 1))
    mask_c = jnp.logical_and(rown * CMP_STRIDE + (CMP_BLOCK - 1) <= tok, rown < n_cmp)
    s = jnp.where(mask_c, _dot_nt(kc_ref[...], qs), NEG)
    e = jnp.exp2(s - jnp.max(s, axis=0, keepdims=True))
    p_c = jnp.where(mask_c, e, 0.0) * (1.0 / jnp.sum(e, axis=0, keepdims=True))
    o_c = lax.dot_general(vc_ref[...], p_c.astype(BF16), TN, preferred_element_type=F32)

    p_sum = p_c[:, 0:tq]
    for h in range(1, NH):
        p_sum = p_sum + p_c[:, h * tq:(h + 1) * tq]
    p_hi, p_lo = _split_hi_lo(p_sum)
    imp = (_dot(ovt_ref[...], p_hi) + _dot(ovt_ref[...], p_lo))[0:n_rows]
    rowj = lax.broadcasted_iota(jnp.int32, (n_rows, tq), 0)
    t_q = i * tq + lax.broadcasted_iota(jnp.int32, (n_rows, tq), 1)
    cur = t_q >> (SEL_BLOCK.bit_length() - 1)
    forced = jnp.logical_or(rowj == 0, jnp.logical_or(rowj == cur, rowj == cur - 1))
    valid = rowj * SEL_BLOCK <= t_q
    score = jnp.where(forced, FORCE_SCORE, jnp.where(valid, imp, NEG))
    score = jnp.where(rowj < n_sel, score, -jnp.inf)
    bias = jnp.where(_topk_rows(score, min(SEL_TOPK, n_sel)), 0.0, NEG)
    bias = jnp.concatenate([bias] * NH, axis=1)

    nk = (i + 1) * tq
    s = _dot_nt(ks_ref[0:nk, :], qs)
    parts = [s[j * SEL_BLOCK:(j + 1) * SEL_BLOCK] + bias[j:j + 1] for j in range(nk // SEL_BLOCK)]
    own = jnp.concatenate(parts[i * tq // SEL_BLOCK:], axis=0) + diagb_ref[...]
    p, inv = _softmax_rows(jnp.concatenate(parts[:i * tq // SEL_BLOCK] + [own], axis=0))
    o_s = _dot(vst_sc[:, 0:nk], p) * inv

    n_wblk = winb_ref.shape[0]
    blocks = [b for b in range(i - n_wblk + 1, i + 1) if b >= 0]
    parts = [_dot_nt(kw_ref[b * tq:(b + 1) * tq, :], qs) + winb_ref[b - i + n_wblk - 1] for b in blocks]
    p, inv = _softmax_rows(jnp.concatenate(parts, axis=0))
    o_w = _dot(vwt_sc[:, blocks[0] * tq:nk], p) * inv

    g = _sigmoid(gt_ref[0:GD_I, :])
    for h in range(NH):
        ls = slice(h * tq, (h + 1) * tq)
        cs = slice(h * DH, (h + 1) * DH)
        mix = (g[3 * h:3 * h + 1] * o_c[:, ls] + g[3 * h + 1:3 * h + 2] * o_s[:, ls]
               + g[3 * h + 2:3 * h + 3] * o_w[:, ls])
        o_ref[:, cs] = (jnp.transpose(mix) * _silu(z_ref[:, cs].astype(F32))).astype(o_ref.dtype)


def _nsa_kernel(q_ref, kc_ref, vc_ref, ks_ref, vs_ref, kw_ref, vw_ref, gt_ref, z_ref, ovt_ref, diagb_ref,
                winb_ref, o_ref, vst_sc, vwt_sc, *, seq, tq):
    i = pl.program_id(1)

    @pl.when(i == 0)
    def _():
        for c in range(seq // tq):
            cols = slice(c * tq, (c + 1) * tq)
            vst_sc[:, cols] = jnp.transpose(vs_ref[cols, :].astype(F32)).astype(BF16)
            vwt_sc[:, cols] = jnp.transpose(vw_ref[cols, :].astype(F32)).astype(BF16)

    for c in range(seq // tq):
        pl.when(i == c)(functools.partial(
            _nsa_tile, c, q_ref, kc_ref, vc_ref, ks_ref, kw_ref, gt_ref, z_ref, ovt_ref, diagb_ref, winb_ref,
            o_ref, vst_sc, vwt_sc, seq=seq, tq=tq))


def _nsa_bias_tables(tq):
    t = jnp.arange(NH * tq)[None, :] % tq
    r = jnp.arange(tq)[:, None]
    diag = jnp.where(r <= t, 0.0, NEG)
    n_wblk = WIN // tq + 1
    win = []
    for b in range(n_wblk):
        diff = t - (r + (b - (n_wblk - 1)) * tq)
        win.append(jnp.where((diff >= 0) & (diff < WIN), 0.0, NEG))
    return diag.astype(F32), jnp.stack(win).astype(F32)


def _nsa(ga, gb, gc, gd_t, kc, vc, ov_t, batch, seq):
    tq = NSA_ROWS
    nq = seq // tq
    assert seq // SEL_BLOCK <= NSA_SEL_ROWS and tq % SEL_BLOCK == 0
    row = lambda b, i: (b * nq + i, 0)
    kv = lambda col: pl.BlockSpec((seq, DH), lambda b, i: (b, col))
    cmp_spec = pl.BlockSpec((None, LANES, DH), lambda b, i: (b, 0, 0))
    full = lambda a: pl.BlockSpec(a.shape, lambda b, i: (0,) * a.ndim)
    diag_b, win_b = _nsa_bias_tables(tq)
    return pl.pallas_call(
        functools.partial(_nsa_kernel, seq=seq, tq=tq),
        out_shape=jax.ShapeDtypeStruct((batch * seq, GROUP), BF16),
        grid=(batch, nq),
        in_specs=[pl.BlockSpec((tq, GROUP), row), cmp_spec, cmp_spec,
                  kv(5), kv(1), kv(6), kv(2),
                  pl.BlockSpec((LANES, tq), lambda b, i: (0, b * nq + i)),
                  pl.BlockSpec((tq, GROUP), lambda b, i: (b * nq + i, GC0)), full(ov_t), full(diag_b), full(win_b)],
        out_specs=pl.BlockSpec((tq, GROUP), row),
        scratch_shapes=[pltpu.VMEM((DH, seq), BF16), pltpu.VMEM((DH, seq), BF16)],
        compiler_params=_cparams(("parallel", "arbitrary")),
        name="nsa_attention",
    )(ga, kc, vc, ga, gb, ga, gb, gd_t, gc, ov_t, diag_b, win_b)


def _mlstm_kernel(q_ref, k_ref, v_ref, gcol_ref, grow_ref, bcol_ref, brow_ref, og_ref, z_ref, ng_ref,
                  tril_ref, triu_ref, out_ref, cn_sc, m_sc, *, tc):
    L = MLSTM_CHUNK

    @pl.when(pl.program_id(1) == 0)
    def _():
        cn_sc[...] = jnp.zeros(cn_sc.shape, F32)
        m_sc[...] = jnp.full(m_sc.shape, NEG, F32)

    gcol = gcol_ref[...] + bcol_ref[...]
    grow = grow_ref[...] + brow_ref[...]
    hi, lo = _split_hi_lo(_log_sigmoid(gcol))
    bcol_all = _dot(tril_ref[...], hi) + _dot(tril_ref[...], lo)
    hi, lo = _split_hi_lo(_log_sigmoid(grow))
    brow_all = _dot(hi, triu_ref[...]) + _dot(lo, triu_ref[...])

    causal = (lax.broadcasted_iota(jnp.int32, (L, L), 1) <= lax.broadcasted_iota(jnp.int32, (L, L), 0))
    ones = jnp.ones((L, DH), BF16)
    mean_mat = jnp.full((DH, DH), 1.0 / DH, BF16)

    def lane_mean(a):
        a_hi, a_lo = _split_hi_lo(a)
        return _dot(a_hi, mean_mat) + _dot(a_lo, mean_mat)

    for c in range(tc // L):
        rs = slice(c * L, (c + 1) * L)
        for h in range(NH):
            cs = slice(h * DH, (h + 1) * DH)
            qh = q_ref[rs, cs]
            kh = k_ref[rs, cs]
            v_aug = jnp.concatenate([v_ref[rs, cs], ones], axis=1)
            ig_col = gcol[rs, GD_I + h:GD_I + h + 1]
            b_col = bcol_all[rs, GD_F + h:GD_F + h + 1]
            ig_row = grow[h:h + 1, rs]
            b_row = brow_all[NH + h:NH + h + 1, rs]
            cn_prev = cn_sc[h]
            m_prev = m_sc[h][:, 0:1]

            d_log = jnp.where(causal, b_col + (ig_row - b_row), NEG)
            m_intra = jnp.max(d_log, axis=-1, keepdims=True)
            m_inter = b_col + m_prev
            m_t = jnp.maximum(m_inter, m_intra)
            w_inter = jnp.exp(m_inter - m_t)
            qk = _dot_nt(qh, kh) * jnp.exp(d_log - (m_t - math.log(SCALE)))
            intra = _dot(qk.astype(BF16), v_aug)
            inter = _dot(qh, cn_prev.astype(BF16))
            num = intra[:, 0:DH] + w_inter * inter[:, 0:DH]
            den = intra[:, DH:2 * DH] + w_inter * inter[:, DH:2 * DH]
            hh = num / jnp.maximum(jnp.abs(den), jnp.exp(-m_t))
            hh = _sigmoid(og_ref[rs, cs].astype(F32)) * hh
            dlt = hh - lane_mean(hh)
            hn = dlt * lax.rsqrt(lane_mean(jnp.square(dlt)) + LN_EPS) * ng_ref[:, cs]
            out_ref[rs, cs] = (hn * _silu(z_ref[rs, cs].astype(F32))).astype(out_ref.dtype)

            b_last = b_row[:, L - 1:L]
            m_loc = jnp.max(b_last - b_row + ig_row, axis=-1, keepdims=True)
            e_col = jnp.exp(b_last - b_col + ig_col - m_loc)
            ek = ((e_col * SCALE) * kh.astype(F32)).astype(BF16)
            g_cn = lax.dot_general(ek, v_aug, TN, preferred_element_type=F32)
            m_new = jnp.maximum(b_last + m_prev, m_loc)
            cn_sc[h] = jnp.exp(b_last + m_prev - m_new) * cn_prev + jnp.exp(m_loc - m_new) * g_cn
            m_sc[h] = jnp.broadcast_to(m_new, (1, LANES))


def _mlstm(gb, gc, gd, gd_t, bias_col, bias_row, norm_g, tril, triu, batch, seq):
    tc = MLSTM_ROWS
    nt = seq // tc
    row = lambda col: (lambda b, j: (b * nt + j, col))
    full = lambda a: pl.BlockSpec(a.shape, lambda b, j: (0,) * a.ndim)
    return pl.pallas_call(
        functools.partial(_mlstm_kernel, tc=tc),
        out_shape=jax.ShapeDtypeStruct((batch * seq, GROUP), BF16),
        grid=(batch, nt),
        in_specs=[pl.BlockSpec((tc, GROUP), row(1)), pl.BlockSpec((tc, GROUP), row(2)),
                  pl.BlockSpec((tc, GROUP), row(3)),
                  pl.BlockSpec((tc, LANES), row(0)),
                  pl.BlockSpec((2 * NH, tc), lambda b, j: (GD_I // (2 * NH), b * nt + j)),
                  full(bias_col), full(bias_row),
                  pl.BlockSpec((tc, GROUP), row(GC0 + 1)), pl.BlockSpec((tc, GROUP), row(GC0 + 2)),
                  full(norm_g), full(tril), full(triu)],
        out_specs=pl.BlockSpec((tc, GROUP), row(0)),
        scratch_shapes=[pltpu.VMEM((NH, DH, 2 * DH), F32), pltpu.VMEM((NH, 1, LANES), F32)],
        compiler_params=_cparams(("parallel", "arbitrary")),
        name="mlstm",
    )(gb, gb, gb, gd, gd_t, bias_col, bias_row, gc, gc, norm_g, tril, triu)


def _lru_kernel(x_ref, z_ref, cw_ref, cb_ref, gw_ref, gb_ref, lam_ref, out_ref, xbuf, h_sc, *, tl):
    pad = 8

    @pl.when(pl.program_id(1) == 0)
    def _():
        xbuf[0:pad, :] = jnp.zeros((pad, GROUP), F32)
        h_sc[...] = jnp.zeros(h_sc.shape, F32)

    x = x_ref[...].astype(F32)
    xbuf[pad:pad + tl, :] = x
    u = cw_ref[3:4, :] * x + cb_ref[...]
    for w in range(3):
        u = u + cw_ref[w:w + 1, :] * xbuf[pad - 3 + w:pad - 3 + w + tl, :]
    xbuf[0:pad, :] = x[tl - pad:tl, :]

    ub = u.astype(BF16)
    pre = []
    for gi in range(2):
        pre.append(jnp.concatenate(
            [_dot(ub[:, n * DH:(n + 1) * DH], gw_ref[gi, n]) for n in range(NH)], axis=1) + gb_ref[gi:gi + 1, :])
    r = _sigmoid(pre[0])
    ig = _sigmoid(pre[1])
    neg_lam = -lam_ref[...]
    softplus = jnp.maximum(neg_lam, 0.0) + jnp.log(1.0 + jnp.exp(-jnp.abs(neg_lam)))
    a = jnp.exp(-LRU_C * r * softplus)
    bx = jnp.sqrt(1.0 - a * a) * (ig * u)

    sub = lax.broadcasted_iota(jnp.int32, (tl, GROUP), 0) & (SUB - 1)
    d = 1
    while d < SUB:
        keep = sub >= d
        a_sh = jnp.where(keep, pltpu.roll(a, d, 0), 1.0)
        b_sh = jnp.where(keep, pltpu.roll(bx, d, 0), 0.0)
        bx = a * b_sh + bx
        a = a * a_sh
        d *= 2
    carry = jnp.broadcast_to(h_sc[...], (SUB, GROUP))
    groups = []
    for g in range(tl // SUB):
        hg = a[g * SUB:(g + 1) * SUB] * carry + bx[g * SUB:(g + 1) * SUB]
        groups.append(hg)
        carry = jnp.broadcast_to(hg[SUB - 1:SUB], (SUB, GROUP))
    hseq = jnp.concatenate(groups, axis=0)
    h_sc[...] = hseq[tl - 1:tl, :]
    out_ref[...] = (hseq * _silu(z_ref[...].astype(F32))).astype(out_ref.dtype)


def _lru(gc, conv_w, conv_b, gate_w, gate_b, lam, batch, seq):
    tl = LRU_ROWS
    nt = seq // tl
    row = lambda col: (lambda b, j: (b * nt + j, col))
    full = lambda a: pl.BlockSpec(a.shape, lambda b, j: (0,) * a.ndim)
    return pl.pallas_call(
        functools.partial(_lru_kernel, tl=tl),
        out_shape=jax.ShapeDtypeStruct((batch * seq, GROUP), BF16),
        grid=(batch, nt),
        in_specs=[pl.BlockSpec((tl, GROUP), row(GC0 + 3)), pl.BlockSpec((tl, GROUP), row(GC0 + 4)),
                  full(conv_w), full(conv_b), full(gate_w), full(gate_b), full(lam)],
        out_specs=pl.BlockSpec((tl, GROUP), row(0)),
        scratch_shapes=[pltpu.VMEM((tl + 8, GROUP), F32), pltpu.VMEM((1, GROUP), F32)],
        compiler_params=_cparams(("parallel", "arbitrary")),
        name="rglru",
    )(gc, gc, conv_w, conv_b, gate_w, gate_b, lam)


def _moba_tile(i, q_ref, k_ref, z_ref, o_ref, kmean_sc, vt_sc, *, seq):
    tq = MOBA_BLOCK
    nb = seq // MOBA_BLOCK
    ncand = MOBA_CAND_ROWS
    nk = (i + 1) * tq
    past = lax.broadcasted_iota(jnp.int32, (ncand, tq), 0) < i
    pw = 2 * tq
    causal = (lax.broadcasted_iota(jnp.int32, (tq, pw), 0)
              <= (lax.broadcasted_iota(jnp.int32, (tq, pw), 1) & (tq - 1)))
    zeros = jnp.zeros((tq, DH), BF16)
    for h0 in range(0, NH, 2):
        pair = slice(h0 * DH, (h0 + 2) * DH)
        qa = q_ref[:, h0 * DH:(h0 + 1) * DH]
        qb = q_ref[:, (h0 + 1) * DH:(h0 + 2) * DH]
        q_pair = jnp.concatenate([jnp.concatenate([_prescale(qa), zeros], axis=1),
                                  jnp.concatenate([zeros, _prescale(qb)], axis=1)], axis=0)
        s = _dot_nt(k_ref[0:nk, pair], q_pair)
        parts = [jnp.where(causal, s[i * tq:nk], NEG)]
        if i > 0:
            bias = []
            for h, qh in ((h0, qa), (h0 + 1, qb)):
                gate = _dot_nt(kmean_sc[h].astype(BF16), qh)[0:ncand]
                score = jnp.where(past, gate, NEG)
                sel = jnp.logical_and(_topk_rows(score, min(MOBA_TOPK, nb)), past)
                bias.append(jnp.where(sel, 0.0, NEG))
            bias = jnp.concatenate(bias, axis=1)
            parts = [s[j * tq:(j + 1) * tq] + bias[j:j + 1] for j in range(i)] + parts
        p, inv = _softmax_rows(jnp.concatenate(parts, axis=0))
        for n, h in enumerate((h0, h0 + 1)):
            cs = slice(h * DH, (h + 1) * DH)
            ls = slice(n * tq, (n + 1) * tq)
            o = jnp.transpose(_dot(vt_sc[h, :, 0:nk], p[:, ls]) * inv[:, ls])
            o_ref[:, cs] = (o * _silu(z_ref[:, cs].astype(F32))).astype(o_ref.dtype)


def _moba_kernel(q_ref, k_ref, v_ref, z_ref, o_ref, kmean_sc, vt_sc, *, seq):
    i = pl.program_id(1)
    tq = MOBA_BLOCK
    nb = seq // MOBA_BLOCK

    @pl.when(i == 0)
    def _():
        kmean_sc[...] = jnp.zeros(kmean_sc.shape, F32)
        for h in range(NH):
            cs = slice(h * DH, (h + 1) * DH)
            kmean_sc[h, 0:nb, :] = jnp.mean(k_ref[:, cs].astype(F32).reshape(nb, MOBA_BLOCK, DH), axis=1)
            for c in range(nb):
                rows = slice(c * tq, (c + 1) * tq)
                vt_sc[h, :, rows] = jnp.transpose(v_ref[rows, cs].astype(F32)).astype(BF16)

    for c in range(nb):
        pl.when(i == c)(functools.partial(_moba_tile, c, q_ref, k_ref, z_ref, o_ref, kmean_sc, vt_sc, seq=seq))


def _moba(ga, gb, gc, batch, seq):
    tq = MOBA_BLOCK
    nq = seq // tq
    assert nq <= MOBA_CAND_ROWS
    row = lambda col: (lambda b, i: (b * nq + i, col))
    return pl.pallas_call(
        functools.partial(_moba_kernel, seq=seq),
        out_shape=jax.ShapeDtypeStruct((batch * seq, GROUP), BF16),
        grid=(batch, nq),
        in_specs=[pl.BlockSpec((tq, GROUP), row(2)),
                  pl.BlockSpec((seq, GROUP), lambda b, i: (b, 3)),
                  pl.BlockSpec((seq, GROUP), lambda b, i: (b, 4)),
                  pl.BlockSpec((tq, GROUP), row(GC0 + 5))],
        out_specs=pl.BlockSpec((tq, GROUP), row(0)),
        scratch_shapes=[pltpu.VMEM((NH, LANES, DH), F32), pltpu.VMEM((NH, DH, seq), BF16)],
        compiler_params=_cparams(("parallel", "arbitrary")),
        name="moba",
    )(ga, ga, gb, gc)


def _out_kernel(ya_ref, yb_ref, yc_ref, yd_ref, w_ref, x_ref, g_ref, b_ref, o_ref, ob_ref):
    acc = DEEPNORM_ALPHA * x_ref[...]
    for p, y_ref in enumerate((ya_ref, yb_ref, yc_ref, yd_ref)):
        acc = acc + _dot(y_ref[...], w_ref[p * GROUP:(p + 1) * GROUP, :])
    mu = jnp.mean(acc, axis=-1, keepdims=True)
    var = jnp.mean(jnp.square(acc - mu), axis=-1, keepdims=True)
    y = (acc - mu) * lax.rsqrt(var + LN_EPS) * g_ref[...] + b_ref[...]
    o_ref[...] = y
    ob_ref[...] = y.astype(BF16)


def _out_proj(ys, w_out, x, ln_g, ln_b):
    m, d = x.shape
    tm = OUT_ROWS
    yspec = pl.BlockSpec((tm, GROUP), lambda i: (i, 0))
    full = lambda a: pl.BlockSpec(a.shape, lambda i: (0,) * a.ndim)
    xspec = pl.BlockSpec((tm, d), lambda i: (i, 0))
    return pl.pallas_call(
        _out_kernel,
        out_shape=(jax.ShapeDtypeStruct((m, d), F32), jax.ShapeDtypeStruct((m, d), BF16)),
        grid=(m // tm,),
        in_specs=[yspec] * 4 + [full(w_out), xspec, full(ln_g), full(ln_b)],
        out_specs=(xspec, xspec),
        compiler_params=_cparams(("parallel",)),
        name="out_proj_ln",
    )(*ys, w_out, x, ln_g, ln_b)


_GROUPS = (
    (('nsa_q', 0, 512), ('nsa_kv', 0, 384), ('gates', 0, LANES), ('moba_qkv', 0, 1024)),
    (('nsa_kv', 384, 384), (None, 0, 128), ('mlstm_qkv', 0, 1536), ('moba_qkv', 1024, 512)),
    (('nsa_z', 0, 512), ('mlstm_o', 0, 512), ('mlstm_z', 0, 512), ('lru_x', 0, 512), ('lru_z', 0, 512),
     ('moba_z', 0, 512)),
)
_GATE_PIECES = (('nsa_gate', 0, 12), (None, 0, 4), ('mlstm_if', 0, 8), (None, 0, LANES - 24))


def _copy_rows(w_ref, pieces, o_ref, dst):
    for name, off, width in pieces:
        if name is None:
            o_ref[dst:dst + width, :] = jnp.zeros((width, o_ref.shape[1]), o_ref.dtype)
        else:
            src = _OFF[name] + off
            o_ref[dst:dst + width, :] = w_ref[src:src + width, :].astype(o_ref.dtype)
        dst += width


def _regroup_kernel(w_ref, o_ref, gate_sc):
    dst = 0
    for piece in sum(_GROUPS, ()):
        if piece[0] == 'gates':
            _copy_rows(w_ref, _GATE_PIECES, gate_sc, 0)
            o_ref[dst:dst + LANES, :] = gate_sc[...].astype(o_ref.dtype)
        else:
            _copy_rows(w_ref, (piece,), o_ref, dst)
        dst += piece[2]


def _regroup_w_in(w_in):
    depth, d, n_in = w_in.shape
    wt = jnp.transpose(w_in, (0, 2, 1))
    tk = REGROUP_COLS
    n_all = sum(p[2] for pieces in _GROUPS for p in pieces)
    return pl.pallas_call(
        _regroup_kernel,
        out_shape=jax.ShapeDtypeStruct((depth, n_all, d), BF16),
        grid=(depth, d // tk),
        in_specs=[pl.BlockSpec((None, n_in, tk), lambda l, i: (l, 0, i))],
        out_specs=pl.BlockSpec((None, n_all, tk), lambda l, i: (l, 0, i)),
        scratch_shapes=[pltpu.VMEM((LANES, tk), F32)],
        compiler_params=_cparams(("parallel", "parallel")),
        name="regroup_w_in",
    )(wt)


def _rope_tables(seq):
    half = ROT_DIM // 2
    inv_freq = jnp.power(ROPE_THETA, -jnp.arange(half, dtype=F32) * (2.0 / ROT_DIM))
    ang = jnp.arange(seq, dtype=jnp.int32).astype(F32)[:, None] * inv_freq[None, :]
    cos, sin = jnp.cos(ang), jnp.sin(ang)
    ones = jnp.ones((seq, DH - ROT_DIM), F32)
    lane = jnp.arange(DH)
    partner = jnp.where(lane < half, lane + half, lane - half)
    swap = (lane[:, None] == partner[None, :]).astype(BF16)
    return (jnp.concatenate([cos, cos, ones], axis=1),
            jnp.concatenate([-sin, sin, 0.0 * ones], axis=1), swap)


def _cmp_overlap_t(seq):
    n_cmp = seq // CMP_STRIDE - 1
    n_sel = seq // SEL_BLOCK
    cs = jnp.arange(LANES)[None, :] * CMP_STRIDE
    ss = jnp.arange(LANES)[:, None] * SEL_BLOCK
    ov = (cs < ss + SEL_BLOCK) & (cs + CMP_BLOCK > ss)
    ov = ov & (jnp.arange(LANES)[None, :] < n_cmp) & (jnp.arange(LANES)[:, None] < n_sel)
    return ov.astype(BF16)


def _chunk_tri(tc):
    r = jnp.arange(tc)
    same = (r[:, None] // MLSTM_CHUNK) == (r[None, :] // MLSTM_CHUNK)
    tril = (same & (r[None, :] <= r[:, None])).astype(BF16)
    return tril, tril.T


def _layer(x, xb, batch, seq, consts, w_all, layer, cmp_w1, cmp_w2, cmp_pe, i_bias, f_bias, norm_g,
           conv_w, conv_b, gate_w, gate_b, lam, w_out, ln_g, ln_b):
    rope, ov_t, tril, triu = consts
    ga, gbc, gd, gd_t = _project(xb, w_all, layer, rope, seq, min(PROJ_ROWS, seq), PROJ_COLS)
    gb = gc = gbc

    nb = seq // CMP_STRIDE
    tk = ga[:, 4 * DH:5 * DH].reshape(batch, nb, CMP_STRIDE * DH)
    tv = gb[:, 0:DH].reshape(batch, nb, CMP_STRIDE * DH)
    pe = cmp_pe.reshape(2, 2, CMP_STRIDE * DH)
    kc, vc = _compress(tk, tv, cmp_w1.astype(BF16), cmp_w2.astype(BF16), pe)
    y_a = _nsa(ga, gb, gc, gd_t, kc, vc, ov_t, batch, seq)

    bias = jnp.concatenate([i_bias, f_bias])
    bias_col = jnp.zeros((1, LANES), F32).at[0, GD_I:GD_I + 2 * NH].set(bias)
    bias_row = bias[:, None]
    y_b = _mlstm(gb, gc, gd, gd_t, bias_col, bias_row, norm_g[None, :], tril, triu, batch, seq)

    y_c = _lru(gc, conv_w, conv_b[None, :], gate_w.astype(BF16), gate_b, lam[None, :], batch, seq)

    y_d = _moba(ga, gb, gc, batch, seq)

    return _out_proj((y_a, y_b, y_c, y_d), w_out.astype(BF16), x, ln_g[None, :], ln_b[None, :])


def kernel(x, w_in, nsa_cmp_w1, nsa_cmp_w2, nsa_cmp_pe, mlstm_i_bias, mlstm_f_bias, mlstm_norm_g,
           lru_conv_w, lru_conv_b, lru_gate_w, lru_gate_b, lru_lambda, w_out, ln_g, ln_b):
    batch, seq, d = x.shape
    tril, triu = _chunk_tri(MLSTM_ROWS)
    consts = (_rope_tables(seq), _cmp_overlap_t(seq), tril, triu)
    xf = x.reshape(batch * seq, d)
    xb = xf.astype(BF16)
    w_all = _regroup_w_in(w_in)
    for l in range(w_in.shape[0]):
        xf, xb = _layer(xf, xb, batch, seq, consts, w_all, l, nsa_cmp_w1[l], nsa_cmp_w2[l],
                        nsa_cmp_pe[l], mlstm_i_bias[l], mlstm_f_bias[l], mlstm_norm_g[l], lru_conv_w[l],
                        lru_conv_b[l], lru_gate_w[l], lru_gate_b[l], lru_lambda[l], w_out[l], ln_g[l], ln_b[l])
    return xf.reshape(batch, seq, d)
```

```python
import functools
import math

import jax
import jax.numpy as jnp
from jax import lax
from jax.experimental import pallas as pl
from jax.experimental.pallas import tpu as pltpu

F32 = jnp.float32
BF16 = jnp.bfloat16

DEPTH = 2
GROUP = 512
DH = 128
NH = 4
ROT_DIM = 32
ROPE_THETA = 500000.0

CMP_BLOCK = 32
CMP_STRIDE = 16
SEL_BLOCK = 64
SEL_TOPK = 8
WIN = 256
FORCE_SCORE = 1e9

MLSTM_CHUNK = 512
LRU_C = 8.0
MOBA_BLOCK = 256
MOBA_TOPK = 3

DEEPNORM_ALPHA = (2 * DEPTH) ** 0.25
NEG = -1e30
LN_EPS = 1e-5
SCALE = DH ** -0.5
EXP2_SCALE = SCALE * math.log2(math.e)

LANES = 128
SUB = 8
VMEM_LIMIT = 56 * 1024 * 1024

PROJ_ROWS = 2048
PROJ_COLS = 512
REGROUP_COLS = 256
NSA_ROWS = 256
NSA_SEL_ROWS = 32
MOBA_CAND_ROWS = SUB
LRU_ROWS = 512
MLSTM_ROWS = 2 * MLSTM_CHUNK
OUT_ROWS = 512

_OFF = {}
_o = 0
for _name, _w in (('nsa_q', 512), ('nsa_kv', 768), ('nsa_gate', 12), ('nsa_z', 512), ('mlstm_qkv', 1536),
                  ('mlstm_if', 8), ('mlstm_o', 512), ('mlstm_z', 512), ('lru_x', 512), ('lru_z', 512),
                  ('moba_qkv', 1536), ('moba_z', 512)):
    _OFF[_name] = _o
    _o += _w

GD_I = 16
GD_F = 20
GC0 = 5
GD_COL = 7 * 128

NT = (((1,), (1,)), ((), ()))
TN = (((0,), (0,)), ((), ()))


def _cparams(sem):
    return pltpu.CompilerParams(dimension_semantics=sem, vmem_limit_bytes=VMEM_LIMIT)


def _dot(a, b):
    return jnp.dot(a, b, preferred_element_type=F32)


def _dot_nt(a, b):
    return lax.dot_general(a, b, NT, preferred_element_type=F32)


def _split_hi_lo(a):
    hi = a.astype(BF16)
    lo = (a - hi.astype(F32)).astype(BF16)
    return hi, lo


def _sigmoid(x):
    return 0.5 * jnp.tanh(0.5 * x) + 0.5


def _silu(x):
    return x * _sigmoid(x)


def _log_sigmoid(x):
    return jnp.minimum(x, 0.0) - jnp.log(1.0 + jnp.exp(-jnp.abs(x)))


def _proj_kernel(x_ref, w_ref, cos_ref, sin_ref, swap_ref, ga_ref, gbc_ref, gd_ref, gdt_ref, *, n_rot):
    j = pl.program_id(1)
    acc = _dot_nt(x_ref[...], w_ref[...])
    gbc_ref[...] = acc.astype(gbc_ref.dtype)

    @pl.when(j < n_rot)
    def _():
        c = cos_ref[...]
        s = sin_ref[...]
        for h in range(acc.shape[1] // DH):
            t = acc[:, h * DH:(h + 1) * DH]
            swapped = _dot(t.astype(BF16), swap_ref[...])
            ga_ref[:, h * DH:(h + 1) * DH] = (t * c + swapped * s).astype(ga_ref.dtype)

    @pl.when(j == GD_COL // acc.shape[1])
    def _():
        gd = acc[:, GD_COL % acc.shape[1]:GD_COL % acc.shape[1] + LANES]
        gd_ref[...] = gd
        gdt_ref[...] = jnp.transpose(gd)


def _project(xb, w_all, layer, rope, seq, tm, tn):
    m, k = xb.shape
    widths = [sum(p[2] for p in pieces) for pieces in _GROUPS]
    assert tn == GROUP and widths[1] == GC0 * GROUP
    n_rot = widths[0] // tn
    nrep = seq // tm
    rope_spec = pl.BlockSpec((tm, DH), lambda i, j: (i % nrep, 0))
    cos, sin, swap = rope
    return pl.pallas_call(
        functools.partial(_proj_kernel, n_rot=n_rot),
        out_shape=(jax.ShapeDtypeStruct((m, widths[0]), BF16), jax.ShapeDtypeStruct((m, widths[1] + widths[2]), BF16),
                   jax.ShapeDtypeStruct((m, LANES), F32), jax.ShapeDtypeStruct((LANES, m), F32)),
        grid=(m // tm, sum(widths) // tn),
        in_specs=[pl.BlockSpec((tm, k), lambda i, j: (i, 0)), pl.BlockSpec((None, tn, k), lambda i, j: (layer, j, 0)),
                  rope_spec, rope_spec, pl.BlockSpec((DH, DH), lambda i, j: (0, 0))],
        out_specs=(pl.BlockSpec((tm, tn), lambda i, j: (i, jnp.minimum(j, n_rot - 1))),
                   pl.BlockSpec((tm, tn), lambda i, j: (i, jnp.maximum(j - n_rot, 0))),
                   pl.BlockSpec((tm, LANES), lambda i, j: (i, 0)), pl.BlockSpec((LANES, tm), lambda i, j: (0, i))),
        compiler_params=_cparams(("parallel", "arbitrary")),
        name="in_proj",
    )(xb, w_all, cos, sin, swap)


def _compress_kernel(tk_ref, tv_ref, w1_ref, w2_ref, pe_ref, kc_ref, vc_ref, *, nb):
    half = (CMP_BLOCK // 2) * DH
    for idx, (t_ref, o_ref) in enumerate(((tk_ref, kc_ref), (tv_ref, vc_ref))):
        t = t_ref[...].astype(F32)
        lo = (t + pe_ref[idx, 0:1, :]).astype(BF16)
        hi = (t + pe_ref[idx, 1:2, :]).astype(BF16)
        a = _dot(lo, w1_ref[idx, 0:half, :])
        b = _dot(hi, w1_ref[idx, half:2 * half, :])
        hid = _silu(a + pltpu.roll(b, nb - 1, 0))
        out = _dot(hid.astype(BF16), w2_ref[idx])
        o_ref[...] = jnp.zeros(o_ref.shape, o_ref.dtype)
        o_ref[0:nb, :] = out.astype(o_ref.dtype)


def _compress(tk, tv, w1, w2, pe):
    b, nb, width = tk.shape
    blk = pl.BlockSpec((None, nb, width), lambda i: (i, 0, 0))
    full = lambda a: pl.BlockSpec(a.shape, lambda i: (0,) * a.ndim)
    out = jax.ShapeDtypeStruct((b, LANES, DH), BF16)
    return pl.pallas_call(
        functools.partial(_compress_kernel, nb=nb),
        out_shape=(out, out),
        grid=(b,),
        in_specs=[blk, blk, full(w1), full(w2), full(pe)],
        out_specs=(pl.BlockSpec((None, LANES, DH), lambda i: (i, 0, 0)),) * 2,
        compiler_params=_cparams(("parallel",)),
        name="nsa_compress",
    )(tk, tv, w1, w2, pe)


def _topk_rows(score, k_top, n_live):
    rowi = lax.broadcasted_iota(jnp.int32, score.shape, 0)
    rank = jnp.zeros(score.shape, F32)
    for k in range(n_live):
        sk = score[k:k + 1, :]
        beats = jnp.logical_or(sk > score, jnp.logical_and(sk == score, rowi > k))
        rank = rank + jnp.where(beats, 1.0, 0.0)
    return rank < k_top


def _prescale(q):
    return (q.astype(F32) * EXP2_SCALE).astype(q.dtype)


def _softmax_rows(s):
    p = jnp.exp2(s - jnp.max(s, axis=0, keepdims=True))
    return p.astype(BF16), 1.0 / jnp.sum(p, axis=0, keepdims=True)


def _nsa_tile(i, q_ref, kc_ref, vc_ref, ks_ref, kw_ref, gt_ref, z_ref, ovt_ref, diagb_ref, winb_ref,
              o_ref, vst_sc, vwt_sc, *, seq, tq):
    width = NH * tq
    n_cmp = seq // CMP_STRIDE - 1
    n_sel = seq // SEL_BLOCK
    n_rows = NSA_SEL_ROWS
    q = q_ref[...]
    qs = jnp.concatenate([q[:, h * DH:(h + 1) * DH] for h in range(NH)], axis=0)
    qs = _prescale(qs)

    rown = lax.broadcasted_iota(jnp.int32, (LANES, width), 0)
    tok = i * tq + (lax.broadcasted_iota(jnp.int32, (LANES, width), 1) & (tq - 1))
    mask_c = jnp.logical_and(rown * CMP_STRIDE + (CMP_BLOCK - 1) <= tok, rown < n_cmp)
    s = jnp.where(mask_c, _dot_nt(kc_ref[...], qs), NEG)
    e = jnp.exp2(s - jnp.max(s, axis=0, keepdims=True))
    p_c = jnp.where(mask_c, e, 0.0) * (1.0 / jnp.sum(e, axis=0, keepdims=True))
    o_c = lax.dot_general(vc_ref[...], p_c.astype(BF16), TN, preferred_element_type=F32)

    p_sum = p_c[:, 0:tq]
    for h in range(1, NH):
        p_sum = p_sum + p_c[:, h * tq:(h + 1) * tq]
    p_hi, p_lo = _split_hi_lo(p_sum)
    imp = (_dot(ovt_ref[...], p_hi) + _dot(ovt_ref[...], p_lo))[0:n_rows]
    rowj = lax.broadcasted_iota(jnp.int32, (n_rows, tq), 0)
    t_q = i * tq + lax.broadcasted_iota(jnp.int32, (n_rows, tq), 1)
    cur = t_q >> (SEL_BLOCK.bit_length() - 1)
    forced = jnp.logical_or(rowj == 0, jnp.logical_or(rowj == cur, rowj == cur - 1))
    valid = rowj * SEL_BLOCK <= t_q
    score = jnp.where(forced, FORCE_SCORE, jnp.where(valid, imp, NEG))
    score = jnp.where(rowj < n_sel, score, -jnp.inf)
    nk = (i + 1) * tq
    n_live = nk // SEL_BLOCK
    live_rows = -(-n_live // SUB) * SUB
    bias = jnp.where(_topk_rows(score[0:live_rows], min(SEL_TOPK, n_sel), n_live), 0.0, NEG)
    bias = jnp.concatenate([bias] * NH, axis=1)

    s = _dot_nt(ks_ref[0:nk, :], qs)
    parts = [s[j * SEL_BLOCK:(j + 1) * SEL_BLOCK] + bias[j:j + 1] for j in range(nk // SEL_BLOCK)]
    own = jnp.concatenate(parts[i * tq // SEL_BLOCK:], axis=0) + diagb_ref[...]
    p, inv = _softmax_rows(jnp.concatenate(parts[:i * tq // SEL_BLOCK] + [own], axis=0))
    o_s = _dot(vst_sc[:, 0:nk], p) * inv

    n_wblk = winb_ref.shape[0]
    blocks = [b for b in range(i - n_wblk + 1, i + 1) if b >= 0]
    parts = [_dot_nt(kw_ref[b * tq:(b + 1) * tq, :], qs) + winb_ref[b - i + n_wblk - 1] for b in blocks]
    p, inv = _softmax_rows(jnp.concatenate(parts, axis=0))
    o_w = _dot(vwt_sc[:, blocks[0] * tq:nk], p) * inv

    g = _sigmoid(gt_ref[0:GD_I, :])
    for h in range(NH):
        ls = slice(h * tq, (h + 1) * tq)
        cs = slice(h * DH, (h + 1) * DH)
        mix = (g[3 * h:3 * h + 1] * o_c[:, ls] + g[3 * h + 1:3 * h + 2] * o_s[:, ls]
               + g[3 * h + 2:3 * h + 3] * o_w[:, ls])
        o_ref[:, cs] = (jnp.transpose(mix) * _silu(z_ref[:, cs].astype(F32))).astype(o_ref.dtype)


def _nsa_kernel(q_ref, kc_ref, vc_ref, ks_ref, vs_ref, kw_ref, vw_ref, gt_ref, z_ref, ovt_ref, diagb_ref,
                winb_ref, o_ref, vst_sc, vwt_sc, *, seq, tq):
    i = pl.program_id(1)

    @pl.when(i == 0)
    def _():
        for c in range(seq // tq):
            cols = slice(c * tq, (c + 1) * tq)
            vst_sc[:, cols] = jnp.transpose(vs_ref[cols, :].astype(F32)).astype(BF16)
            vwt_sc[:, cols] = jnp.transpose(vw_ref[cols, :].astype(F32)).astype(BF16)

    for c in range(seq // tq):
        pl.when(i == c)(functools.partial(
            _nsa_tile, c, q_ref, kc_ref, vc_ref, ks_ref, kw_ref, gt_ref, z_ref, ovt_ref, diagb_ref, winb_ref,
            o_ref, vst_sc, vwt_sc, seq=seq, tq=tq))


def _nsa_bias_tables(tq):
    t = jnp.arange(NH * tq)[None, :] % tq
    r = jnp.arange(tq)[:, None]
    diag = jnp.where(r <= t, 0.0, NEG)
    n_wblk = WIN // tq + 1
    win = []
    for b in range(n_wblk):
        diff = t - (r + (b - (n_wblk - 1)) * tq)
        win.append(jnp.where((diff >= 0) & (diff < WIN), 0.0, NEG))
    return diag.astype(F32), jnp.stack(win).astype(F32)


def _nsa(ga, gb, gc, gd_t, kc, vc, ov_t, batch, seq):
    tq = NSA_ROWS
    nq = seq // tq
    assert seq // SEL_BLOCK <= NSA_SEL_ROWS and tq % SEL_BLOCK == 0
    row = lambda b, i: (b * nq + i, 0)
    kv = lambda col: pl.BlockSpec((seq, DH), lambda b, i: (b, col))
    cmp_spec = pl.BlockSpec((None, LANES, DH), lambda b, i: (b, 0, 0))
    full = lambda a: pl.BlockSpec(a.shape, lambda b, i: (0,) * a.ndim)
    diag_b, win_b = _nsa_bias_tables(tq)
    return pl.pallas_call(
        functools.partial(_nsa_kernel, seq=seq, tq=tq),
        out_shape=jax.ShapeDtypeStruct((batch * seq, GROUP), BF16),
        grid=(batch, nq),
        in_specs=[pl.BlockSpec((tq, GROUP), row), cmp_spec, cmp_spec,
                  kv(5), kv(1), kv(6), kv(2),
                  pl.BlockSpec((LANES, tq), lambda b, i: (0, b * nq + i)),
                  pl.BlockSpec((tq, GROUP), lambda b, i: (b * nq + i, GC0)), full(ov_t), full(diag_b), full(win_b)],
        out_specs=pl.BlockSpec((tq, GROUP), row),
        scratch_shapes=[pltpu.VMEM((DH, seq), BF16), pltpu.VMEM((DH, seq), BF16)],
        compiler_params=_cparams(("parallel", "arbitrary")),
        name="nsa_attention",
    )(ga, kc, vc, ga, gb, ga, gb, gd_t, gc, ov_t, diag_b, win_b)


def _mlstm_kernel(q_ref, k_ref, v_ref, gcol_ref, grow_ref, bcol_ref, brow_ref, og_ref, z_ref, ng_ref,
                  tril_ref, triu_ref, out_ref, cn_sc, m_sc, *, tc):
    L = MLSTM_CHUNK

    @pl.when(pl.program_id(1) == 0)
    def _():
        cn_sc[...] = jnp.zeros(cn_sc.shape, F32)
        m_sc[...] = jnp.full(m_sc.shape, NEG, F32)

    gcol = gcol_ref[...] + bcol_ref[...]
    grow = grow_ref[...] + brow_ref[...]
    hi, lo = _split_hi_lo(_log_sigmoid(gcol))
    bcol_all = _dot(tril_ref[...], hi) + _dot(tril_ref[...], lo)
    hi, lo = _split_hi_lo(_log_sigmoid(grow))
    brow_all = _dot(hi, triu_ref[...]) + _dot(lo, triu_ref[...])

    causal = (lax.broadcasted_iota(jnp.int32, (L, L), 1) <= lax.broadcasted_iota(jnp.int32, (L, L), 0))
    ones = jnp.ones((L, DH), BF16)
    mean_mat = jnp.full((DH, DH), 1.0 / DH, BF16)

    def lane_mean(a):
        a_hi, a_lo = _split_hi_lo(a)
        return _dot(a_hi, mean_mat) + _dot(a_lo, mean_mat)

    for c in range(tc // L):
        rs = slice(c * L, (c + 1) * L)
        for h in range(NH):
            cs = slice(h * DH, (h + 1) * DH)
            qh = q_ref[rs, cs]
            kh = k_ref[rs, cs]
            v_aug = jnp.concatenate([v_ref[rs, cs], ones], axis=1)
            ig_col = gcol[rs, GD_I + h:GD_I + h + 1]
            b_col = bcol_all[rs, GD_F + h:GD_F + h + 1]
            ig_row = grow[h:h + 1, rs]
            b_row = brow_all[NH + h:NH + h + 1, rs]
            cn_prev = cn_sc[h]
            m_prev = m_sc[h][:, 0:1]

            d_log = jnp.where(causal, b_col + (ig_row - b_row), NEG)
            m_intra = jnp.max(d_log, axis=-1, keepdims=True)
            m_inter = b_col + m_prev
            m_t = jnp.maximum(m_inter, m_intra)
            w_inter = jnp.exp(m_inter - m_t)
            qk = _dot_nt(qh, kh) * jnp.exp(d_log - (m_t - math.log(SCALE)))
            intra = _dot(qk.astype(BF16), v_aug)
            inter = _dot(qh, cn_prev.astype(BF16))
            num = intra[:, 0:DH] + w_inter * inter[:, 0:DH]
            den = intra[:, DH:2 * DH] + w_inter * inter[:, DH:2 * DH]
            hh = num / jnp.maximum(jnp.abs(den), jnp.exp(-m_t))
            hh = _sigmoid(og_ref[rs, cs].astype(F32)) * hh
            dlt = hh - lane_mean(hh)
            hn = dlt * lax.rsqrt(lane_mean(jnp.square(dlt)) + LN_EPS) * ng_ref[:, cs]
            out_ref[rs, cs] = (hn * _silu(z_ref[rs, cs].astype(F32))).astype(out_ref.dtype)

            b_last = b_row[:, L - 1:L]
            m_loc = jnp.max(b_last - b_row + ig_row, axis=-1, keepdims=True)
            e_col = jnp.exp(b_last - b_col + ig_col - m_loc)
            ek = ((e_col * SCALE) * kh.astype(F32)).astype(BF16)
            g_cn = lax.dot_general(ek, v_aug, TN, preferred_element_type=F32)
            m_new = jnp.maximum(b_last + m_prev, m_loc)
            cn_sc[h] = jnp.exp(b_last + m_prev - m_new) * cn_prev + jnp.exp(m_loc - m_new) * g_cn
            m_sc[h] = jnp.broadcast_to(m_new, (1, LANES))


def _mlstm(gb, gc, gd, gd_t, bias_col, bias_row, norm_g, tril, triu, batch, seq):
    tc = MLSTM_ROWS
    nt = seq // tc
    row = lambda col: (lambda b, j: (b * nt + j, col))
    full = lambda a: pl.BlockSpec(a.shape, lambda b, j: (0,) * a.ndim)
    return pl.pallas_call(
        functools.partial(_mlstm_kernel, tc=tc),
        out_shape=jax.ShapeDtypeStruct((batch * seq, GROUP), BF16),
        grid=(batch, nt),
        in_specs=[pl.BlockSpec((tc, GROUP), row(1)), pl.BlockSpec((tc, GROUP), row(2)),
                  pl.BlockSpec((tc, GROUP), row(3)),
                  pl.BlockSpec((tc, LANES), row(0)),
                  pl.BlockSpec((2 * NH, tc), lambda b, j: (GD_I // (2 * NH), b * nt + j)),
                  full(bias_col), full(bias_row),
                  pl.BlockSpec((tc, GROUP), row(GC0 + 1)), pl.BlockSpec((tc, GROUP), row(GC0 + 2)),
                  full(norm_g), full(tril), full(triu)],
        out_specs=pl.BlockSpec((tc, GROUP), row(0)),
        scratch_shapes=[pltpu.VMEM((NH, DH, 2 * DH), F32), pltpu.VMEM((NH, 1, LANES), F32)],
        compiler_params=_cparams(("parallel", "arbitrary")),
        name="mlstm",
    )(gb, gb, gb, gd, gd_t, bias_col, bias_row, gc, gc, norm_g, tril, triu)


def _lru_kernel(x_ref, z_ref, cw_ref, cb_ref, gw_ref, gb_ref, lam_ref, out_ref, xbuf, h_sc, *, tl):
    pad = 8

    @pl.when(pl.program_id(1) == 0)
    def _():
        xbuf[0:pad, :] = jnp.zeros((pad, GROUP), F32)
        h_sc[...] = jnp.zeros(h_sc.shape, F32)

    x = x_ref[...].astype(F32)
    xbuf[pad:pad + tl, :] = x
    u = cw_ref[3:4, :] * x + cb_ref[...]
    for w in range(3):
        u = u + cw_ref[w:w + 1, :] * xbuf[pad - 3 + w:pad - 3 + w + tl, :]
    xbuf[0:pad, :] = x[tl - pad:tl, :]

    ub = u.astype(BF16)
    pre = []
    for gi in range(2):
        pre.append(jnp.concatenate(
            [_dot(ub[:, n * DH:(n + 1) * DH], gw_ref[gi, n]) for n in range(NH)], axis=1) + gb_ref[gi:gi + 1, :])
    r = _sigmoid(pre[0])
    ig = _sigmoid(pre[1])
    neg_lam = -lam_ref[...]
    softplus = jnp.maximum(neg_lam, 0.0) + jnp.log(1.0 + jnp.exp(-jnp.abs(neg_lam)))
    a = jnp.exp(-LRU_C * r * softplus)
    bx = jnp.sqrt(1.0 - a * a) * (ig * u)

    sub = lax.broadcasted_iota(jnp.int32, (tl, GROUP), 0) & (SUB - 1)
    d = 1
    while d < SUB:
        keep = sub >= d
        a_sh = jnp.where(keep, pltpu.roll(a, d, 0), 1.0)
        b_sh = jnp.where(keep, pltpu.roll(bx, d, 0), 0.0)
        bx = a * b_sh + bx
        a = a * a_sh
        d *= 2
    carry = jnp.broadcast_to(h_sc[...], (SUB, GROUP))
    groups = []
    for g in range(tl // SUB):
        hg = a[g * SUB:(g + 1) * SUB] * carry + bx[g * SUB:(g + 1) * SUB]
        groups.append(hg)
        carry = jnp.broadcast_to(hg[SUB - 1:SUB], (SUB, GROUP))
    hseq = jnp.concatenate(groups, axis=0)
    h_sc[...] = hseq[tl - 1:tl, :]
    out_ref[...] = (hseq * _silu(z_ref[...].astype(F32))).astype(out_ref.dtype)


def _lru(gc, conv_w, conv_b, gate_w, gate_b, lam, batch, seq):
    tl = LRU_ROWS
    nt = seq // tl
    row = lambda col: (lambda b, j: (b * nt + j, col))
    full = lambda a: pl.BlockSpec(a.shape, lambda b, j: (0,) * a.ndim)
    return pl.pallas_call(
        functools.partial(_lru_kernel, tl=tl),
        out_shape=jax.ShapeDtypeStruct((batch * seq, GROUP), BF16),
        grid=(batch, nt),
        in_specs=[pl.BlockSpec((tl, GROUP), row(GC0 + 3)), pl.BlockSpec((tl, GROUP), row(GC0 + 4)),
                  full(conv_w), full(conv_b), full(gate_w), full(gate_b), full(lam)],
        out_specs=pl.BlockSpec((tl, GROUP), row(0)),
        scratch_shapes=[pltpu.VMEM((tl + 8, GROUP), F32), pltpu.VMEM((1, GROUP), F32)],
        compiler_params=_cparams(("parallel", "arbitrary")),
        name="rglru",
    )(gc, gc, conv_w, conv_b, gate_w, gate_b, lam)


def _moba_tile(i, q_ref, k_ref, z_ref, o_ref, kmean_sc, vt_sc, *, seq):
    tq = MOBA_BLOCK
    nb = seq // MOBA_BLOCK
    ncand = MOBA_CAND_ROWS
    nk = (i + 1) * tq
    past = lax.broadcasted_iota(jnp.int32, (ncand, tq), 0) < i
    pw = 2 * tq
    causal = (lax.broadcasted_iota(jnp.int32, (tq, pw), 0)
              <= (lax.broadcasted_iota(jnp.int32, (tq, pw), 1) & (tq - 1)))
    zeros = jnp.zeros((tq, DH), BF16)
    for h0 in range(0, NH, 2):
        pair = slice(h0 * DH, (h0 + 2) * DH)
        qa = q_ref[:, h0 * DH:(h0 + 1) * DH]
        qb = q_ref[:, (h0 + 1) * DH:(h0 + 2) * DH]
        q_pair = jnp.concatenate([jnp.concatenate([_prescale(qa), zeros], axis=1),
                                  jnp.concatenate([zeros, _prescale(qb)], axis=1)], axis=0)
        s = _dot_nt(k_ref[0:nk, pair], q_pair)
        parts = [jnp.where(causal, s[i * tq:nk], NEG)]
        if i > 0:
            bias = []
            for h, qh in ((h0, qa), (h0 + 1, qb)):
                gate = _dot_nt(kmean_sc[h].astype(BF16), qh)[0:ncand]
                score = jnp.where(past, gate, NEG)
                sel = jnp.logical_and(_topk_rows(score, min(MOBA_TOPK, nb), i), past)
                bias.append(jnp.where(sel, 0.0, NEG))
            bias = jnp.concatenate(bias, axis=1)
            parts = [s[j * tq:(j + 1) * tq] + bias[j:j + 1] for j in range(i)] + parts
        p, inv = _softmax_rows(jnp.concatenate(parts, axis=0))
        for n, h in enumerate((h0, h0 + 1)):
            cs = slice(h * DH, (h + 1) * DH)
            ls = slice(n * tq, (n + 1) * tq)
            o = jnp.transpose(_dot(vt_sc[h, :, 0:nk], p[:, ls]) * inv[:, ls])
            o_ref[:, cs] = (o * _silu(z_ref[:, cs].astype(F32))).astype(o_ref.dtype)


def _moba_kernel(q_ref, k_ref, v_ref, z_ref, o_ref, kmean_sc, vt_sc, *, seq):
    i = pl.program_id(1)
    tq = MOBA_BLOCK
    nb = seq // MOBA_BLOCK

    @pl.when(i == 0)
    def _():
        kmean_sc[...] = jnp.zeros(kmean_sc.shape, F32)
        for h in range(NH):
            cs = slice(h * DH, (h + 1) * DH)
            kmean_sc[h, 0:nb, :] = jnp.mean(k_ref[:, cs].astype(F32).reshape(nb, MOBA_BLOCK, DH), axis=1)
            for c in range(nb):
                rows = slice(c * tq, (c + 1) * tq)
                vt_sc[h, :, rows] = jnp.transpose(v_ref[rows, cs].astype(F32)).astype(BF16)

    for c in range(nb):
        pl.when(i == c)(functools.partial(_moba_tile, c, q_ref, k_ref, z_ref, o_ref, kmean_sc, vt_sc, seq=seq))


def _moba(ga, gb, gc, batch, seq):
    tq = MOBA_BLOCK
    nq = seq // tq
    assert nq <= MOBA_CAND_ROWS
    row = lambda col: (lambda b, i: (b * nq + i, col))
    return pl.pallas_call(
        functools.partial(_moba_kernel, seq=seq),
        out_shape=jax.ShapeDtypeStruct((batch * seq, GROUP), BF16),
        grid=(batch, nq),
        in_specs=[pl.BlockSpec((tq, GROUP), row(2)),
                  pl.BlockSpec((seq, GROUP), lambda b, i: (b, 3)),
                  pl.BlockSpec((seq, GROUP), lambda b, i: (b, 4)),
                  pl.BlockSpec((tq, GROUP), row(GC0 + 5))],
        out_specs=pl.BlockSpec((tq, GROUP), row(0)),
        scratch_shapes=[pltpu.VMEM((NH, LANES, DH), F32), pltpu.VMEM((NH, DH, seq), BF16)],
        compiler_params=_cparams(("parallel", "arbitrary")),
        name="moba",
    )(ga, ga, gb, gc)


def _out_kernel(ya_ref, yb_ref, yc_ref, yd_ref, w_ref, x_ref, g_ref, b_ref, o_ref, ob_ref):
    acc = DEEPNORM_ALPHA * x_ref[...]
    for p, y_ref in enumerate((ya_ref, yb_ref, yc_ref, yd_ref)):
        acc = acc + _dot(y_ref[...], w_ref[p * GROUP:(p + 1) * GROUP, :])
    mu = jnp.mean(acc, axis=-1, keepdims=True)
    var = jnp.mean(jnp.square(acc - mu), axis=-1, keepdims=True)
    y = (acc - mu) * lax.rsqrt(var + LN_EPS) * g_ref[...] + b_ref[...]
    o_ref[...] = y
    ob_ref[...] = y.astype(BF16)


def _out_proj(ys, w_out, x, ln_g, ln_b):
    m, d = x.shape
    tm = OUT_ROWS
    yspec = pl.BlockSpec((tm, GROUP), lambda i: (i, 0))
    full = lambda a: pl.BlockSpec(a.shape, lambda i: (0,) * a.ndim)
    xspec = pl.BlockSpec((tm, d), lambda i: (i, 0))
    return pl.pallas_call(
        _out_kernel,
        out_shape=(jax.ShapeDtypeStruct((m, d), F32), jax.ShapeDtypeStruct((m, d), BF16)),
        grid=(m // tm,),
        in_specs=[yspec] * 4 + [full(w_out), xspec, full(ln_g), full(ln_b)],
        out_specs=(xspec, xspec),
        compiler_params=_cparams(("parallel",)),
        name="out_proj_ln",
    )(*ys, w_out, x, ln_g, ln_b)


_GROUPS = (
    (('nsa_q', 0, 512), ('nsa_kv', 0, 384), ('gates', 0, LANES), ('moba_qkv', 0, 1024)),
    (('nsa_kv', 384, 384), (None, 0, 128), ('mlstm_qkv', 0, 1536), ('moba_qkv', 1024, 512)),
    (('nsa_z', 0, 512), ('mlstm_o', 0, 512), ('mlstm_z', 0, 512), ('lru_x', 0, 512), ('lru_z', 0, 512),
     ('moba_z', 0, 512)),
)
_GATE_PIECES = (('nsa_gate', 0, 12), (None, 0, 4), ('mlstm_if', 0, 8), (None, 0, LANES - 24))


def _copy_rows(w_ref, pieces, o_ref, dst):
    for name, off, width in pieces:
        if name is None:
            o_ref[dst:dst + width, :] = jnp.zeros((width, o_ref.shape[1]), o_ref.dtype)
        else:
            src = _OFF[name] + off
            o_ref[dst:dst + width, :] = w_ref[src:src + width, :].astype(o_ref.dtype)
        dst += width


def _regroup_kernel(w_ref, o_ref, gate_sc):
    dst = 0
    for piece in sum(_GROUPS, ()):
        if piece[0] == 'gates':
            _copy_rows(w_ref, _GATE_PIECES, gate_sc, 0)
            o_ref[dst:dst + LANES, :] = gate_sc[...].astype(o_ref.dtype)
        else:
            _copy_rows(w_ref, (piece,), o_ref, dst)
        dst += piece[2]


def _regroup_w_in(w_in):
    depth, d, n_in = w_in.shape
    wt = jnp.transpose(w_in, (0, 2, 1))
    tk = REGROUP_COLS
    n_all = sum(p[2] for pieces in _GROUPS for p in pieces)
    return pl.pallas_call(
        _regroup_kernel,
        out_shape=jax.ShapeDtypeStruct((depth, n_all, d), BF16),
        grid=(depth, d // tk),
        in_specs=[pl.BlockSpec((None, n_in, tk), lambda l, i: (l, 0, i))],
        out_specs=pl.BlockSpec((None, n_all, tk), lambda l, i: (l, 0, i)),
        scratch_shapes=[pltpu.VMEM((LANES, tk), F32)],
        compiler_params=_cparams(("parallel", "parallel")),
        name="regroup_w_in",
    )(wt)


def _rope_tables(seq):
    half = ROT_DIM // 2
    inv_freq = jnp.power(ROPE_THETA, -jnp.arange(half, dtype=F32) * (2.0 / ROT_DIM))
    ang = jnp.arange(seq, dtype=jnp.int32).astype(F32)[:, None] * inv_freq[None, :]
    cos, sin = jnp.cos(ang), jnp.sin(ang)
    ones = jnp.ones((seq, DH - ROT_DIM), F32)
    lane = jnp.arange(DH)
    partner = jnp.where(lane < half, lane + half, lane - half)
    swap = (lane[:, None] == partner[None, :]).astype(BF16)
    return (jnp.concatenate([cos, cos, ones], axis=1),
            jnp.concatenate([-sin, sin, 0.0 * ones], axis=1), swap)


def _cmp_overlap_t(seq):
    n_cmp = seq // CMP_STRIDE - 1
    n_sel = seq // SEL_BLOCK
    cs = jnp.arange(LANES)[None, :] * CMP_STRIDE
    ss = jnp.arange(LANES)[:, None] * SEL_BLOCK
    ov = (cs < ss + SEL_BLOCK) & (cs + CMP_BLOCK > ss)
    ov = ov & (jnp.arange(LANES)[None, :] < n_cmp) & (jnp.arange(LANES)[:, None] < n_sel)
    return ov.astype(BF16)


def _chunk_tri(tc):
    r = jnp.arange(tc)
    same = (r[:, None] // MLSTM_CHUNK) == (r[None, :] // MLSTM_CHUNK)
    tril = (same & (r[None, :] <= r[:, None])).astype(BF16)
    return tril, tril.T


def _layer(x, xb, batch, seq, consts, w_all, layer, cmp_w1, cmp_w2, cmp_pe, i_bias, f_bias, norm_g,
           conv_w, conv_b, gate_w, gate_b, lam, w_out, ln_g, ln_b):
    rope, ov_t, tril, triu = consts
    ga, gbc, gd, gd_t = _project(xb, w_all, layer, rope, seq, min(PROJ_ROWS, seq), PROJ_COLS)
    gb = gc = gbc

    nb = seq // CMP_STRIDE
    tk = ga[:, 4 * DH:5 * DH].reshape(batch, nb, CMP_STRIDE * DH)
    tv = gb[:, 0:DH].reshape(batch, nb, CMP_STRIDE * DH)
    pe = cmp_pe.reshape(2, 2, CMP_STRIDE * DH)
    kc, vc = _compress(tk, tv, cmp_w1.astype(BF16), cmp_w2.astype(BF16), pe)
    y_a = _nsa(ga, gb, gc, gd_t, kc, vc, ov_t, batch, seq)

    bias = jnp.concatenate([i_bias, f_bias])
    bias_col = jnp.zeros((1, LANES), F32).at[0, GD_I:GD_I + 2 * NH].set(bias)
    bias_row = bias[:, None]
    y_b = _mlstm(gb, gc, gd, gd_t, bias_col, bias_row, norm_g[None, :], tril, triu, batch, seq)

    y_c = _lru(gc, conv_w, conv_b[None, :], gate_w.astype(BF16), gate_b, lam[None, :], batch, seq)

    y_d = _moba(ga, gb, gc, batch, seq)

    return _out_proj((y_a, y_b, y_c, y_d), w_out.astype(BF16), x, ln_g[None, :], ln_b[None, :])


def kernel(x, w_in, nsa_cmp_w1, nsa_cmp_w2, nsa_cmp_pe, mlstm_i_bias, mlstm_f_bias, mlstm_norm_g,
           lru_conv_w, lru_conv_b, lru_gate_w, lru_gate_b, lru_lambda, w_out, ln_g, ln_b):
    batch, seq, d = x.shape
    tril, triu = _chunk_tri(MLSTM_ROWS)
    consts = (_rope_tables(seq), _cmp_overlap_t(seq), tril, triu)
    xf = x.reshape(batch * seq, d)
    xb = xf.astype(BF16)
    w_all = _regroup_w_in(w_in)
    for l in range(w_in.shape[0]):
        xf, xb = _layer(xf, xb, batch, seq, consts, w_all, l, nsa_cmp_w1[l], nsa_cmp_w2[l],
                        nsa_cmp_pe[l], mlstm_i_bias[l], mlstm_f_bias[l], mlstm_norm_g[l], lru_conv_w[l],
                        lru_conv_b[l], lru_gate_w[l], lru_gate_b[l], lru_lambda[l], w_out[l], ln_g[l], ln_b[l])
    return xf.reshape(batch, seq, d)
```

```python
import functools
import math

import jax
import jax.numpy as jnp
from jax import lax
from jax.experimental import pallas as pl
from jax.experimental.pallas import tpu as pltpu

F32 = jnp.float32
BF16 = jnp.bfloat16

DEPTH = 2
GROUP = 512
DH = 128
NH = 4
ROT_DIM = 32
ROPE_THETA = 500000.0

CMP_BLOCK = 32
CMP_STRIDE = 16
SEL_BLOCK = 64
SEL_TOPK = 8
WIN = 256
FORCE_SCORE = 1e9

MLSTM_CHUNK = 512
LRU_C = 8.0
MOBA_BLOCK = 256
MOBA_TOPK = 3

DEEPNORM_ALPHA = (2 * DEPTH) ** 0.25
NEG = -1e30
LN_EPS = 1e-5
SCALE = DH ** -0.5
EXP2_SCALE = SCALE * math.log2(math.e)

LANES = 128
SUB = 8
VMEM_LIMIT = 56 * 1024 * 1024

PROJ_ROWS = 2048
PROJ_COLS = 512
REGROUP_COLS = 256
NSA_ROWS = 256
NSA_SEL_ROWS = 32
MOBA_CAND_ROWS = SUB
LRU_ROWS = 512
MLSTM_ROWS = 2 * MLSTM_CHUNK
OUT_ROWS = 512

_OFF = {}
_o = 0
for _name, _w in (('nsa_q', 512), ('nsa_kv', 768), ('nsa_gate', 12), ('nsa_z', 512), ('mlstm_qkv', 1536),
                  ('mlstm_if', 8), ('mlstm_o', 512), ('mlstm_z', 512), ('lru_x', 512), ('lru_z', 512),
                  ('moba_qkv', 1536), ('moba_z', 512)):
    _OFF[_name] = _o
    _o += _w

GD_I = 16
GD_F = 20
GC0 = 5
GD_COL = 7 * 128

NT = (((1,), (1,)), ((), ()))
TN = (((0,), (0,)), ((), ()))


def _cparams(sem):
    return pltpu.CompilerParams(dimension_semantics=sem, vmem_limit_bytes=VMEM_LIMIT)


def _dot(a, b):
    return jnp.dot(a, b, preferred_element_type=F32)


def _dot_nt(a, b):
    return lax.dot_general(a, b, NT, preferred_element_type=F32)


def _split_hi_lo(a):
    hi = a.astype(BF16)
    lo = (a - hi.astype(F32)).astype(BF16)
    return hi, lo


def _sigmoid(x):
    return 0.5 * jnp.tanh(0.5 * x) + 0.5


def _silu(x):
    return x * _sigmoid(x)


def _log_sigmoid(x):
    return jnp.minimum(x, 0.0) - jnp.log(1.0 + jnp.exp(-jnp.abs(x)))


def _proj_kernel(x_ref, w_ref, cos_ref, sin_ref, swap_ref, ga_ref, gbc_ref, gd_ref, gdt_ref, *, n_rot):
    j = pl.program_id(1)
    acc = _dot_nt(x_ref[...], w_ref[...])
    gbc_ref[...] = acc.astype(gbc_ref.dtype)

    @pl.when(j < n_rot)
    def _():
        c = cos_ref[...]
        s = sin_ref[...]
        for h in range(acc.shape[1] // DH):
            t = acc[:, h * DH:(h + 1) * DH]
            swapped = _dot(t.astype(BF16), swap_ref[...])
            ga_ref[:, h * DH:(h + 1) * DH] = (t * c + swapped * s).astype(ga_ref.dtype)

    @pl.when(j == GD_COL // acc.shape[1])
    def _():
        gd = acc[:, GD_COL % acc.shape[1]:GD_COL % acc.shape[1] + LANES]
        gd_ref[...] = gd
        gdt_ref[...] = jnp.transpose(gd)


def _project(xb, w_all, layer, rope, seq, tm, tn):
    m, k = xb.shape
    widths = [sum(p[2] for p in pieces) for pieces in _GROUPS]
    assert tn == GROUP and widths[1] == GC0 * GROUP
    n_rot = widths[0] // tn
    nrep = seq // tm
    rope_spec = pl.BlockSpec((tm, DH), lambda i, j: (i % nrep, 0))
    cos, sin, swap = rope
    return pl.pallas_call(
        functools.partial(_proj_kernel, n_rot=n_rot),
        out_shape=(jax.ShapeDtypeStruct((m, widths[0]), BF16), jax.ShapeDtypeStruct((m, widths[1] + widths[2]), BF16),
                   jax.ShapeDtypeStruct((m, LANES), F32), jax.ShapeDtypeStruct((LANES, m), F32)),
        grid=(m // tm, sum(widths) // tn),
        in_specs=[pl.BlockSpec((tm, k), lambda i, j: (i, 0)), pl.BlockSpec((None, tn, k), lambda i, j: (layer, j, 0)),
                  rope_spec, rope_spec, pl.BlockSpec((DH, DH), lambda i, j: (0, 0))],
        out_specs=(pl.BlockSpec((tm, tn), lambda i, j: (i, jnp.minimum(j, n_rot - 1))),
                   pl.BlockSpec((tm, tn), lambda i, j: (i, jnp.maximum(j - n_rot, 0))),
                   pl.BlockSpec((tm, LANES), lambda i, j: (i, 0)), pl.BlockSpec((LANES, tm), lambda i, j: (0, i))),
        compiler_params=_cparams(("parallel", "arbitrary")),
        name="in_proj",
    )(xb, w_all, cos, sin, swap)


def _compress_kernel(tk_ref, tv_ref, w1_ref, w2_ref, pe_ref, kc_ref, vc_ref, *, nb):
    half = (CMP_BLOCK // 2) * DH
    for idx, (t_ref, o_ref) in enumerate(((tk_ref, kc_ref), (tv_ref, vc_ref))):
        t = t_ref[...].astype(F32)
        lo = (t + pe_ref[idx, 0:1, :]).astype(BF16)
        hi = (t + pe_ref[idx, 1:2, :]).astype(BF16)
        a = _dot(lo, w1_ref[idx, 0:half, :])
        b = _dot(hi, w1_ref[idx, half:2 * half, :])
        hid = _silu(a + pltpu.roll(b, nb - 1, 0))
        out = _dot(hid.astype(BF16), w2_ref[idx])
        o_ref[...] = jnp.zeros(o_ref.shape, o_ref.dtype)
        o_ref[0:nb, :] = out.astype(o_ref.dtype)


def _compress(tk, tv, w1, w2, pe):
    b, nb, width = tk.shape
    blk = pl.BlockSpec((None, nb, width), lambda i: (i, 0, 0))
    full = lambda a: pl.BlockSpec(a.shape, lambda i: (0,) * a.ndim)
    out = jax.ShapeDtypeStruct((b, LANES, DH), BF16)
    return pl.pallas_call(
        functools.partial(_compress_kernel, nb=nb),
        out_shape=(out, out),
        grid=(b,),
        in_specs=[blk, blk, full(w1), full(w2), full(pe)],
        out_specs=(pl.BlockSpec((None, LANES, DH), lambda i: (i, 0, 0)),) * 2,
        compiler_params=_cparams(("parallel",)),
        name="nsa_compress",
    )(tk, tv, w1, w2, pe)


def _topk_rows(score, k_top, n_live):
    rowi = lax.broadcasted_iota(jnp.int32, score.shape, 0)
    rank = jnp.zeros(score.shape, F32)
    for k in range(n_live):
        sk = score[k:k + 1, :]
        beats = jnp.logical_or(sk > score, jnp.logical_and(sk == score, rowi > k))
        rank = rank + jnp.where(beats, 1.0, 0.0)
    return rank < k_top


def _prescale(q):
    return (q.astype(F32) * EXP2_SCALE).astype(q.dtype)


def _softmax_rows(s):
    p = jnp.exp2(s - jnp.max(s, axis=0, keepdims=True))
    return p.astype(BF16), 1.0 / jnp.sum(p, axis=0, keepdims=True)


def _nsa_tile(i, q_ref, kc_ref, vc_ref, ks_ref, kw_ref, gt_ref, z_ref, ovt_ref, diagb_ref, winb_ref,
              o_ref, vst_sc, vwt_sc, *, seq, tq):
    width = NH * tq
    n_cmp = seq // CMP_STRIDE - 1
    n_sel = seq // SEL_BLOCK
    n_rows = NSA_SEL_ROWS
    q = q_ref[...]
    qs = jnp.concatenate([q[:, h * DH:(h + 1) * DH] for h in range(NH)], axis=0)
    qs = _prescale(qs)

    n_c = min(LANES, (i + 1) * tq // CMP_STRIDE)
    rown = lax.broadcasted_iota(jnp.int32, (n_c, width), 0)
    tok = i * tq + (lax.broadcasted_iota(jnp.int32, (n_c, width), 1) & (tq - 1))
    mask_c = jnp.logical_and(rown * CMP_STRIDE + (CMP_BLOCK - 1) <= tok, rown < n_cmp)
    s = jnp.where(mask_c, _dot_nt(kc_ref[0:n_c, :], qs), NEG)
    e = jnp.exp2(s - jnp.max(s, axis=0, keepdims=True))
    p_c = jnp.where(mask_c, e, 0.0) * (1.0 / jnp.sum(e, axis=0, keepdims=True))
    o_c = lax.dot_general(vc_ref[0:n_c, :], p_c.astype(BF16), TN, preferred_element_type=F32)

    p_sum = p_c[:, 0:tq]
    for h in range(1, NH):
        p_sum = p_sum + p_c[:, h * tq:(h + 1) * tq]
    if n_c < LANES:
        p_sum = jnp.concatenate([p_sum, jnp.zeros((LANES - n_c, tq), F32)], axis=0)
    p_hi, p_lo = _split_hi_lo(p_sum)
    imp = (_dot(ovt_ref[...], p_hi) + _dot(ovt_ref[...], p_lo))[0:n_rows]
    rowj = lax.broadcasted_iota(jnp.int32, (n_rows, tq), 0)
    t_q = i * tq + lax.broadcasted_iota(jnp.int32, (n_rows, tq), 1)
    cur = t_q >> (SEL_BLOCK.bit_length() - 1)
    forced = jnp.logical_or(rowj == 0, jnp.logical_or(rowj == cur, rowj == cur - 1))
    valid = rowj * SEL_BLOCK <= t_q
    score = jnp.where(forced, FORCE_SCORE, jnp.where(valid, imp, NEG))
    score = jnp.where(rowj < n_sel, score, -jnp.inf)
    nk = (i + 1) * tq
    n_live = nk // SEL_BLOCK
    live_rows = -(-n_live // SUB) * SUB
    bias = jnp.where(_topk_rows(score[0:live_rows], min(SEL_TOPK, n_sel), n_live), 0.0, NEG)
    bias = jnp.concatenate([bias] * NH, axis=1)

    s = _dot_nt(ks_ref[0:nk, :], qs)
    parts = [s[j * SEL_BLOCK:(j + 1) * SEL_BLOCK] + bias[j:j + 1] for j in range(nk // SEL_BLOCK)]
    own = jnp.concatenate(parts[i * tq // SEL_BLOCK:], axis=0) + diagb_ref[...]
    p, inv = _softmax_rows(jnp.concatenate(parts[:i * tq // SEL_BLOCK] + [own], axis=0))
    o_s = _dot(vst_sc[:, 0:nk], p) * inv

    n_wblk = winb_ref.shape[0]
    blocks = [b for b in range(i - n_wblk + 1, i + 1) if b >= 0]
    parts = [_dot_nt(kw_ref[b * tq:(b + 1) * tq, :], qs) + winb_ref[b - i + n_wblk - 1] for b in blocks]
    p, inv = _softmax_rows(jnp.concatenate(parts, axis=0))
    o_w = _dot(vwt_sc[:, blocks[0] * tq:nk], p) * inv

    g = _sigmoid(gt_ref[0:GD_I, :])
    for h in range(NH):
        ls = slice(h * tq, (h + 1) * tq)
        cs = slice(h * DH, (h + 1) * DH)
        mix = (g[3 * h:3 * h + 1] * o_c[:, ls] + g[3 * h + 1:3 * h + 2] * o_s[:, ls]
               + g[3 * h + 2:3 * h + 3] * o_w[:, ls])
        o_ref[:, cs] = (jnp.transpose(mix) * _silu(z_ref[:, cs].astype(F32))).astype(o_ref.dtype)


def _nsa_kernel(q_ref, kc_ref, vc_ref, ks_ref, vs_ref, kw_ref, vw_ref, gt_ref, z_ref, ovt_ref, diagb_ref,
                winb_ref, o_ref, vst_sc, vwt_sc, *, seq, tq):
    i = pl.program_id(1)

    @pl.when(i == 0)
    def _():
        for c in range(seq // tq):
            cols = slice(c * tq, (c + 1) * tq)
            vst_sc[:, cols] = jnp.transpose(vs_ref[cols, :].astype(F32)).astype(BF16)
            vwt_sc[:, cols] = jnp.transpose(vw_ref[cols, :].astype(F32)).astype(BF16)

    for c in range(seq // tq):
        pl.when(i == c)(functools.partial(
            _nsa_tile, c, q_ref, kc_ref, vc_ref, ks_ref, kw_ref, gt_ref, z_ref, ovt_ref, diagb_ref, winb_ref,
            o_ref, vst_sc, vwt_sc, seq=seq, tq=tq))


def _nsa_bias_tables(tq):
    t = jnp.arange(NH * tq)[None, :] % tq
    r = jnp.arange(tq)[:, None]
    diag = jnp.where(r <= t, 0.0, NEG)
    n_wblk = WIN // tq + 1
    win = []
    for b in range(n_wblk):
        diff = t - (r + (b - (n_wblk - 1)) * tq)
        win.append(jnp.where((diff >= 0) & (diff < WIN), 0.0, NEG))
    return diag.astype(F32), jnp.stack(win).astype(F32)


def _nsa(ga, gb, gc, gd_t, kc, vc, ov_t, batch, seq):
    tq = NSA_ROWS
    nq = seq // tq
    assert seq // SEL_BLOCK <= NSA_SEL_ROWS and tq % SEL_BLOCK == 0
    row = lambda b, i: (b * nq + i, 0)
    kv = lambda col: pl.BlockSpec((seq, DH), lambda b, i: (b, col))
    cmp_spec = pl.BlockSpec((None, LANES, DH), lambda b, i: (b, 0, 0))
    full = lambda a: pl.BlockSpec(a.shape, lambda b, i: (0,) * a.ndim)
    diag_b, win_b = _nsa_bias_tables(tq)
    return pl.pallas_call(
        functools.partial(_nsa_kernel, seq=seq, tq=tq),
        out_shape=jax.ShapeDtypeStruct((batch * seq, GROUP), BF16),
        grid=(batch, nq),
        in_specs=[pl.BlockSpec((tq, GROUP), row), cmp_spec, cmp_spec,
                  kv(5), kv(1), kv(6), kv(2),
                  pl.BlockSpec((LANES, tq), lambda b, i: (0, b * nq + i)),
                  pl.BlockSpec((tq, GROUP), lambda b, i: (b * nq + i, GC0)), full(ov_t), full(diag_b), full(win_b)],
        out_specs=pl.BlockSpec((tq, GROUP), row),
        scratch_shapes=[pltpu.VMEM((DH, seq), BF16), pltpu.VMEM((DH, seq), BF16)],
        compiler_params=_cparams(("parallel", "arbitrary")),
        name="nsa_attention",
    )(ga, kc, vc, ga, gb, ga, gb, gd_t, gc, ov_t, diag_b, win_b)


def _mlstm_kernel(q_ref, k_ref, v_ref, gcol_ref, grow_ref, bcol_ref, brow_ref, og_ref, z_ref, ng_ref,
                  tril_ref, triu_ref, out_ref, cn_sc, m_sc, *, tc):
    L = MLSTM_CHUNK

    @pl.when(pl.program_id(1) == 0)
    def _():
        cn_sc[...] = jnp.zeros(cn_sc.shape, F32)
        m_sc[...] = jnp.full(m_sc.shape, NEG, F32)

    gcol = gcol_ref[...] + bcol_ref[...]
    grow = grow_ref[...] + brow_ref[...]
    hi, lo = _split_hi_lo(_log_sigmoid(gcol))
    bcol_all = _dot(tril_ref[...], hi) + _dot(tril_ref[...], lo)
    hi, lo = _split_hi_lo(_log_sigmoid(grow))
    brow_all = _dot(hi, triu_ref[...]) + _dot(lo, triu_ref[...])

    causal = (lax.broadcasted_iota(jnp.int32, (L, L), 1) <= lax.broadcasted_iota(jnp.int32, (L, L), 0))
    ones = jnp.ones((L, DH), BF16)
    mean_mat = jnp.full((DH, DH), 1.0 / DH, BF16)

    def lane_mean(a):
        a_hi, a_lo = _split_hi_lo(a)
        return _dot(a_hi, mean_mat) + _dot(a_lo, mean_mat)

    for c in range(tc // L):
        rs = slice(c * L, (c + 1) * L)
        for h in range(NH):
            cs = slice(h * DH, (h + 1) * DH)
            qh = q_ref[rs, cs]
            kh = k_ref[rs, cs]
            v_aug = jnp.concatenate([v_ref[rs, cs], ones], axis=1)
            ig_col = gcol[rs, GD_I + h:GD_I + h + 1]
            b_col = bcol_all[rs, GD_F + h:GD_F + h + 1]
            ig_row = grow[h:h + 1, rs]
            b_row = brow_all[NH + h:NH + h + 1, rs]
            cn_prev = cn_sc[h]
            m_prev = m_sc[h][:, 0:1]

            d_log = jnp.where(causal, b_col + (ig_row - b_row), NEG)
            m_intra = jnp.max(d_log, axis=-1, keepdims=True)
            m_inter = b_col + m_prev
            m_t = jnp.maximum(m_inter, m_intra)
            w_inter = jnp.exp(m_inter - m_t)
            qk = _dot_nt(qh, kh) * jnp.exp(d_log - (m_t - math.log(SCALE)))
            intra = _dot(qk.astype(BF16), v_aug)
            inter = _dot(qh, cn_prev.astype(BF16))
            num = intra[:, 0:DH] + w_inter * inter[:, 0:DH]
            den = intra[:, DH:2 * DH] + w_inter * inter[:, DH:2 * DH]
            hh = num / jnp.maximum(jnp.abs(den), jnp.exp(-m_t))
            hh = _sigmoid(og_ref[rs, cs].astype(F32)) * hh
            dlt = hh - lane_mean(hh)
            hn = dlt * lax.rsqrt(lane_mean(jnp.square(dlt)) + LN_EPS) * ng_ref[:, cs]
            out_ref[rs, cs] = (hn * _silu(z_ref[rs, cs].astype(F32))).astype(out_ref.dtype)

            b_last = b_row[:, L - 1:L]
            m_loc = jnp.max(b_last - b_row + ig_row, axis=-1, keepdims=True)
            e_col = jnp.exp(b_last - b_col + ig_col - m_loc)
            ek = ((e_col * SCALE) * kh.astype(F32)).astype(BF16)
            g_cn = lax.dot_general(ek, v_aug, TN, preferred_element_type=F32)
            m_new = jnp.maximum(b_last + m_prev, m_loc)
            cn_sc[h] = jnp.exp(b_last + m_prev - m_new) * cn_prev + jnp.exp(m_loc - m_new) * g_cn
            m_sc[h] = jnp.broadcast_to(m_new, (1, LANES))


def _mlstm(gb, gc, gd, gd_t, bias_col, bias_row, norm_g, tril, triu, batch, seq):
    tc = MLSTM_ROWS
    nt = seq // tc
    row = lambda col: (lambda b, j: (b * nt + j, col))
    full = lambda a: pl.BlockSpec(a.shape, lambda b, j: (0,) * a.ndim)
    return pl.pallas_call(
        functools.partial(_mlstm_kernel, tc=tc),
        out_shape=jax.ShapeDtypeStruct((batch * seq, GROUP), BF16),
        grid=(batch, nt),
        in_specs=[pl.BlockSpec((tc, GROUP), row(1)), pl.BlockSpec((tc, GROUP), row(2)),
                  pl.BlockSpec((tc, GROUP), row(3)),
                  pl.BlockSpec((tc, LANES), row(0)),
                  pl.BlockSpec((2 * NH, tc), lambda b, j: (GD_I // (2 * NH), b * nt + j)),
                  full(bias_col), full(bias_row),
                  pl.BlockSpec((tc, GROUP), row(GC0 + 1)), pl.BlockSpec((tc, GROUP), row(GC0 + 2)),
                  full(norm_g), full(tril), full(triu)],
        out_specs=pl.BlockSpec((tc, GROUP), row(0)),
        scratch_shapes=[pltpu.VMEM((NH, DH, 2 * DH), F32), pltpu.VMEM((NH, 1, LANES), F32)],
        compiler_params=_cparams(("parallel", "arbitrary")),
        name="mlstm",
    )(gb, gb, gb, gd, gd_t, bias_col, bias_row, gc, gc, norm_g, tril, triu)


def _lru_kernel(x_ref, z_ref, cw_ref, cb_ref, gw_ref, gb_ref, lam_ref, out_ref, xbuf, h_sc, *, tl):
    pad = 8

    @pl.when(pl.program_id(1) == 0)
    def _():
        xbuf[0:pad, :] = jnp.zeros((pad, GROUP), F32)
        h_sc[...] = jnp.zeros(h_sc.shape, F32)

    x = x_ref[...].astype(F32)
    xbuf[pad:pad + tl, :] = x
    u = cw_ref[3:4, :] * x + cb_ref[...]
    for w in range(3):
        u = u + cw_ref[w:w + 1, :] * xbuf[pad - 3 + w:pad - 3 + w + tl, :]
    xbuf[0:pad, :] = x[tl - pad:tl, :]

    ub = u.astype(BF16)
    pre = []
    for gi in range(2):
        pre.append(jnp.concatenate(
            [_dot(ub[:, n * DH:(n + 1) * DH], gw_ref[gi, n]) for n in range(NH)], axis=1) + gb_ref[gi:gi + 1, :])
    r = _sigmoid(pre[0])
    ig = _sigmoid(pre[1])
    neg_lam = -lam_ref[...]
    softplus = jnp.maximum(neg_lam, 0.0) + jnp.log(1.0 + jnp.exp(-jnp.abs(neg_lam)))
    a = jnp.exp(-LRU_C * r * softplus)
    bx = jnp.sqrt(1.0 - a * a) * (ig * u)

    sub = lax.broadcasted_iota(jnp.int32, (tl, GROUP), 0) & (SUB - 1)
    d = 1
    while d < SUB:
        keep = sub >= d
        a_sh = jnp.where(keep, pltpu.roll(a, d, 0), 1.0)
        b_sh = jnp.where(keep, pltpu.roll(bx, d, 0), 0.0)
        bx = a * b_sh + bx
        a = a * a_sh
        d *= 2
    carry = jnp.broadcast_to(h_sc[...], (SUB, GROUP))
    groups = []
    for g in range(tl // SUB):
        hg = a[g * SUB:(g + 1) * SUB] * carry + bx[g * SUB:(g + 1) * SUB]
        groups.append(hg)
        carry = jnp.broadcast_to(hg[SUB - 1:SUB], (SUB, GROUP))
    hseq = jnp.concatenate(groups, axis=0)
    h_sc[...] = hseq[tl - 1:tl, :]
    out_ref[...] = (hseq * _silu(z_ref[...].astype(F32))).astype(out_ref.dtype)


def _lru(gc, conv_w, conv_b, gate_w, gate_b, lam, batch, seq):
    tl = LRU_ROWS
    nt = seq // tl
    row = lambda col: (lambda b, j: (b * nt + j, col))
    full = lambda a: pl.BlockSpec(a.shape, lambda b, j: (0,) * a.ndim)
    return pl.pallas_call(
        functools.partial(_lru_kernel, tl=tl),
        out_shape=jax.ShapeDtypeStruct((batch * seq, GROUP), BF16),
        grid=(batch, nt),
        in_specs=[pl.BlockSpec((tl, GROUP), row(GC0 + 3)), pl.BlockSpec((tl, GROUP), row(GC0 + 4)),
                  full(conv_w), full(conv_b), full(gate_w), full(gate_b), full(lam)],
        out_specs=pl.BlockSpec((tl, GROUP), row(0)),
        scratch_shapes=[pltpu.VMEM((tl + 8, GROUP), F32), pltpu.VMEM((1, GROUP), F32)],
        compiler_params=_cparams(("parallel", "arbitrary")),
        name="rglru",
    )(gc, gc, conv_w, conv_b, gate_w, gate_b, lam)


def _moba_tile(i, q_ref, k_ref, z_ref, o_ref, kmean_sc, vt_sc, *, seq):
    tq = MOBA_BLOCK
    nb = seq // MOBA_BLOCK
    ncand = MOBA_CAND_ROWS
    nk = (i + 1) * tq
    past = lax.broadcasted_iota(jnp.int32, (ncand, tq), 0) < i
    pw = 2 * tq
    causal = (lax.broadcasted_iota(jnp.int32, (tq, pw), 0)
              <= (lax.broadcasted_iota(jnp.int32, (tq, pw), 1) & (tq - 1)))
    zeros = jnp.zeros((tq, DH), BF16)
    for h0 in range(0, NH, 2):
        pair = slice(h0 * DH, (h0 + 2) * DH)
        qa = q_ref[:, h0 * DH:(h0 + 1) * DH]
        qb = q_ref[:, (h0 + 1) * DH:(h0 + 2) * DH]
        q_pair = jnp.concatenate([jnp.concatenate([_prescale(qa), zeros], axis=1),
                                  jnp.concatenate([zeros, _prescale(qb)], axis=1)], axis=0)
        s = _dot_nt(k_ref[0:nk, pair], q_pair)
        parts = [jnp.where(causal, s[i * tq:nk], NEG)]
        if i > 0:
            bias = []
            for h, qh in ((h0, qa), (h0 + 1, qb)):
                gate = _dot_nt(kmean_sc[h].astype(BF16), qh)[0:ncand]
                score = jnp.where(past, gate, NEG)
                sel = jnp.logical_and(_topk_rows(score, min(MOBA_TOPK, nb), i), past)
                bias.append(jnp.where(sel, 0.0, NEG))
            bias = jnp.concatenate(bias, axis=1)
            parts = [s[j * tq:(j + 1) * tq] + bias[j:j + 1] for j in range(i)] + parts
        p, inv = _softmax_rows(jnp.concatenate(parts, axis=0))
        for n, h in enumerate((h0, h0 + 1)):
            cs = slice(h * DH, (h + 1) * DH)
            ls = slice(n * tq, (n + 1) * tq)
            o = jnp.transpose(_dot(vt_sc[h, :, 0:nk], p[:, ls]) * inv[:, ls])
            o_ref[:, cs] = (o * _silu(z_ref[:, cs].astype(F32))).astype(o_ref.dtype)


def _moba_kernel(q_ref, k_ref, v_ref, z_ref, o_ref, kmean_sc, vt_sc, *, seq):
    i = pl.program_id(1)
    tq = MOBA_BLOCK
    nb = seq // MOBA_BLOCK

    @pl.when(i == 0)
    def _():
        kmean_sc[...] = jnp.zeros(kmean_sc.shape, F32)
        for h in range(NH):
            cs = slice(h * DH, (h + 1) * DH)
            kmean_sc[h, 0:nb, :] = jnp.mean(k_ref[:, cs].astype(F32).reshape(nb, MOBA_BLOCK, DH), axis=1)
            for c in range(nb):
                rows = slice(c * tq, (c + 1) * tq)
                vt_sc[h, :, rows] = jnp.transpose(v_ref[rows, cs].astype(F32)).astype(BF16)

    for c in range(nb):
        pl.when(i == c)(functools.partial(_moba_tile, c, q_ref, k_ref, z_ref, o_ref, kmean_sc, vt_sc, seq=seq))


def _moba(ga, gb, gc, batch, seq):
    tq = MOBA_BLOCK
    nq = seq // tq
    assert nq <= MOBA_CAND_ROWS
    row = lambda col: (lambda b, i: (b * nq + i, col))
    return pl.pallas_call(
        functools.partial(_moba_kernel, seq=seq),
        out_shape=jax.ShapeDtypeStruct((batch * seq, GROUP), BF16),
        grid=(batch, nq),
        in_specs=[pl.BlockSpec((tq, GROUP), row(2)),
                  pl.BlockSpec((seq, GROUP), lambda b, i: (b, 3)),
                  pl.BlockSpec((seq, GROUP), lambda b, i: (b, 4)),
                  pl.BlockSpec((tq, GROUP), row(GC0 + 5))],
        out_specs=pl.BlockSpec((tq, GROUP), row(0)),
        scratch_shapes=[pltpu.VMEM((NH, LANES, DH), F32), pltpu.VMEM((NH, DH, seq), BF16)],
        compiler_params=_cparams(("parallel", "arbitrary")),
        name="moba",
    )(ga, ga, gb, gc)


def _out_kernel(ya_ref, yb_ref, yc_ref, yd_ref, w_ref, x_ref, g_ref, b_ref, o_ref, ob_ref):
    acc = DEEPNORM_ALPHA * x_ref[...]
    for p, y_ref in enumerate((ya_ref, yb_ref, yc_ref, yd_ref)):
        acc = acc + _dot(y_ref[...], w_ref[p * GROUP:(p + 1) * GROUP, :])
    mu = jnp.mean(acc, axis=-1, keepdims=True)
    var = jnp.mean(jnp.square(acc - mu), axis=-1, keepdims=True)
    y = (acc - mu) * lax.rsqrt(var + LN_EPS) * g_ref[...] + b_ref[...]
    o_ref[...] = y
    ob_ref[...] = y.astype(BF16)


def _out_proj(ys, w_out, x, ln_g, ln_b):
    m, d = x.shape
    tm = OUT_ROWS
    yspec = pl.BlockSpec((tm, GROUP), lambda i: (i, 0))
    full = lambda a: pl.BlockSpec(a.shape, lambda i: (0,) * a.ndim)
    xspec = pl.BlockSpec((tm, d), lambda i: (i, 0))
    return pl.pallas_call(
        _out_kernel,
        out_shape=(jax.ShapeDtypeStruct((m, d), F32), jax.ShapeDtypeStruct((m, d), BF16)),
        grid=(m // tm,),
        in_specs=[yspec] * 4 + [full(w_out), xspec, full(ln_g), full(ln_b)],
        out_specs=(xspec, xspec),
        compiler_params=_cparams(("parallel",)),
        name="out_proj_ln",
    )(*ys, w_out, x, ln_g, ln_b)


_GROUPS = (
    (('nsa_q', 0, 512), ('nsa_kv', 0, 384), ('gates', 0, LANES), ('moba_qkv', 0, 1024)),
    (('nsa_kv', 384, 384), (None, 0, 128), ('mlstm_qkv', 0, 1536), ('moba_qkv', 1024, 512)),
    (('nsa_z', 0, 512), ('mlstm_o', 0, 512), ('mlstm_z', 0, 512), ('lru_x', 0, 512), ('lru_z', 0, 512),
     ('moba_z', 0, 512)),
)
_GATE_PIECES = (('nsa_gate', 0, 12), (None, 0, 4), ('mlstm_if', 0, 8), (None, 0, LANES - 24))


def _copy_rows(w_ref, pieces, o_ref, dst):
    for name, off, width in pieces:
        if name is None:
            o_ref[dst:dst + width, :] = jnp.zeros((width, o_ref.shape[1]), o_ref.dtype)
        else:
            src = _OFF[name] + off
            o_ref[dst:dst + width, :] = w_ref[src:src + width, :].astype(o_ref.dtype)
        dst += width


def _regroup_kernel(w_ref, o_ref, gate_sc):
    dst = 0
    for piece in sum(_GROUPS, ()):
        if piece[0] == 'gates':
            _copy_rows(w_ref, _GATE_PIECES, gate_sc, 0)
            o_ref[dst:dst + LANES, :] = gate_sc[...].astype(o_ref.dtype)
        else:
            _copy_rows(w_ref, (piece,), o_ref, dst)
        dst += piece[2]


def _regroup_w_in(w_in):
    depth, d, n_in = w_in.shape
    wt = jnp.transpose(w_in, (0, 2, 1))
    tk = REGROUP_COLS
    n_all = sum(p[2] for pieces in _GROUPS for p in pieces)
    return pl.pallas_call(
        _regroup_kernel,
        out_shape=jax.ShapeDtypeStruct((depth, n_all, d), BF16),
        grid=(depth, d // tk),
        in_specs=[pl.BlockSpec((None, n_in, tk), lambda l, i: (l, 0, i))],
        out_specs=pl.BlockSpec((None, n_all, tk), lambda l, i: (l, 0, i)),
        scratch_shapes=[pltpu.VMEM((LANES, tk), F32)],
        compiler_params=_cparams(("parallel", "parallel")),
        name="regroup_w_in",
    )(wt)


def _rope_tables(seq):
    half = ROT_DIM // 2
    inv_freq = jnp.power(ROPE_THETA, -jnp.arange(half, dtype=F32) * (2.0 / ROT_DIM))
    ang = jnp.arange(seq, dtype=jnp.int32).astype(F32)[:, None] * inv_freq[None, :]
    cos, sin = jnp.cos(ang), jnp.sin(ang)
    ones = jnp.ones((seq, DH - ROT_DIM), F32)
    lane = jnp.arange(DH)
    partner = jnp.where(lane < half, lane + half, lane - half)
    swap = (lane[:, None] == partner[None, :]).astype(BF16)
    return (jnp.concatenate([cos, cos, ones], axis=1),
            jnp.concatenate([-sin, sin, 0.0 * ones], axis=1), swap)


def _cmp_overlap_t(seq):
    n_cmp = seq // CMP_STRIDE - 1
    n_sel = seq // SEL_BLOCK
    cs = jnp.arange(LANES)[None, :] * CMP_STRIDE
    ss = jnp.arange(LANES)[:, None] * SEL_BLOCK
    ov = (cs < ss + SEL_BLOCK) & (cs + CMP_BLOCK > ss)
    ov = ov & (jnp.arange(LANES)[None, :] < n_cmp) & (jnp.arange(LANES)[:, None] < n_sel)
    return ov.astype(BF16)


def _chunk_tri(tc):
    r = jnp.arange(tc)
    same = (r[:, None] // MLSTM_CHUNK) == (r[None, :] // MLSTM_CHUNK)
    tril = (same & (r[None, :] <= r[:, None])).astype(BF16)
    return tril, tril.T


def _layer(x, xb, batch, seq, consts, w_all, layer, cmp_w1, cmp_w2, cmp_pe, i_bias, f_bias, norm_g,
           conv_w, conv_b, gate_w, gate_b, lam, w_out, ln_g, ln_b):
    rope, ov_t, tril, triu = consts
    ga, gbc, gd, gd_t = _project(xb, w_all, layer, rope, seq, min(PROJ_ROWS, seq), PROJ_COLS)
    gb = gc = gbc

    nb = seq // CMP_STRIDE
    tk = ga[:, 4 * DH:5 * DH].reshape(batch, nb, CMP_STRIDE * DH)
    tv = gb[:, 0:DH].reshape(batch, nb, CMP_STRIDE * DH)
    pe = cmp_pe.reshape(2, 2, CMP_STRIDE * DH)
    kc, vc = _compress(tk, tv, cmp_w1.astype(BF16), cmp_w2.astype(BF16), pe)
    y_a = _nsa(ga, gb, gc, gd_t, kc, vc, ov_t, batch, seq)

    bias = jnp.concatenate([i_bias, f_bias])
    bias_col = jnp.zeros((1, LANES), F32).at[0, GD_I:GD_I + 2 * NH].set(bias)
    bias_row = bias[:, None]
    y_b = _mlstm(gb, gc, gd, gd_t, bias_col, bias_row, norm_g[None, :], tril, triu, batch, seq)

    y_c = _lru(gc, conv_w, conv_b[None, :], gate_w.astype(BF16), gate_b, lam[None, :], batch, seq)

    y_d = _moba(ga, gb, gc, batch, seq)

    return _out_proj((y_a, y_b, y_c, y_d), w_out.astype(BF16), x, ln_g[None, :], ln_b[None, :])


def kernel(x, w_in, nsa_cmp_w1, nsa_cmp_w2, nsa_cmp_pe, mlstm_i_bias, mlstm_f_bias, mlstm_norm_g,
           lru_conv_w, lru_conv_b, lru_gate_w, lru_gate_b, lru_lambda, w_out, ln_g, ln_b):
    batch, seq, d = x.shape
    tril, triu = _chunk_tri(MLSTM_ROWS)
    consts = (_rope_tables(seq), _cmp_overlap_t(seq), tril, triu)
    xf = x.reshape(batch * seq, d)
    xb = xf.astype(BF16)
    w_all = _regroup_w_in(w_in)
    for l in range(w_in.shape[0]):
        xf, xb = _layer(xf, xb, batch, seq, consts, w_all, l, nsa_cmp_w1[l], nsa_cmp_w2[l],
                        nsa_cmp_pe[l], mlstm_i_bias[l], mlstm_f_bias[l], mlstm_norm_g[l], lru_conv_w[l],
                        lru_conv_b[l], lru_gate_w[l], lru_gate_b[l], lru_lambda[l], w_out[l], ln_g[l], ln_b[l])
    return xf.reshape(batch, seq, d)
```

```python
import functools
import math

import jax
import jax.numpy as jnp
from jax import lax
from jax.experimental import pallas as pl
from jax.experimental.pallas import tpu as pltpu

F32 = jnp.float32
BF16 = jnp.bfloat16

DEPTH = 2
GROUP = 512
DH = 128
NH = 4
ROT_DIM = 32
ROPE_THETA = 500000.0

CMP_BLOCK = 32
CMP_STRIDE = 16
SEL_BLOCK = 64
SEL_TOPK = 8
WIN = 256
FORCE_SCORE = 1e9

MLSTM_CHUNK = 512
LRU_C = 8.0
MOBA_BLOCK = 256
MOBA_TOPK = 3

DEEPNORM_ALPHA = (2 * DEPTH) ** 0.25
NEG = -1e30
LN_EPS = 1e-5
SCALE = DH ** -0.5
EXP2_SCALE = SCALE * math.log2(math.e)

LANES = 128
SUB = 8
VMEM_LIMIT = 56 * 1024 * 1024

PROJ_ROWS = 2048
PROJ_COLS = 512
REGROUP_COLS = 256
NSA_ROWS = 256
NSA_SEL_ROWS = 32
MOBA_CAND_ROWS = SUB
LRU_ROWS = 512
MLSTM_ROWS = 2 * MLSTM_CHUNK
OUT_ROWS = 512

_OFF = {}
_o = 0
for _name, _w in (('nsa_q', 512), ('nsa_kv', 768), ('nsa_gate', 12), ('nsa_z', 512), ('mlstm_qkv', 1536),
                  ('mlstm_if', 8), ('mlstm_o', 512), ('mlstm_z', 512), ('lru_x', 512), ('lru_z', 512),
                  ('moba_qkv', 1536), ('moba_z', 512)):
    _OFF[_name] = _o
    _o += _w

GD_I = 16
GD_F = 20
GC0 = 5
GD_COL = 7 * 128

NT = (((1,), (1,)), ((), ()))
TN = (((0,), (0,)), ((), ()))


def _cparams(sem):
    return pltpu.CompilerParams(dimension_semantics=sem, vmem_limit_bytes=VMEM_LIMIT)


def _dot(a, b):
    return jnp.dot(a, b, preferred_element_type=F32)


def _dot_nt(a, b):
    return lax.dot_general(a, b, NT, preferred_element_type=F32)


def _split_hi_lo(a):
    hi = a.astype(BF16)
    lo = (a - hi.astype(F32)).astype(BF16)
    return hi, lo


def _sigmoid(x):
    return 0.5 * jnp.tanh(0.5 * x) + 0.5


def _silu(x):
    return x * _sigmoid(x)


def _log_sigmoid(x):
    return jnp.minimum(x, 0.0) - jnp.log(1.0 + jnp.exp(-jnp.abs(x)))


def _proj_kernel(x_ref, w_ref, cos_ref, sin_ref, swap_ref, ga_ref, gbc_ref, gd_ref, gdt_ref, *, n_rot):
    j = pl.program_id(1)
    acc = _dot_nt(x_ref[...], w_ref[...])
    gbc_ref[...] = acc.astype(gbc_ref.dtype)

    @pl.when(j < n_rot)
    def _():
        c = cos_ref[...]
        s = sin_ref[...]
        for h in range(acc.shape[1] // DH):
            t = acc[:, h * DH:(h + 1) * DH]
            swapped = _dot(t.astype(BF16), swap_ref[...])
            ga_ref[:, h * DH:(h + 1) * DH] = (t * c + swapped * s).astype(ga_ref.dtype)

    @pl.when(j == GD_COL // acc.shape[1])
    def _():
        gd = acc[:, GD_COL % acc.shape[1]:GD_COL % acc.shape[1] + LANES]
        gd_ref[...] = gd
        gdt_ref[...] = jnp.transpose(gd)


def _project(xb, w_all, layer, rope, seq, tm, tn):
    m, k = xb.shape
    widths = [sum(p[2] for p in pieces) for pieces in _GROUPS]
    assert tn == GROUP and widths[1] == GC0 * GROUP
    n_rot = widths[0] // tn
    nrep = seq // tm
    rope_spec = pl.BlockSpec((tm, DH), lambda i, j: (i % nrep, 0))
    cos, sin, swap = rope
    return pl.pallas_call(
        functools.partial(_proj_kernel, n_rot=n_rot),
        out_shape=(jax.ShapeDtypeStruct((m, widths[0]), BF16), jax.ShapeDtypeStruct((m, widths[1] + widths[2]), BF16),
                   jax.ShapeDtypeStruct((m, LANES), F32), jax.ShapeDtypeStruct((LANES, m), F32)),
        grid=(m // tm, sum(widths) // tn),
        in_specs=[pl.BlockSpec((tm, k), lambda i, j: (i, 0)), pl.BlockSpec((None, tn, k), lambda i, j: (layer, j, 0)),
                  rope_spec, rope_spec, pl.BlockSpec((DH, DH), lambda i, j: (0, 0))],
        out_specs=(pl.BlockSpec((tm, tn), lambda i, j: (i, jnp.minimum(j, n_rot - 1))),
                   pl.BlockSpec((tm, tn), lambda i, j: (i, jnp.maximum(j - n_rot, 0))),
                   pl.BlockSpec((tm, LANES), lambda i, j: (i, 0)), pl.BlockSpec((LANES, tm), lambda i, j: (0, i))),
        compiler_params=_cparams(("parallel", "arbitrary")),
        name="in_proj",
    )(xb, w_all, cos, sin, swap)


def _compress_kernel(tk_ref, tv_ref, w1_ref, w2_ref, pe_ref, kc_ref, vc_ref, *, nb):
    half = (CMP_BLOCK // 2) * DH
    for idx, (t_ref, o_ref) in enumerate(((tk_ref, kc_ref), (tv_ref, vc_ref))):
        t = t_ref[...].astype(F32)
        lo = (t + pe_ref[idx, 0:1, :]).astype(BF16)
        hi = (t + pe_ref[idx, 1:2, :]).astype(BF16)
        a = _dot(lo, w1_ref[idx, 0:half, :])
        b = _dot(hi, w1_ref[idx, half:2 * half, :])
        hid = _silu(a + pltpu.roll(b, nb - 1, 0))
        out = _dot(hid.astype(BF16), w2_ref[idx])
        o_ref[...] = jnp.zeros(o_ref.shape, o_ref.dtype)
        o_ref[0:nb, :] = out.astype(o_ref.dtype)


def _compress(tk, tv, w1, w2, pe):
    b, nb, width = tk.shape
    blk = pl.BlockSpec((None, nb, width), lambda i: (i, 0, 0))
    full = lambda a: pl.BlockSpec(a.shape, lambda i: (0,) * a.ndim)
    out = jax.ShapeDtypeStruct((b, LANES, DH), BF16)
    return pl.pallas_call(
        functools.partial(_compress_kernel, nb=nb),
        out_shape=(out, out),
        grid=(b,),
        in_specs=[blk, blk, full(w1), full(w2), full(pe)],
        out_specs=(pl.BlockSpec((None, LANES, DH), lambda i: (i, 0, 0)),) * 2,
        compiler_params=_cparams(("parallel",)),
        name="nsa_compress",
    )(tk, tv, w1, w2, pe)


def _topk_rows(score, k_top, n_live):
    rowi = lax.broadcasted_iota(jnp.int32, score.shape, 0)
    rank = jnp.zeros(score.shape, F32)
    for k in range(n_live):
        sk = score[k:k + 1, :]
        beats = jnp.logical_or(sk > score, jnp.logical_and(sk == score, rowi > k))
        rank = rank + jnp.where(beats, 1.0, 0.0)
    return rank < k_top


def _prescale(q):
    return (q.astype(F32) * EXP2_SCALE).astype(q.dtype)


def _softmax_rows(s):
    p = jnp.exp2(s - jnp.max(s, axis=0, keepdims=True))
    return p.astype(BF16), 1.0 / jnp.sum(p, axis=0, keepdims=True)


def _nsa_tile(i, q_ref, kc_ref, vc_ref, ks_ref, kw_ref, gt_ref, z_ref, ovt_ref, diagb_ref, winb_ref,
              o_ref, vst_sc, vwt_sc, *, seq, tq):
    width = NH * tq
    n_cmp = seq // CMP_STRIDE - 1
    n_sel = seq // SEL_BLOCK
    n_rows = NSA_SEL_ROWS
    q = q_ref[...]
    qs = jnp.concatenate([q[:, h * DH:(h + 1) * DH] for h in range(NH)], axis=0)
    qs = _prescale(qs)

    n_c = min(LANES, (i + 1) * tq // CMP_STRIDE)
    rown = lax.broadcasted_iota(jnp.int32, (n_c, width), 0)
    tok = i * tq + (lax.broadcasted_iota(jnp.int32, (n_c, width), 1) & (tq - 1))
    mask_c = jnp.logical_and(rown * CMP_STRIDE + (CMP_BLOCK - 1) <= tok, rown < n_cmp)
    s = jnp.where(mask_c, _dot_nt(kc_ref[0:n_c, :], qs), NEG)
    e = jnp.exp2(s - jnp.max(s, axis=0, keepdims=True))
    p_c = jnp.where(mask_c, e, 0.0) * (1.0 / jnp.sum(e, axis=0, keepdims=True))
    o_c = lax.dot_general(vc_ref[0:n_c, :], p_c.astype(BF16), TN, preferred_element_type=F32)

    p_sum = p_c[:, 0:tq]
    for h in range(1, NH):
        p_sum = p_sum + p_c[:, h * tq:(h + 1) * tq]
    if n_c < LANES:
        p_sum = jnp.concatenate([p_sum, jnp.zeros((LANES - n_c, tq), F32)], axis=0)
    p_hi, p_lo = _split_hi_lo(p_sum)
    imp = (_dot(ovt_ref[...], p_hi) + _dot(ovt_ref[...], p_lo))[0:n_rows]
    rowj = lax.broadcasted_iota(jnp.int32, (n_rows, tq), 0)
    t_q = i * tq + lax.broadcasted_iota(jnp.int32, (n_rows, tq), 1)
    cur = t_q >> (SEL_BLOCK.bit_length() - 1)
    forced = jnp.logical_or(rowj == 0, jnp.logical_or(rowj == cur, rowj == cur - 1))
    valid = rowj * SEL_BLOCK <= t_q
    score = jnp.where(forced, FORCE_SCORE, jnp.where(valid, imp, NEG))
    score = jnp.where(rowj < n_sel, score, -jnp.inf)
    nk = (i + 1) * tq
    n_live = nk // SEL_BLOCK
    live_rows = -(-n_live // SUB) * SUB
    bias = jnp.where(_topk_rows(score[0:live_rows], min(SEL_TOPK, n_sel), n_live), 0.0, NEG)
    bias = jnp.concatenate([bias] * NH, axis=1)

    s = _dot_nt(ks_ref[0:nk, :], qs)
    parts = [s[j * SEL_BLOCK:(j + 1) * SEL_BLOCK] + bias[j:j + 1] for j in range(nk // SEL_BLOCK)]
    own = jnp.concatenate(parts[i * tq // SEL_BLOCK:], axis=0) + diagb_ref[...]
    p, inv = _softmax_rows(jnp.concatenate(parts[:i * tq // SEL_BLOCK] + [own], axis=0))
    o_s = _dot(vst_sc[:, 0:nk], p) * inv

    n_wblk = winb_ref.shape[0]
    blocks = [b for b in range(i - n_wblk + 1, i + 1) if b >= 0]
    parts = [_dot_nt(kw_ref[b * tq:(b + 1) * tq, :], qs) + winb_ref[b - i + n_wblk - 1] for b in blocks]
    p, inv = _softmax_rows(jnp.concatenate(parts, axis=0))
    o_w = _dot(vwt_sc[:, blocks[0] * tq:nk], p) * inv

    g = _sigmoid(gt_ref[0:GD_I, :])
    for h in range(NH):
        ls = slice(h * tq, (h + 1) * tq)
        cs = slice(h * DH, (h + 1) * DH)
        mix = (g[3 * h:3 * h + 1] * o_c[:, ls] + g[3 * h + 1:3 * h + 2] * o_s[:, ls]
               + g[3 * h + 2:3 * h + 3] * o_w[:, ls])
        o_ref[:, cs] = (jnp.transpose(mix) * _silu(z_ref[:, cs].astype(F32))).astype(o_ref.dtype)


def _nsa_kernel(q_ref, kc_ref, vc_ref, ks_ref, vs_ref, kw_ref, vw_ref, gt_ref, z_ref, ovt_ref, diagb_ref,
                winb_ref, o_ref, vst_sc, vwt_sc, *, seq, tq):
    i = pl.program_id(1)

    @pl.when(i == 0)
    def _():
        for c in range(seq // tq):
            cols = slice(c * tq, (c + 1) * tq)
            vst_sc[:, cols] = jnp.transpose(vs_ref[cols, :].astype(F32)).astype(BF16)
            vwt_sc[:, cols] = jnp.transpose(vw_ref[cols, :].astype(F32)).astype(BF16)

    for c in range(seq // tq):
        pl.when(i == c)(functools.partial(
            _nsa_tile, c, q_ref, kc_ref, vc_ref, ks_ref, kw_ref, gt_ref, z_ref, ovt_ref, diagb_ref, winb_ref,
            o_ref, vst_sc, vwt_sc, seq=seq, tq=tq))


def _nsa_bias_tables(tq):
    t = jnp.arange(NH * tq)[None, :] % tq
    r = jnp.arange(tq)[:, None]
    diag = jnp.where(r <= t, 0.0, NEG)
    n_wblk = WIN // tq + 1
    win = []
    for b in range(n_wblk):
        diff = t - (r + (b - (n_wblk - 1)) * tq)
        win.append(jnp.where((diff >= 0) & (diff < WIN), 0.0, NEG))
    return diag.astype(F32), jnp.stack(win).astype(F32)


def _nsa(ga, gb, gc, gd_t, kc, vc, ov_t, batch, seq):
    tq = NSA_ROWS
    nq = seq // tq
    assert seq // SEL_BLOCK <= NSA_SEL_ROWS and tq % SEL_BLOCK == 0
    row = lambda b, i: (b * nq + i, 0)
    kv = lambda col: pl.BlockSpec((seq, DH), lambda b, i: (b, col))
    cmp_spec = pl.BlockSpec((None, LANES, DH), lambda b, i: (b, 0, 0))
    full = lambda a: pl.BlockSpec(a.shape, lambda b, i: (0,) * a.ndim)
    diag_b, win_b = _nsa_bias_tables(tq)
    return pl.pallas_call(
        functools.partial(_nsa_kernel, seq=seq, tq=tq),
        out_shape=jax.ShapeDtypeStruct((batch * seq, GROUP), BF16),
        grid=(batch, nq),
        in_specs=[pl.BlockSpec((tq, GROUP), row), cmp_spec, cmp_spec,
                  kv(5), kv(1), kv(6), kv(2),
                  pl.BlockSpec((LANES, tq), lambda b, i: (0, b * nq + i)),
                  pl.BlockSpec((tq, GROUP), lambda b, i: (b * nq + i, GC0)), full(ov_t), full(diag_b), full(win_b)],
        out_specs=pl.BlockSpec((tq, GROUP), row),
        scratch_shapes=[pltpu.VMEM((DH, seq), BF16), pltpu.VMEM((DH, seq), BF16)],
        compiler_params=_cparams(("parallel", "arbitrary")),
        name="nsa_attention",
    )(ga, kc, vc, ga, gb, ga, gb, gd_t, gc, ov_t, diag_b, win_b)


def _mlstm_kernel(q_ref, k_ref, v_ref, gcol_ref, grow_ref, bcol_ref, brow_ref, og_ref, z_ref, ng_ref,
                  tril_ref, triu_ref, out_ref, cn_sc, m_sc, *, tc):
    L = MLSTM_CHUNK

    @pl.when(pl.program_id(1) == 0)
    def _():
        cn_sc[...] = jnp.zeros(cn_sc.shape, F32)
        m_sc[...] = jnp.full(m_sc.shape, NEG, F32)

    gcol = gcol_ref[...] + bcol_ref[...]
    grow = grow_ref[...] + brow_ref[...]
    hi, lo = _split_hi_lo(_log_sigmoid(gcol))
    bcol_all = _dot(tril_ref[...], hi) + _dot(tril_ref[...], lo)
    hi, lo = _split_hi_lo(_log_sigmoid(grow))
    brow_all = _dot(hi, triu_ref[...]) + _dot(lo, triu_ref[...])

    causal = (lax.broadcasted_iota(jnp.int32, (L, L), 1) <= lax.broadcasted_iota(jnp.int32, (L, L), 0))
    ones = jnp.ones((L, DH), BF16)
    mean_mat = jnp.full((DH, DH), 1.0 / DH, BF16)

    def lane_mean(a):
        a_hi, a_lo = _split_hi_lo(a)
        return _dot(a_hi, mean_mat) + _dot(a_lo, mean_mat)

    for c in range(tc // L):
        rs = slice(c * L, (c + 1) * L)
        for h in range(NH):
            cs = slice(h * DH, (h + 1) * DH)
            qh = q_ref[rs, cs]
            kh = k_ref[rs, cs]
            v_aug = jnp.concatenate([v_ref[rs, cs], ones], axis=1)
            ig_col = gcol[rs, GD_I + h:GD_I + h + 1]
            b_col = bcol_all[rs, GD_F + h:GD_F + h + 1]
            ig_row = grow[h:h + 1, rs]
            b_row = brow_all[NH + h:NH + h + 1, rs]
            cn_prev = cn_sc[h]
            m_prev = m_sc[h][:, 0:1]

            d_log = jnp.where(causal, b_col + (ig_row - b_row), NEG)
            m_intra = jnp.max(d_log, axis=-1, keepdims=True)
            m_inter = b_col + m_prev
            m_t = jnp.maximum(m_inter, m_intra)
            w_inter = jnp.exp(m_inter - m_t)
            qk = _dot_nt(qh, kh) * jnp.exp(d_log - (m_t - math.log(SCALE)))
            intra = _dot(qk.astype(BF16), v_aug)
            inter = _dot(qh, cn_prev.astype(BF16))
            num = intra[:, 0:DH] + w_inter * inter[:, 0:DH]
            den = intra[:, DH:2 * DH] + w_inter * inter[:, DH:2 * DH]
            hh = num / jnp.maximum(jnp.abs(den), jnp.exp(-m_t))
            hh = _sigmoid(og_ref[rs, cs].astype(F32)) * hh
            dlt = hh - lane_mean(hh)
            hn = dlt * lax.rsqrt(lane_mean(jnp.square(dlt)) + LN_EPS) * ng_ref[:, cs]
            out_ref[rs, cs] = (hn * _silu(z_ref[rs, cs].astype(F32))).astype(out_ref.dtype)

            b_last = b_row[:, L - 1:L]
            m_loc = jnp.max(b_last - b_row + ig_row, axis=-1, keepdims=True)
            e_col = jnp.exp(b_last - b_col + ig_col - m_loc)
            ek = ((e_col * SCALE) * kh.astype(F32)).astype(BF16)
            g_cn = lax.dot_general(ek, v_aug, TN, preferred_element_type=F32)
            m_new = jnp.maximum(b_last + m_prev, m_loc)
            cn_sc[h] = jnp.exp(b_last + m_prev - m_new) * cn_prev + jnp.exp(m_loc - m_new) * g_cn
            m_sc[h] = jnp.broadcast_to(m_new, (1, LANES))


def _mlstm(gb, gc, gd, gd_t, bias_col, bias_row, norm_g, tril, triu, batch, seq):
    tc = MLSTM_ROWS
    nt = seq // tc
    row = lambda col: (lambda b, j: (b * nt + j, col))
    full = lambda a: pl.BlockSpec(a.shape, lambda b, j: (0,) * a.ndim)
    return pl.pallas_call(
        functools.partial(_mlstm_kernel, tc=tc),
        out_shape=jax.ShapeDtypeStruct((batch * seq, GROUP), BF16),
        grid=(batch, nt),
        in_specs=[pl.BlockSpec((tc, GROUP), row(1)), pl.BlockSpec((tc, GROUP), row(2)),
                  pl.BlockSpec((tc, GROUP), row(3)),
                  pl.BlockSpec((tc, LANES), row(0)),
                  pl.BlockSpec((2 * NH, tc), lambda b, j: (GD_I // (2 * NH), b * nt + j)),
                  full(bias_col), full(bias_row),
                  pl.BlockSpec((tc, GROUP), row(GC0 + 1)), pl.BlockSpec((tc, GROUP), row(GC0 + 2)),
                  full(norm_g), full(tril), full(triu)],
        out_specs=pl.BlockSpec((tc, GROUP), row(0)),
        scratch_shapes=[pltpu.VMEM((NH, DH, 2 * DH), F32), pltpu.VMEM((NH, 1, LANES), F32)],
        compiler_params=_cparams(("parallel", "arbitrary")),
        name="mlstm",
    )(gb, gb, gb, gd, gd_t, bias_col, bias_row, gc, gc, norm_g, tril, triu)


def _lru_kernel(x_ref, z_ref, cw_ref, cb_ref, gw_ref, gb_ref, lam_ref, out_ref, xbuf, h_sc, *, tl):
    pad = 8

    @pl.when(pl.program_id(1) == 0)
    def _():
        xbuf[0:pad, :] = jnp.zeros((pad, GROUP), F32)
        h_sc[...] = jnp.zeros(h_sc.shape, F32)

    x = x_ref[...].astype(F32)
    xbuf[pad:pad + tl, :] = x
    u = cw_ref[3:4, :] * x + cb_ref[...]
    for w in range(3):
        u = u + cw_ref[w:w + 1, :] * xbuf[pad - 3 + w:pad - 3 + w + tl, :]
    xbuf[0:pad, :] = x[tl - pad:tl, :]

    ub = u.astype(BF16)
    pre = []
    for gi in range(2):
        pre.append(jnp.concatenate(
            [_dot(ub[:, n * DH:(n + 1) * DH], gw_ref[gi, n]) for n in range(NH)], axis=1) + gb_ref[gi:gi + 1, :])
    r = _sigmoid(pre[0])
    ig = _sigmoid(pre[1])
    neg_lam = -lam_ref[...]
    softplus = jnp.maximum(neg_lam, 0.0) + jnp.log(1.0 + jnp.exp(-jnp.abs(neg_lam)))
    a = jnp.exp(-LRU_C * r * softplus)
    bx = jnp.sqrt(1.0 - a * a) * (ig * u)

    sub = lax.broadcasted_iota(jnp.int32, (tl, GROUP), 0) & (SUB - 1)
    d = 1
    while d < SUB:
        keep = sub >= d
        a_sh = jnp.where(keep, pltpu.roll(a, d, 0), 1.0)
        b_sh = jnp.where(keep, pltpu.roll(bx, d, 0), 0.0)
        bx = a * b_sh + bx
        a = a * a_sh
        d *= 2
    carry = jnp.broadcast_to(h_sc[...], (SUB, GROUP))
    groups = []
    for g in range(tl // SUB):
        hg = a[g * SUB:(g + 1) * SUB] * carry + bx[g * SUB:(g + 1) * SUB]
        groups.append(hg)
        carry = jnp.broadcast_to(hg[SUB - 1:SUB], (SUB, GROUP))
    hseq = jnp.concatenate(groups, axis=0)
    h_sc[...] = hseq[tl - 1:tl, :]
    out_ref[...] = (hseq * _silu(z_ref[...].astype(F32))).astype(out_ref.dtype)


def _lru(gc, conv_w, conv_b, gate_w, gate_b, lam, batch, seq):
    tl = LRU_ROWS
    nt = seq // tl
    row = lambda col: (lambda b, j: (b * nt + j, col))
    full = lambda a: pl.BlockSpec(a.shape, lambda b, j: (0,) * a.ndim)
    return pl.pallas_call(
        functools.partial(_lru_kernel, tl=tl),
        out_shape=jax.ShapeDtypeStruct((batch * seq, GROUP), BF16),
        grid=(batch, nt),
        in_specs=[pl.BlockSpec((tl, GROUP), row(GC0 + 3)), pl.BlockSpec((tl, GROUP), row(GC0 + 4)),
                  full(conv_w), full(conv_b), full(gate_w), full(gate_b), full(lam)],
        out_specs=pl.BlockSpec((tl, GROUP), row(0)),
        scratch_shapes=[pltpu.VMEM((tl + 8, GROUP), F32), pltpu.VMEM((1, GROUP), F32)],
        compiler_params=_cparams(("parallel", "arbitrary")),
        name="rglru",
    )(gc, gc, conv_w, conv_b, gate_w, gate_b, lam)


def _moba_tile(i, q_ref, k_ref, z_ref, o_ref, kmean_sc, vt_sc, *, seq):
    tq = MOBA_BLOCK
    nb = seq // MOBA_BLOCK
    ncand = MOBA_CAND_ROWS
    nk = (i + 1) * tq
    past = lax.broadcasted_iota(jnp.int32, (ncand, tq), 0) < i
    pw = 2 * tq
    causal = (lax.broadcasted_iota(jnp.int32, (tq, pw), 0)
              <= (lax.broadcasted_iota(jnp.int32, (tq, pw), 1) & (tq - 1)))
    zeros = jnp.zeros((tq, DH), BF16)
    for h0 in range(0, NH, 2):
        pair = slice(h0 * DH, (h0 + 2) * DH)
        qa = q_ref[:, h0 * DH:(h0 + 1) * DH]
        qb = q_ref[:, (h0 + 1) * DH:(h0 + 2) * DH]
        q_pair = jnp.concatenate([jnp.concatenate([_prescale(qa), zeros], axis=1),
                                  jnp.concatenate([zeros, _prescale(qb)], axis=1)], axis=0)
        s = _dot_nt(k_ref[0:nk, pair], q_pair)
        parts = [jnp.where(causal, s[i * tq:nk], NEG)]
        if i > 0:
            bias = []
            for h, qh in ((h0, qa), (h0 + 1, qb)):
                gate = _dot_nt(kmean_sc[h].astype(BF16), qh)[0:ncand]
                score = jnp.where(past, gate, NEG)
                sel = jnp.logical_and(_topk_rows(score, min(MOBA_TOPK, nb), i), past)
                bias.append(jnp.where(sel, 0.0, NEG))
            bias = jnp.concatenate(bias, axis=1)
            parts = [s[j * tq:(j + 1) * tq] + bias[j:j + 1] for j in range(i)] + parts
        p, inv = _softmax_rows(jnp.concatenate(parts, axis=0))
        for n, h in enumerate((h0, h0 + 1)):
            cs = slice(h * DH, (h + 1) * DH)
            ls = slice(n * tq, (n + 1) * tq)
            o = jnp.transpose(_dot(vt_sc[h, :, 0:nk], p[:, ls]) * inv[:, ls])
            o_ref[:, cs] = (o * _silu(z_ref[:, cs].astype(F32))).astype(o_ref.dtype)


def _moba_kernel(q_ref, k_ref, v_ref, z_ref, o_ref, kmean_sc, vt_sc, *, seq):
    i = pl.program_id(1)
    tq = MOBA_BLOCK
    nb = seq // MOBA_BLOCK

    @pl.when(i == 0)
    def _():
        kmean_sc[...] = jnp.zeros(kmean_sc.shape, F32)
        for h in range(NH):
            cs = slice(h * DH, (h + 1) * DH)
            kmean_sc[h, 0:nb, :] = jnp.mean(k_ref[:, cs].astype(F32).reshape(nb, MOBA_BLOCK, DH), axis=1)
            for c in range(nb):
                rows = slice(c * tq, (c + 1) * tq)
                vt_sc[h, :, rows] = jnp.transpose(v_ref[rows, cs].astype(F32)).astype(BF16)

    for c in range(nb):
        pl.when(i == c)(functools.partial(_moba_tile, c, q_ref, k_ref, z_ref, o_ref, kmean_sc, vt_sc, seq=seq))


def _moba(ga, gb, gc, batch, seq):
    tq = MOBA_BLOCK
    nq = seq // tq
    assert nq <= MOBA_CAND_ROWS
    row = lambda col: (lambda b, i: (b * nq + i, col))
    return pl.pallas_call(
        functools.partial(_moba_kernel, seq=seq),
        out_shape=jax.ShapeDtypeStruct((batch * seq, GROUP), BF16),
        grid=(batch, nq),
        in_specs=[pl.BlockSpec((tq, GROUP), row(2)),
                  pl.BlockSpec((seq, GROUP), lambda b, i: (b, 3)),
                  pl.BlockSpec((seq, GROUP), lambda b, i: (b, 4)),
                  pl.BlockSpec((tq, GROUP), row(GC0 + 5))],
        out_specs=pl.BlockSpec((tq, GROUP), row(0)),
        scratch_shapes=[pltpu.VMEM((NH, LANES, DH), F32), pltpu.VMEM((NH, DH, seq), BF16)],
        compiler_params=_cparams(("parallel", "arbitrary")),
        name="moba",
    )(ga, ga, gb, gc)


def _out_kernel(ya_ref, yb_ref, yc_ref, yd_ref, w_ref, x_ref, g_ref, b_ref, o_ref, ob_ref):
    half = o_ref.shape[0] // 2
    for r in range(2):
        rows = slice(r * half, (r + 1) * half)
        acc = DEEPNORM_ALPHA * x_ref[rows, :]
        for p, y_ref in enumerate((ya_ref, yb_ref, yc_ref, yd_ref)):
            acc = acc + _dot(y_ref[rows, :], w_ref[p * GROUP:(p + 1) * GROUP, :])
        mu = jnp.mean(acc, axis=-1, keepdims=True)
        var = jnp.mean(jnp.square(acc - mu), axis=-1, keepdims=True)
        y = (acc - mu) * lax.rsqrt(var + LN_EPS) * g_ref[...] + b_ref[...]
        o_ref[rows, :] = y
        ob_ref[rows, :] = y.astype(BF16)


def _out_proj(ys, w_out, x, ln_g, ln_b):
    m, d = x.shape
    tm = OUT_ROWS
    yspec = pl.BlockSpec((tm, GROUP), lambda i: (i, 0))
    full = lambda a: pl.BlockSpec(a.shape, lambda i: (0,) * a.ndim)
    xspec = pl.BlockSpec((tm, d), lambda i: (i, 0))
    return pl.pallas_call(
        _out_kernel,
        out_shape=(jax.ShapeDtypeStruct((m, d), F32), jax.ShapeDtypeStruct((m, d), BF16)),
        grid=(m // tm,),
        in_specs=[yspec] * 4 + [full(w_out), xspec, full(ln_g), full(ln_b)],
        out_specs=(xspec, xspec),
        compiler_params=_cparams(("parallel",)),
        name="out_proj_ln",
    )(*ys, w_out, x, ln_g, ln_b)


_GROUPS = (
    (('nsa_q', 0, 512), ('nsa_kv', 0, 384), ('gates', 0, LANES), ('moba_qkv', 0, 1024)),
    (('nsa_kv', 384, 384), (None, 0, 128), ('mlstm_qkv', 0, 1536), ('moba_qkv', 1024, 512)),
    (('nsa_z', 0, 512), ('mlstm_o', 0, 512), ('mlstm_z', 0, 512), ('lru_x', 0, 512), ('lru_z', 0, 512),
     ('moba_z', 0, 512)),
)
_GATE_PIECES = (('nsa_gate', 0, 12), (None, 0, 4), ('mlstm_if', 0, 8), (None, 0, LANES - 24))


def _copy_rows(w_ref, pieces, o_ref, dst):
    for name, off, width in pieces:
        if name is None:
            o_ref[dst:dst + width, :] = jnp.zeros((width, o_ref.shape[1]), o_ref.dtype)
        else:
            src = _OFF[name] + off
            o_ref[dst:dst + width, :] = w_ref[src:src + width, :].astype(o_ref.dtype)
        dst += width


def _regroup_kernel(w_ref, o_ref, gate_sc):
    dst = 0
    for piece in sum(_GROUPS, ()):
        if piece[0] == 'gates':
            _copy_rows(w_ref, _GATE_PIECES, gate_sc, 0)
            o_ref[dst:dst + LANES, :] = gate_sc[...].astype(o_ref.dtype)
        else:
            _copy_rows(w_ref, (piece,), o_ref, dst)
        dst += piece[2]


def _regroup_w_in(w_in):
    depth, d, n_in = w_in.shape
    wt = jnp.transpose(w_in, (0, 2, 1))
    tk = REGROUP_COLS
    n_all = sum(p[2] for pieces in _GROUPS for p in pieces)
    return pl.pallas_call(
        _regroup_kernel,
        out_shape=jax.ShapeDtypeStruct((depth, n_all, d), BF16),
        grid=(depth, d // tk),
        in_specs=[pl.BlockSpec((None, n_in, tk), lambda l, i: (l, 0, i))],
        out_specs=pl.BlockSpec((None, n_all, tk), lambda l, i: (l, 0, i)),
        scratch_shapes=[pltpu.VMEM((LANES, tk), F32)],
        compiler_params=_cparams(("parallel", "parallel")),
        name="regroup_w_in",
    )(wt)


def _rope_tables(seq):
    half = ROT_DIM // 2
    inv_freq = jnp.power(ROPE_THETA, -jnp.arange(half, dtype=F32) * (2.0 / ROT_DIM))
    ang = jnp.arange(seq, dtype=jnp.int32).astype(F32)[:, None] * inv_freq[None, :]
    cos, sin = jnp.cos(ang), jnp.sin(ang)
    ones = jnp.ones((seq, DH - ROT_DIM), F32)
    lane = jnp.arange(DH)
    partner = jnp.where(lane < half, lane + half, lane - half)
    swap = (lane[:, None] == partner[None, :]).astype(BF16)
    return (jnp.concatenate([cos, cos, ones], axis=1),
            jnp.concatenate([-sin, sin, 0.0 * ones], axis=1), swap)


def _cmp_overlap_t(seq):
    n_cmp = seq // CMP_STRIDE - 1
    n_sel = seq // SEL_BLOCK
    cs = jnp.arange(LANES)[None, :] * CMP_STRIDE
    ss = jnp.arange(LANES)[:, None] * SEL_BLOCK
    ov = (cs < ss + SEL_BLOCK) & (cs + CMP_BLOCK > ss)
    ov = ov & (jnp.arange(LANES)[None, :] < n_cmp) & (jnp.arange(LANES)[:, None] < n_sel)
    return ov.astype(BF16)


def _chunk_tri(tc):
    r = jnp.arange(tc)
    same = (r[:, None] // MLSTM_CHUNK) == (r[None, :] // MLSTM_CHUNK)
    tril = (same & (r[None, :] <= r[:, None])).astype(BF16)
    return tril, tril.T


def _layer(x, xb, batch, seq, consts, w_all, layer, cmp_w1, cmp_w2, cmp_pe, i_bias, f_bias, norm_g,
           conv_w, conv_b, gate_w, gate_b, lam, w_out, ln_g, ln_b):
    rope, ov_t, tril, triu = consts
    ga, gbc, gd, gd_t = _project(xb, w_all, layer, rope, seq, min(PROJ_ROWS, seq), PROJ_COLS)
    gb = gc = gbc

    nb = seq // CMP_STRIDE
    tk = ga[:, 4 * DH:5 * DH].reshape(batch, nb, CMP_STRIDE * DH)
    tv = gb[:, 0:DH].reshape(batch, nb, CMP_STRIDE * DH)
    pe = cmp_pe.reshape(2, 2, CMP_STRIDE * DH)
    kc, vc = _compress(tk, tv, cmp_w1.astype(BF16), cmp_w2.astype(BF16), pe)
    y_a = _nsa(ga, gb, gc, gd_t, kc, vc, ov_t, batch, seq)

    bias = jnp.concatenate([i_bias, f_bias])
    bias_col = jnp.zeros((1, LANES), F32).at[0, GD_I:GD_I + 2 * NH].set(bias)
    bias_row = bias[:, None]
    y_b = _mlstm(gb, gc, gd, gd_t, bias_col, bias_row, norm_g[None, :], tril, triu, batch, seq)

    y_c = _lru(gc, conv_w, conv_b[None, :], gate_w.astype(BF16), gate_b, lam[None, :], batch, seq)

    y_d = _moba(ga, gb, gc, batch, seq)

    return _out_proj((y_a, y_b, y_c, y_d), w_out.astype(BF16), x, ln_g[None, :], ln_b[None, :])


def kernel(x, w_in, nsa_cmp_w1, nsa_cmp_w2, nsa_cmp_pe, mlstm_i_bias, mlstm_f_bias, mlstm_norm_g,
           lru_conv_w, lru_conv_b, lru_gate_w, lru_gate_b, lru_lambda, w_out, ln_g, ln_b):
    batch, seq, d = x.shape
    tril, triu = _chunk_tri(MLSTM_ROWS)
    consts = (_rope_tables(seq), _cmp_overlap_t(seq), tril, triu)
    xf = x.reshape(batch * seq, d)
    xb = xf.astype(BF16)
    w_all = _regroup_w_in(w_in)
    for l in range(w_in.shape[0]):
        xf, xb = _layer(xf, xb, batch, seq, consts, w_all, l, nsa_cmp_w1[l], nsa_cmp_w2[l],
                        nsa_cmp_pe[l], mlstm_i_bias[l], mlstm_f_bias[l], mlstm_norm_g[l], lru_conv_w[l],
                        lru_conv_b[l], lru_gate_w[l], lru_gate_b[l], lru_lambda[l], w_out[l], ln_g[l], ln_b[l])
    return xf.reshape(batch, seq, d)
```
